```python
import math
import jax
import jax.numpy as jnp
from jax import lax
import numpy as np

D_MODEL = 4096
BATCH = 2
SEQ = 8192
DEPTH = 2

CTX_LEN = 256
GRID_W = 64
N_EVEN = (DEPTH + 1) // 2
N_ODD = DEPTH // 2
RMS_EPS = 1e-6
N_MOD = 6

HEAD_DIM = 64
A_WIDTH = D_MODEL // 2
B_WIDTH = D_MODEL - A_WIDTH
A_HEADS = A_WIDTH // HEAD_DIM
B_HEADS = B_WIDTH // HEAD_DIM
DECAY_RANK = max(32, int(round(1.8 * A_WIDTH ** 0.5 / 32)) * 32)
AAA_RANK = max(32, int(round(1.8 * A_WIDTH ** 0.5 / 32)) * 32)
GATE_RANK = max(32, int(round(0.6 * A_WIDTH ** 0.8 / 32)) * 32)
A_SPLITS = [A_WIDTH, 2 * A_WIDTH, 3 * A_WIDTH, 3 * A_WIDTH + 2 * DECAY_RANK, 3 * A_WIDTH + 2 * DECAY_RANK + 2 * AAA_RANK]
A_COLS = 3 * A_WIDTH + 2 * DECAY_RANK + 2 * AAA_RANK + GATE_RANK
B_COLS = 3 * B_WIDTH
EV_COLS = A_COLS + B_COLS
GN_EPS = 64e-5
WIN_R = 8
WIN_C = 16

SSM_INNER = 2 * D_MODEL
SSM_HEAD_DIM = 64
SSM_HEADS = SSM_INNER // SSM_HEAD_DIM
SSM_GROUPS = 8
SSM_STATE = 128
SSM_CONV = 4
SSM_CHUNK = 128
SSM_CONV_DIM = SSM_INNER + 2 * SSM_GROUPS * SSM_STATE
OD_COLS = SSM_INNER + SSM_CONV_DIM + 2 * SSM_HEADS

N_EXPERTS = 32
TOP_K = 4
EXPERT_FF = D_MODEL // 8
SWIGLU_ALPHA = 1.702
SWIGLU_LIMIT = 7.0
MOE_BLOCK = 128

kernel_name = 'hybrid_rwkv7_natten_mamba2_moe_dit'


def rmsnorm(x, g):
    xf = x.astype(jnp.float32)
    y = xf * lax.rsqrt(jnp.mean(xf * xf, axis=-1, keepdims=True) + RMS_EPS)
    return (y * g.astype(jnp.float32)).astype(x.dtype)


def modulate(h, shift, scale):
    return h * (1.0 + scale) + shift


def centred_shift(u, mu):
    prev = jnp.pad(u, ((0, 0), (1, 0), (0, 0)))[:, :-1]
    nxt = jnp.pad(u, ((0, 0), (0, 1), (0, 0)))[:, 1:]
    return u + mu * (0.5 * (prev + nxt) - u)


def depthwise_conv_centred(u, w, b):
    k = w.shape[0]
    out = lax.conv_general_dilated(u, w[:, None, :], window_strides=(1,), padding=[(k // 2, (k - 1) // 2)],
                                   dimension_numbers=('NWC', 'WIO', 'NWC'), feature_group_count=u.shape[-1])
    return out + b


def rwkv_terms(u, w0, w_up, a0, a_up, g_up, k_k, k_a):
    n_b, n_t, _ = u.shape
    r, k, v, wd, ad, gd = jnp.split(u, A_SPLITS, axis=-1)
    wd = jnp.tanh(wd.reshape(n_b, n_t, 2, DECAY_RANK))
    w_log = -jax.nn.softplus(-(w0 + jnp.einsum('btdr,drc->btdc', wd, w_up))) - 0.5
    decay = jnp.exp(-jnp.exp(w_log.astype(jnp.float32)))
    a = jax.nn.sigmoid(a0 + jnp.einsum('btdr,drc->btdc', ad.reshape(n_b, n_t, 2, AAA_RANK), a_up))
    g = jax.nn.sigmoid(gd) @ g_up
    kk = (k * k_k).reshape(n_b, n_t, A_HEADS, HEAD_DIM).astype(jnp.float32)
    kk = kk * lax.rsqrt(jnp.maximum(jnp.sum(kk * kk, axis=-1, keepdims=True), 1e-24))
    k_dir = k[:, :, None] * (1.0 + (a - 1.0) * k_a)
    hd = lambda t: t.reshape(t.shape[:-1] + (A_HEADS, HEAD_DIM))
    return (hd(r), hd(k_dir), hd(v), hd(decay), hd(a), kk.astype(u.dtype), g)


def rwkv_scan(terms, d, s0):
    r, k_dir, v, decay, a, kk, _ = terms
    seqs = (r, decay[:, :, d], k_dir[:, :, d], v, -kk, kk * a[:, :, d])
    xs = tuple(jnp.swapaxes(t.astype(jnp.float32), 0, 1) for t in seqs)

    def step(s, inp):
        r_t, w_t, k_t, v_t, a_t, b_t = inp
        sa = jnp.einsum('bhvk,bhk->bhv', s, a_t)
        s = s * w_t[:, :, None, :] + sa[..., None] * b_t[:, :, None, :] + v_t[..., None] * k_t[:, :, None, :]
        return s, jnp.einsum('bhvk,bhk->bhv', s, r_t)

    s_fin, y = lax.scan(step, s0, xs, reverse=(d == 1))
    return jnp.swapaxes(y, 0, 1).astype(r.dtype), s_fin


def rwkv_output(y_sum, terms, r_k, ln_w, ln_b):
    r, k_dir, v, _, _, _, g = terms
    n_b, n_t, n_h, n_d = y_sum.shape
    yf = y_sum.astype(jnp.float32)
    mu = jnp.mean(yf, axis=-1, keepdims=True)
    var = jnp.mean(jnp.square(yf - mu), axis=-1, keepdims=True)
    yn = ((yf - mu) * lax.rsqrt(var + GN_EPS)).reshape(n_b, n_t, n_h * n_d) * ln_w + ln_b
    bonus = jnp.sum(r[:, :, None] * k_dir * r_k, axis=(2, -1))[..., None] * v
    return (yn.astype(g.dtype) + bonus.reshape(n_b, n_t, n_h * n_d)) * g


def neighbourhood_attention(q, k, v, kc, vc, rpb):
    n_b, n_s, n_h, dh = q.shape
    rows = n_s // GRID_W
    kr = min(WIN_R, rows)
    grid = lambda t: t.reshape(n_b, rows, GRID_W, n_h, dh)
    qg = jnp.moveaxis(grid(q * dh ** -0.5), 1, 0)
    kg, vg = grid(k), grid(v)
    col = np.arange(GRID_W)
    col_start = np.clip(col - WIN_C // 2, 0, GRID_W - WIN_C)
    col_idx = col_start[:, None] + np.arange(WIN_C)[None]
    col_off = col_idx - col[:, None] + (WIN_C - 1)
    n_loc = kr * WIN_C

    def row_block(args):
        r, q_r = args
        r0 = jnp.clip(r - kr // 2, 0, rows - kr)
        k_win = lax.dynamic_slice_in_dim(kg, r0, kr, axis=1)[:, :, col_idx]
        v_win = lax.dynamic_slice_in_dim(vg, r0, kr, axis=1)[:, :, col_idx]
        row_off = r0 + jnp.arange(kr) - r + (WIN_R - 1)
        bias = jnp.transpose(rpb[:, row_off[:, None, None], col_off[None]], (0, 2, 1, 3))
        s_loc = jnp.einsum('bqhd,biqjhd->bhqij', q_r, k_win).astype(jnp.float32) + bias[None].astype(jnp.float32)
        s_ctx = jnp.einsum('bqhd,bchd->bhqc', q_r, kc).astype(jnp.float32)
        s = jnp.concatenate([s_loc.reshape(n_b, n_h, GRID_W, n_loc), s_ctx], axis=-1)
        p = jax.nn.softmax(s, axis=-1).astype(q.dtype)
        p_loc = p[..., :n_loc].reshape(n_b, n_h, GRID_W, kr, WIN_C)
        return (jnp.einsum('bhqij,biqjhd->bqhd', p_loc, v_win)
                + jnp.einsum('bhqc,bchd->bqhd', p[..., n_loc:], vc))

    out = lax.map(row_block, (jnp.arange(rows), qg))
    return jnp.moveaxis(out, 0, 1).reshape(n_b, n_s, n_h * dh)


def context_attention(qc, kc, vc):
    n_b, n_c, n_h, dh = qc.shape
    s = jnp.einsum('bqhd,bkhd->bhqk', qc * dh ** -0.5, kc).astype(jnp.float32)
    p = jax.nn.softmax(s, axis=-1).astype(qc.dtype)
    return jnp.einsum('bhqk,bkhd->bqhd', p, vc).reshape(n_b, n_c, n_h * dh)


def even_mixer(hx, hc, w_in, shift_mu, w0, w_up, a0, a_up, g_up, k_k, k_a, r_k, ln_w, ln_b, rpb, w_out, need_ctx):
    n_b = hx.shape[0]
    px = hx @ w_in
    pc = hc @ w_in
    lora = (w0, w_up, a0, a_up, g_up, k_k, k_a)
    tx = rwkv_terms(centred_shift(px[..., :A_COLS], shift_mu), *lora)
    tc = rwkv_terms(centred_shift(pc[..., :A_COLS], shift_mu), *lora)
    s0 = jnp.zeros((n_b, A_HEADS, HEAD_DIM, HEAD_DIM), jnp.float32)
    yc_f, sc_f = rwkv_scan(tc, 0, s0)
    yc_b, sc_b = rwkv_scan(tc, 1, s0)
    yx_f, _ = rwkv_scan(tx, 0, sc_f)
    yx_b, _ = rwkv_scan(tx, 1, sc_b)
    heads = lambda t: t.reshape(t.shape[:2] + (B_HEADS, HEAD_DIM))
    qx, kx, vx = [heads(t) for t in jnp.split(px[..., A_COLS:], 3, axis=-1)]
    qc, kc, vc = [heads(t) for t in jnp.split(pc[..., A_COLS:], 3, axis=-1)]
    rx = rwkv_output(yx_f + yx_b, tx, r_k, ln_w, ln_b)
    nx = neighbourhood_attention(qx, kx, vx, kc, vc, rpb)
    ox = jnp.concatenate([rx, nx], axis=-1) @ w_out
    oc = None
    if need_ctx:
        rc = rwkv_output(yc_f + yc_b, tc, r_k, ln_w, ln_b)
        oc = jnp.concatenate([rc, context_attention(qc, kc, vc)], axis=-1) @ w_out
    return ox, oc


def ssd_scan(xs, dt, a, bm, cm, s0, reverse):
    out_dtype = xs.dtype
    if reverse:
        xs, dt, bm, cm = (jnp.flip(t, axis=1) for t in (xs, dt, bm, cm))
    n_b, n_t = xs.shape[:2]
    n_c = n_t // SSM_CHUNK
    e = SSM_HEADS // SSM_GROUPS

    def chunks(t):
        t = t.astype(jnp.float32).reshape((n_b, n_c, SSM_CHUNK) + t.shape[2:])
        return jnp.moveaxis(t, 1, 0)

    xs_c = chunks(xs.reshape(n_b, n_t, SSM_GROUPS, e, SSM_HEAD_DIM))
    dt_c = chunks(dt.reshape(n_b, n_t, SSM_GROUPS, e))
    b_c, c_c = chunks(bm), chunks(cm)
    a_ge = a.astype(jnp.float32).reshape(SSM_GROUPS, e)
    lower = jnp.tril(jnp.ones((SSM_CHUNK, SSM_CHUNK), dtype=bool))

    def step(state, inp):
        x_t, dt_t, b_t, c_t = inp
        cum = jnp.cumsum(dt_t * a_ge, axis=1)
        seg = cum[:, :, None] - cum[:, None, :]
        lmat = jnp.exp(jnp.where(lower[None, :, :, None, None], seg, -jnp.inf))
        xdt = x_t * dt_t[..., None]
        cb = jnp.einsum('blgn,bsgn->blsg', c_t, b_t)
        y = jnp.einsum('blsg,blsge,bsgep->blgep', cb, lmat, xdt)
        y = y + jnp.einsum('blgn,bgepn->blgep', c_t, state) * jnp.exp(cum)[..., None]
        to_end = jnp.exp(cum[:, -1:] - cum)
        state = (state * jnp.exp(cum[:, -1])[..., None, None]
                 + jnp.einsum('bsgn,bsge,bsgep->bgepn', b_t, to_end, xdt))
        return state, y

    s_fin, ys = lax.scan(step, s0, (xs_c, dt_c, b_c, c_c))
    y = jnp.moveaxis(ys, 0, 1).reshape(n_b, n_t, SSM_HEADS, SSM_HEAD_DIM)
    if reverse:
        y = jnp.flip(y, axis=1)
    return y.astype(out_dtype), s_fin


def ssm_project(h, w_in, conv_w, conv_b, dt_bias):
    n_b, n_t, _ = h.shape
    z, xbc, dt = jnp.split(h @ w_in, [SSM_INNER, SSM_INNER + SSM_CONV_DIM], axis=-1)
    xbc = jax.nn.silu(depthwise_conv_centred(xbc, conv_w, conv_b))
    xs, bm, cm = jnp.split(xbc, [SSM_INNER, SSM_INNER + SSM_GROUPS * SSM_STATE], axis=-1)
    xs = xs.reshape(n_b, n_t, SSM_HEADS, SSM_HEAD_DIM)
    bm = bm.reshape(n_b, n_t, SSM_GROUPS, SSM_STATE)
    cm = cm.reshape(n_b, n_t, SSM_GROUPS, SSM_STATE)
    dt = jax.nn.softplus((dt.reshape(n_b, n_t, 2, SSM_HEADS) + dt_bias).astype(jnp.float32))
    return z, xs, bm, cm, dt


def ssm_output(y, z, xs, d_skip, norm_w, w_out):
    n_b, n_t = z.shape[:2]
    y = (y + d_skip[:, None] * xs).reshape(n_b, n_t, SSM_INNER) * jax.nn.silu(z)
    yf = y.astype(jnp.float32).reshape(n_b, n_t, SSM_GROUPS, SSM_INNER // SSM_GROUPS)
    yf = yf * lax.rsqrt(jnp.mean(yf * yf, axis=-1, keepdims=True) + RMS_EPS)
    y = (yf.reshape(n_b, n_t, SSM_INNER) * norm_w).astype(z.dtype)
    return y @ w_out


def odd_mixer(hx, hc, w_in, conv_w, conv_b, dt_bias, a_log, d_skip, norm_w, w_out, need_ctx):
    zx, xsx, bx, cx, dtx = ssm_project(hx, w_in, conv_w, conv_b, dt_bias)
    zc, xsc, bc, cc, dtc = ssm_project(hc, w_in, conv_w, conv_b, dt_bias)
    a = -jnp.exp(a_log.astype(jnp.float32))
    s0 = jnp.zeros((hx.shape[0], SSM_GROUPS, SSM_HEADS // SSM_GROUPS, SSM_HEAD_DIM, SSM_STATE), jnp.float32)
    yc_f, sc_f = ssd_scan(xsc, dtc[:, :, 0], a[0], bc, cc, s0, False)
    yc_b, sc_b = ssd_scan(xsc, dtc[:, :, 1], a[1], bc, cc, s0, True)
    yx_f, _ = ssd_scan(xsx, dtx[:, :, 0], a[0], bx, cx, sc_f, False)
    yx_b, _ = ssd_scan(xsx, dtx[:, :, 1], a[1], bx, cx, sc_b, True)
    ox = ssm_output(yx_f + yx_b, zx, xsx, d_skip, norm_w, w_out)
    oc = ssm_output(yc_f + yc_b, zc, xsc, d_skip, norm_w, w_out) if need_ctx else None
    return ox, oc


def clamped_swiglu(u):
    glu = jnp.minimum(u[..., ::2], SWIGLU_LIMIT)
    lin = jnp.clip(u[..., 1::2], -SWIGLU_LIMIT, SWIGLU_LIMIT)
    return glu * jax.nn.sigmoid(SWIGLU_ALPHA * glu) * (lin + 1.0)


def moe_ffn(h, router_w, router_b, w1, b1, w2, b2):
    n_tok = h.shape[0]
    logits = (h @ router_w).astype(jnp.float32) + router_b.astype(jnp.float32)
    top_val, top_idx = lax.top_k(logits, TOP_K)
    gate = jax.nn.softmax(top_val, axis=-1).astype(h.dtype)
    n_assign = n_tok * TOP_K
    flat_e = top_idx.reshape(-1)
    order = jnp.argsort(flat_e)
    sorted_e = flat_e[order]
    counts = jnp.bincount(flat_e, length=N_EXPERTS)
    padded = (counts + MOE_BLOCK - 1) // MOE_BLOCK * MOE_BLOCK
    pad_end = jnp.cumsum(padded)
    pad_start = pad_end - padded
    sort_start = jnp.cumsum(counts) - counts
    dest = pad_start[sorted_e] + jnp.arange(n_assign) - sort_start[sorted_e]
    n_blocks = -(-(n_assign + N_EXPERTS * MOE_BLOCK) // MOE_BLOCK)
    n_rows = n_blocks * MOE_BLOCK
    row_tok = jnp.zeros((n_rows,), jnp.int32).at[dest].set((order // TOP_K).astype(jnp.int32))
    row_gate = jnp.zeros((n_rows,), h.dtype).at[dest].set(gate.reshape(-1)[order])
    block_exp = jnp.minimum(jnp.searchsorted(pad_end, jnp.arange(n_blocks) * MOE_BLOCK, side='right'), N_EXPERTS - 1)

    def block(y, inp):
        tok, g, e = inp
        u = h[tok] @ w1[e] + b1[e]
        out = clamped_swiglu(u) @ w2[e] + b2[e]
        return y.at[tok].add(out * g[:, None]), None

    y, _ = lax.scan(block, jnp.zeros_like(h),
                    (row_tok.reshape(n_blocks, MOE_BLOCK), row_gate.reshape(n_blocks, MOE_BLOCK), block_exp))
    return y


def setup_inputs(seed: int = 0) -> dict:
    key = jax.random.key(seed)
    ks = iter(jax.random.split(key, 48))
    f32 = jnp.float32
    nrm = lambda shape, scale: scale * jax.random.normal(next(ks), shape, f32)
    uni = lambda shape, lo, hi: jax.random.uniform(next(ks), shape, f32, lo, hi)
    dt0 = jnp.exp(uni((N_ODD, 2, SSM_HEADS), math.log(1e-3), math.log(1e-1)))
    return {
        'x': nrm((BATCH, SEQ, D_MODEL), 1.0),
        'c': nrm((BATCH, D_MODEL), 1.0),
        'ctx': nrm((BATCH, CTX_LEN, D_MODEL), 1.0),
        'c_ctx': nrm((D_MODEL,), 1.0),
        'ada_w': nrm((DEPTH, D_MODEL, N_MOD * D_MODEL), 0.3 * D_MODEL ** -0.5),
        'ada_b': nrm((DEPTH, N_MOD * D_MODEL), 0.02),
        'norm_g': 1.0 + nrm((DEPTH, 4, D_MODEL), 0.02),
        'ev_w_in': nrm((N_EVEN, D_MODEL, EV_COLS), D_MODEL ** -0.5),
        'ev_shift_mu': uni((N_EVEN, A_COLS), 0.0, 1.0),
        'rk_w0': uni((N_EVEN, 2, A_WIDTH), -5.5, -0.5),
        'rk_w_up': nrm((N_EVEN, 2, DECAY_RANK, A_WIDTH), 0.5 * DECAY_RANK ** -0.5),
        'rk_a0': nrm((N_EVEN, 2, A_WIDTH), 0.1),
        'rk_a_up': nrm((N_EVEN, 2, AAA_RANK, A_WIDTH), 0.5 * AAA_RANK ** -0.5),
        'rk_g_up': nrm((N_EVEN, GATE_RANK, A_WIDTH), GATE_RANK ** -0.5),
        'rk_k_k': 0.85 + nrm((N_EVEN, A_WIDTH), 0.05),
        'rk_k_a': 1.0 + nrm((N_EVEN, A_WIDTH), 0.05),
        'rk_r_k': nrm((N_EVEN, A_HEADS, HEAD_DIM), 0.1),
        'rk_ln_w': 1.0 + nrm((N_EVEN, A_WIDTH), 0.02),
        'rk_ln_b': nrm((N_EVEN, A_WIDTH), 0.02),
        'na_rpb': nrm((N_EVEN, B_HEADS, 2 * WIN_R - 1, 2 * WIN_C - 1), 0.1),
        'ev_w_out': nrm((N_EVEN, A_WIDTH + B_WIDTH, D_MODEL), (A_WIDTH + B_WIDTH) ** -0.5),
        'od_w_in': nrm((N_ODD, D_MODEL, OD_COLS), D_MODEL ** -0.5),
        'od_conv_w': nrm((N_ODD, SSM_CONV, SSM_CONV_DIM), SSM_CONV ** -0.5),
        'od_conv_b': nrm((N_ODD, SSM_CONV_DIM), 0.02),
        'od_dt_bias': dt0 + jnp.log(-jnp.expm1(-dt0)),
        'od_a_log': jnp.log(uni((N_ODD, 2, SSM_HEADS), 1.0, 16.0)),
        'od_d': 1.0 + nrm((N_ODD, SSM_HEADS), 0.1),
        'od_norm_w': 1.0 + nrm((N_ODD, SSM_INNER), 0.02),
        'od_w_out': nrm((N_ODD, SSM_INNER, D_MODEL), SSM_INNER ** -0.5),
        'router_w': nrm((DEPTH, D_MODEL, N_EXPERTS), D_MODEL ** -0.5),
        'router_b': nrm((DEPTH, N_EXPERTS), 0.01),
        'moe_w1': nrm((DEPTH, N_EXPERTS, D_MODEL, 2 * EXPERT_FF), D_MODEL ** -0.5),
        'moe_b1': nrm((DEPTH, N_EXPERTS, 2 * EXPERT_FF), 0.02),
        'moe_w2': nrm((DEPTH, N_EXPERTS, EXPERT_FF, D_MODEL), EXPERT_FF ** -0.5),
        'moe_b2': nrm((DEPTH, N_EXPERTS, D_MODEL), 0.02),
    }


def reference(x, c, ctx, c_ctx, ada_w, ada_b, norm_g,
              ev_w_in, ev_shift_mu, rk_w0, rk_w_up, rk_a0, rk_a_up, rk_g_up, rk_k_k, rk_k_a, rk_r_k,
              rk_ln_w, rk_ln_b, na_rpb, ev_w_out,
              od_w_in, od_conv_w, od_conv_b, od_dt_bias, od_a_log, od_d, od_norm_w, od_w_out,
              router_w, router_b, moe_w1, moe_b1, moe_w2, moe_b2):
    n_b, n_seq, d_model = x.shape
    n_ctx = ctx.shape[1]
    cond_x = jax.nn.silu(c)
    cond_c = jax.nn.silu(c_ctx)
    for layer in range(DEPTH):
        need_ctx = layer < DEPTH - 1
        i = layer // 2
        mod_x = (cond_x @ ada_w[layer] + ada_b[layer]).reshape(n_b, N_MOD, 1, d_model)
        mod_c = (cond_c @ ada_w[layer] + ada_b[layer]).reshape(N_MOD, d_model)
        g = norm_g[layer]
        hx = modulate(rmsnorm(x, g[0]), mod_x[:, 0], mod_x[:, 1])
        hc = modulate(rmsnorm(ctx, g[0]), mod_c[0], mod_c[1])
        if layer % 2 == 0:
            ox, oc = even_mixer(hx, hc, ev_w_in[i], ev_shift_mu[i], rk_w0[i], rk_w_up[i], rk_a0[i], rk_a_up[i],
                                rk_g_up[i], rk_k_k[i], rk_k_a[i], rk_r_k[i], rk_ln_w[i], rk_ln_b[i], na_rpb[i],
                                ev_w_out[i], need_ctx)
        else:
            ox, oc = odd_mixer(hx, hc, od_w_in[i], od_conv_w[i], od_conv_b[i], od_dt_bias[i], od_a_log[i],
                               od_d[i], od_norm_w[i], od_w_out[i], need_ctx)
        x = x + mod_x[:, 2] * rmsnorm(ox, g[1])
        hx = modulate(rmsnorm(x, g[2]), mod_x[:, 3], mod_x[:, 4])
        moe_args = (router_w[layer], router_b[layer], moe_w1[layer], moe_b1[layer], moe_w2[layer], moe_b2[layer])
        if need_ctx:
            ctx = ctx + mod_c[2] * rmsnorm(oc, g[1])
            hc = modulate(rmsnorm(ctx, g[2]), mod_c[3], mod_c[4])
            f = moe_ffn(jnp.concatenate([hc.reshape(-1, d_model), hx.reshape(-1, d_model)], axis=0), *moe_args)
            fc = f[: n_b * n_ctx].reshape(n_b, n_ctx, d_model)
            fx = f[n_b * n_ctx:].reshape(n_b, n_seq, d_model)
            ctx = ctx + mod_c[5] * rmsnorm(fc, g[3])
        else:
            fx = moe_ffn(hx.reshape(-1, d_model), *moe_args).reshape(n_b, n_seq, d_model)
        x = x + mod_x[:, 5] * rmsnorm(fx, g[3])
    return x
```

```python
import functools
import math

import numpy as np
import jax
import jax.numpy as jnp
from jax import lax
from jax.experimental import pallas as pl
from jax.experimental.pallas import tpu as pltpu

F32 = jnp.float32
BF16 = jnp.bfloat16

RMS_EPS = 1e-6
GN_EPS = 64e-5
HEAD_DIM = 64
GRID_W = 64
WIN_R = 8
WIN_C = 16
SSM_HEAD_DIM = 64
SSM_GROUPS = 8
SSM_STATE = 128
SSM_CHUNK = 128
TOP_K = 4
SWIGLU_ALPHA = 1.702
SWIGLU_LIMIT = 7.0
MOE_ROWS = 256
RWKV_CHUNK = 64
LANES = 128
SUBLANES = 8
VMEM_LIMIT = 56 * 1024 * 1024
NEG_BIG = -1e30


def _pick(n, cands):
    for c in cands:
        if n % c == 0:
            return c
    return n


def _params(sem, vmem=VMEM_LIMIT):
    return pltpu.CompilerParams(dimension_semantics=sem, vmem_limit_bytes=vmem)


def _split3(x):
    hi = x.astype(BF16)
    r1 = x - hi.astype(F32)
    mid = r1.astype(BF16)
    lo = (r1 - mid.astype(F32)).astype(BF16)
    return hi, mid, lo


def _dot(a, b):
    return jnp.dot(a, b, preferred_element_type=F32)


def _dot_nt(a, b):
    return lax.dot_general(a, b, (((1,), (1,)), ((), ())), preferred_element_type=F32)


def _dot_tn(a, b):
    return lax.dot_general(a, b, (((0,), (0,)), ((), ())), preferred_element_type=F32)


def _dot_exact_rhs(x, m_bf16):
    hi, mid, lo = _split3(x)
    return _dot(hi, m_bf16) + _dot(mid, m_bf16) + _dot(lo, m_bf16)


def _mm_kernel(a_ref, w_ref, b_ref, o_ref, acc_ref, *, nk, pre):
    k = pl.program_id(2)

    @pl.when(k == 0)
    def _():
        acc_ref[...] = jnp.zeros_like(acc_ref)

    a = a_ref[...]
    if pre == "silu":
        a = a * jax.nn.sigmoid(a)
    acc_ref[...] += _dot(a.astype(BF16), w_ref[...].astype(BF16))

    @pl.when(k == nk - 1)
    def _():
        o_ref[...] = (acc_ref[...] + b_ref[...]).astype(o_ref.dtype)


def _matmul(a, w, bias=None, out_dtype=F32, pre=None):
    m, kdim = a.shape
    n = w.shape[1]
    tm = _pick(m, (1024, 512, 256, 128, 64, 32, 16, 8))
    tn = _pick(n, (1024, 768, 640, 512, 384, 256, 128))
    tk = _pick(kdim, (1024, 512, 256, 128))
    nk = kdim // tk
    if bias is None:
        bias = jnp.zeros((n,), F32)
    return pl.pallas_call(
        functools.partial(_mm_kernel, nk=nk, pre=pre),
        grid=(m // tm, n // tn, nk),
        in_specs=[pl.BlockSpec((tm, tk), lambda i, j, k: (i, k)),
                  pl.BlockSpec((tk, tn), lambda i, j, k: (k, j)),
                  pl.BlockSpec((1, tn), lambda i, j, k: (0, j))],
        out_specs=pl.BlockSpec((tm, tn), lambda i, j, k: (i, j)),
        out_shape=jax.ShapeDtypeStruct((m, n), out_dtype),
        scratch_shapes=[pltpu.VMEM((tm, tn), F32)],
        compiler_params=_params(("parallel", "parallel", "arbitrary")),
        name="matmul",
    )(a, w, bias.reshape(1, n).astype(F32))


def _adaln_kernel(c_ref, w_ref, b_ref, o_ref, acc_ref, *, nk):
    k = pl.program_id(2)

    @pl.when(k == 0)
    def _():
        acc_ref[...] = jnp.zeros_like(acc_ref)

    c = c_ref[...]
    c = c * jax.nn.sigmoid(c)
    acc_ref[...] += _dot(c.astype(BF16), w_ref[...].astype(BF16))

    @pl.when(k == nk - 1)
    def _():
        o_ref[...] = acc_ref[...] + b_ref[...]


def _adaln(cond, ada_w, ada_b):
    depth, d, n = ada_w.shape
    r = cond.shape[0]
    tn = _pick(n, (2048, 1024, 512, 256, 128))
    tk = _pick(d, (1024, 512, 256, 128))
    nk = d // tk
    return pl.pallas_call(
        functools.partial(_adaln_kernel, nk=nk),
        grid=(depth, n // tn, nk),
        in_specs=[pl.BlockSpec((r, tk), lambda l, j, k: (0, k)),
                  pl.BlockSpec((None, tk, tn), lambda l, j, k: (l, k, j)),
                  pl.BlockSpec((None, 1, tn), lambda l, j, k: (l, 0, j))],
        out_specs=pl.BlockSpec((None, r, tn), lambda l, j, k: (l, 0, j)),
        out_shape=jax.ShapeDtypeStruct((depth, r, n), F32),
        scratch_shapes=[pltpu.VMEM((r, tn), F32)],
        compiler_params=_params(("parallel", "parallel", "arbitrary")),
        name="adaln",
    )(cond, ada_w, ada_b.reshape(depth, 1, n))


def _rms(x, g):
    return x * lax.rsqrt(jnp.mean(x * x, axis=-1, keepdims=True) + RMS_EPS) * g


def _norm_mod_kernel(x_ref, g_ref, mod_ref, o_ref, *, i_shift, i_scale):
    y = _rms(x_ref[...], g_ref[...])
    o = y * (1.0 + mod_ref[i_scale:i_scale + 1, :]) + mod_ref[i_shift:i_shift + 1, :]
    o_ref[...] = o.astype(o_ref.dtype)


def _norm_mod(x, g, mod, i_shift, i_scale, out_dtype=BF16):
    t, d = x.shape
    seg = t // mod.shape[0]
    tr = _pick(seg, (256, 128, 64, 32, 16, 8))
    per = seg // tr
    return pl.pallas_call(
        functools.partial(_norm_mod_kernel, i_shift=i_shift, i_scale=i_scale),
        grid=(t // tr,),
        in_specs=[pl.BlockSpec((tr, d), lambda i: (i, 0)),
                  pl.BlockSpec((1, d), lambda i: (0, 0)),
                  pl.BlockSpec((None, 6, d), lambda i: (i // per, 0, 0))],
        out_specs=pl.BlockSpec((tr, d), lambda i: (i, 0)),
        out_shape=jax.ShapeDtypeStruct((t, d), out_dtype),
        compiler_params=_params(("parallel",)),
        name="norm_mod",
    )(x, g.reshape(1, d), mod)


def _residual_kernel(*refs, n_sum, i_gate, i_shift, i_scale, with_h, with_router, n_exp):
    x_ref, o_ref, ga_ref, mod_ref = refs[:4]
    pos = 4
    if with_h:
        gb_ref, modh_ref = refs[pos:pos + 2]
        pos += 2
    if with_router:
        rw_ref, rb_ref = refs[pos:pos + 2]
        pos += 2
    xo_ref = refs[pos]
    pos += 1
    d = x_ref.shape[-1]
    o = o_ref[:, 0:d]
    for j in range(1, n_sum):
        o = o + o_ref[:, j * d:(j + 1) * d]
    xn = x_ref[...] + mod_ref[i_gate:i_gate + 1, :] * _rms(o, ga_ref[...])
    xo_ref[...] = xn
    if not with_h:
        return
    h_ref = refs[pos]
    pos += 1
    h = _rms(xn, gb_ref[...]) * (1.0 + modh_ref[i_scale:i_scale + 1, :]) + modh_ref[i_shift:i_shift + 1, :]
    h_ref[...] = h.astype(h_ref.dtype)
    if not with_router:
        return
    idx_ref, gate_ref = refs[pos:pos + 2]
    h1, h2, h3 = _split3(h)
    w1 = rw_ref[0]
    w2 = rw_ref[1]
    logits = (_dot(h1, w1) + _dot(h2, w1) + _dot(h1, w2) + _dot(h3, w1) + _dot(h2, w2)) + rb_ref[...]
    lane = lax.broadcasted_iota(jnp.int32, logits.shape, 1)
    logits = jnp.where(lane < n_exp, logits, NEG_BIG)
    vals, idxs = [], []
    for _ in range(TOP_K):
        m = jnp.max(logits, axis=-1, keepdims=True)
        ix = jnp.min(jnp.where(logits == m, lane, 1 << 30), axis=-1, keepdims=True)
        vals.append(m)
        idxs.append(ix)
        logits = jnp.where(lane == ix, NEG_BIG * 2, logits)
    es = [jnp.exp(v - vals[0]) for v in vals]
    den = es[0]
    for e in es[1:]:
        den = den + e
    idx_out = jnp.zeros(lane.shape, jnp.int32)
    gate_out = jnp.zeros(lane.shape, F32)
    for j in range(TOP_K):
        idx_out = jnp.where(lane == j, idxs[j], idx_out)
        gate_out = jnp.where(lane == j, es[j] / den, gate_out)
    idx_ref[...] = idx_out
    gate_ref[...] = gate_out


def _residual(x, o, g_a, mod, i_gate, n_sum=1, g_b=None, i_shift=0, i_scale=0, h_dtype=F32, router=None,
              mod_h=None):
    t, d = x.shape
    seg = t // mod.shape[0]
    tr = _pick(seg, (128, 64, 32, 16, 8) if n_sum == 1 else (64, 32, 16, 8))
    per = seg // tr
    with_h = g_b is not None
    with_router = router is not None
    row = lambda i: (i, 0)
    fixed = lambda i: (0, 0)
    ins = [x, o, g_a.reshape(1, d), mod]
    in_specs = [pl.BlockSpec((tr, d), row), pl.BlockSpec((tr, n_sum * d), row), pl.BlockSpec((1, d), fixed),
                pl.BlockSpec((None, 6, d), lambda i: (i // per, 0, 0))]
    outs = [jax.ShapeDtypeStruct((t, d), F32)]
    out_specs = [pl.BlockSpec((tr, d), row)]
    n_exp = 0
    if with_h:
        ins += [g_b.reshape(1, d), mod if mod_h is None else mod_h]
        in_specs += [pl.BlockSpec((1, d), fixed), pl.BlockSpec((None, 6, d), lambda i: (i // per, 0, 0))]
        outs.append(jax.ShapeDtypeStruct((t, d), h_dtype))
        out_specs.append(pl.BlockSpec((tr, d), row))
    if with_router:
        rw, rb = router
        n_exp = rw.shape[1]
        rw = jnp.pad(rw, ((0, 0), (0, LANES - n_exp)))
        rw_hi = rw.astype(BF16)
        rw_lo = (rw - rw_hi.astype(F32)).astype(BF16)
        ins += [jnp.stack([rw_hi, rw_lo]), jnp.pad(rb, (0, LANES - n_exp)).reshape(1, LANES)]
        in_specs += [pl.BlockSpec((2, d, LANES), lambda i: (0, 0, 0)), pl.BlockSpec((1, LANES), fixed)]
        outs += [jax.ShapeDtypeStruct((t, LANES), jnp.int32), jax.ShapeDtypeStruct((t, LANES), F32)]
        out_specs += [pl.BlockSpec((tr, LANES), row), pl.BlockSpec((tr, LANES), row)]
    res = pl.pallas_call(
        functools.partial(_residual_kernel, n_sum=n_sum, i_gate=i_gate, i_shift=i_shift, i_scale=i_scale,
                          with_h=with_h, with_router=with_router, n_exp=n_exp),
        grid=(t // tr,),
        in_specs=in_specs,
        out_specs=out_specs,
        out_shape=outs,
        compiler_params=_params(("parallel",)),
        name="residual",
    )(*ins)
    return res


def _moe_kernel(be_ref, tok_ref, slot_ref, gate_ref, h_hbm, w1g_ref, w1l_ref, b1g_ref, b1l_ref, w2_ref, b2_ref,
                out_hbm, xbuf, obuf, sem_in, sem_out, *, bm):
    del be_ref

    def gather_copy(r):
        return pltpu.make_async_copy(h_hbm.at[pl.ds(tok_ref[0, r], 1)], xbuf.at[pl.ds(r, 1)], sem_in)

    def scatter_copy(r):
        return pltpu.make_async_copy(obuf.at[pl.ds(r, 1)], out_hbm.at[pl.ds(slot_ref[0, r], 1)], sem_out)

    def start_gather(r, c):
        gather_copy(r).start()
        return c

    def wait_gather(r, c):
        gather_copy(r).wait()
        return c

    lax.fori_loop(0, bm, start_gather, 0)
    lax.fori_loop(0, bm, wait_gather, 0)
    x = xbuf[...].astype(BF16)
    glu = jnp.minimum(_dot(x, w1g_ref[...]) + b1g_ref[...], SWIGLU_LIMIT)
    lin = jnp.clip(_dot(x, w1l_ref[...]) + b1l_ref[...], -SWIGLU_LIMIT, SWIGLU_LIMIT)
    act = glu * jax.nn.sigmoid(SWIGLU_ALPHA * glu) * (lin + 1.0)
    out = _dot(act.astype(BF16), w2_ref[...]) + b2_ref[...]
    obuf[...] = out * gate_ref[...]

    def start_scatter(r, c):
        scatter_copy(r).start()
        return c

    def wait_scatter(r, c):
        scatter_copy(r).wait()
        return c

    lax.fori_loop(0, bm, start_scatter, 0)
    lax.fori_loop(0, bm, wait_scatter, 0)


def _moe(h, top_idx, gate, w1g, w1l, b1g, b1l, w2, b2):
    t, d = h.shape
    n_exp, f = w2.shape[0], w2.shape[1]
    bm = MOE_ROWS
    n_assign = t * TOP_K
    flat_e = top_idx.reshape(-1)
    order = jnp.argsort(flat_e, stable=True).astype(jnp.int32)
    sorted_e = flat_e[order]
    counts = jnp.bincount(flat_e, length=n_exp)
    padded = (counts + bm - 1) // bm * bm
    pad_end = jnp.cumsum(padded)
    pad_start = pad_end - padded
    sort_start = jnp.cumsum(counts) - counts
    dest = (pad_start[sorted_e] + jnp.arange(n_assign) - sort_start[sorted_e]).astype(jnp.int32)
    n_blocks = -(-(n_assign + n_exp * (bm - 1)) // bm)
    n_rows = n_blocks * bm
    is_real = jnp.zeros((n_rows,), jnp.bool_).at[dest].set(True)
    spare = n_assign + jnp.cumsum(jnp.logical_not(is_real).astype(jnp.int32)) - 1
    row_tok = jnp.zeros((n_rows,), jnp.int32).at[dest].set(order // TOP_K)
    row_slot = spare.astype(jnp.int32).at[dest].set(order)
    row_gate = jnp.zeros((n_rows,), F32).at[dest].set(gate.reshape(-1)[order])
    block_exp = jnp.minimum(jnp.searchsorted(pad_end, jnp.arange(n_blocks) * bm, side="right"),
                            n_exp - 1).astype(jnp.int32)
    grid_spec = pltpu.PrefetchScalarGridSpec(
        num_scalar_prefetch=1,
        grid=(n_blocks,),
        in_specs=[pl.BlockSpec((None, 1, bm), lambda i, be: (i, 0, 0), memory_space=pltpu.SMEM),
                  pl.BlockSpec((None, 1, bm), lambda i, be: (i, 0, 0), memory_space=pltpu.SMEM),
                  pl.BlockSpec((bm, 1), lambda i, be: (i, 0)),
                  pl.BlockSpec(memory_space=pl.ANY),
                  pl.BlockSpec((None, d, f), lambda i, be: (be[i], 0, 0)),
                  pl.BlockSpec((None, d, f), lambda i, be: (be[i], 0, 0)),
                  pl.BlockSpec((None, 1, f), lambda i, be: (be[i], 0, 0)),
                  pl.BlockSpec((None, 1, f), lambda i, be: (be[i], 0, 0)),
                  pl.BlockSpec((None, f, d), lambda i, be: (be[i], 0, 0)),
                  pl.BlockSpec((None, 1, d), lambda i, be: (be[i], 0, 0))],
        out_specs=pl.BlockSpec(memory_space=pl.ANY),
        scratch_shapes=[pltpu.VMEM((bm, d), F32), pltpu.VMEM((bm, d), F32),
                        pltpu.SemaphoreType.DMA(()), pltpu.SemaphoreType.DMA(())],
    )
    out = pl.pallas_call(
        functools.partial(_moe_kernel, bm=bm),
        grid_spec=grid_spec,
        out_shape=jax.ShapeDtypeStruct((n_rows, d), F32),
        compiler_params=_params(("arbitrary",)),
        name="moe_experts",
    )(block_exp, row_tok.reshape(n_blocks, 1, bm), row_slot.reshape(n_blocks, 1, bm), row_gate.reshape(n_rows, 1),
      h, w1g, w1l, b1g, b1l, w2, b2)
    return out.reshape(n_rows // TOP_K, TOP_K * d)


def _moe_weights(w1, b1, w2, b2):
    n_exp = w1.shape[0]
    return (w1[:, :, 0::2].astype(BF16), w1[:, :, 1::2].astype(BF16),
            b1[:, 0::2].reshape(n_exp, 1, -1), b1[:, 1::2].reshape(n_exp, 1, -1),
            w2.astype(BF16), b2.reshape(n_exp, 1, -1))


def _row_shift(u, prev8, next8, k):
    n = u.shape[0]
    rolled = pltpu.roll(u, k % n, 0)
    sub = lax.broadcasted_iota(jnp.int32, (SUBLANES, u.shape[1]), 0)
    if k > 0:
        halo = pltpu.roll(prev8, k, 0)
        top = jnp.where(sub < k, halo, rolled[0:SUBLANES])
        return jnp.concatenate([top, rolled[SUBLANES:]], axis=0)
    halo = pltpu.roll(next8, SUBLANES + k, 0)
    bot = jnp.where(sub >= SUBLANES + k, halo, rolled[n - SUBLANES:])
    return jnp.concatenate([rolled[:n - SUBLANES], bot], axis=0)


def _halo_tiles(prev_ref, next_ref, per):
    j = pl.program_id(0) % per
    prev8 = jnp.where(j == 0, 0.0, prev_ref[...])
    next8 = jnp.where(j == per - 1, 0.0, next_ref[...])
    return prev8, next8


def _shift_kernel(u_ref, prev_ref, next_ref, mu_ref, o_ref, *, per):
    u = u_ref[...]
    prev8, next8 = _halo_tiles(prev_ref, next_ref, per)
    nb = 0.5 * (_row_shift(u, prev8, next8, 1) + _row_shift(u, prev8, next8, -1))
    o_ref[...] = u + mu_ref[...] * (nb - u)


def _conv_kernel(u_ref, prev_ref, next_ref, w_ref, b_ref, o_ref, *, per):
    u = u_ref[...]
    prev8, next8 = _halo_tiles(prev_ref, next_ref, per)
    acc = (w_ref[0:1, :] * _row_shift(u, prev8, next8, 2) + w_ref[1:2, :] * _row_shift(u, prev8, next8, 1)
           + w_ref[2:3, :] * u + w_ref[3:4, :] * _row_shift(u, prev8, next8, -1)) + b_ref[...]
    o_ref[...] = acc * jax.nn.sigmoid(acc)


def _row_neighbour_call(kernel_fn, u, seg, extra, name):
    t, n = u.shape
    tr = _pick(seg, (256, 128, 64, 32, 16, 8))
    tc = _pick(n, (2048, 1024, 768, 512, 384, 256, 128))
    per = seg // tr
    r8 = tr // SUBLANES
    last8 = t // SUBLANES - 1
    in_specs = [pl.BlockSpec((tr, tc), lambda i, j: (i, j)),
                pl.BlockSpec((SUBLANES, tc), lambda i, j: (jnp.maximum(i * r8 - 1, 0), j)),
                pl.BlockSpec((SUBLANES, tc), lambda i, j: (jnp.minimum((i + 1) * r8, last8), j))]
    for e in extra:
        in_specs.append(pl.BlockSpec((e.shape[0], tc), lambda i, j: (0, j)))
    return pl.pallas_call(
        functools.partial(kernel_fn, per=per),
        grid=(t // tr, n // tc),
        in_specs=in_specs,
        out_specs=pl.BlockSpec((tr, tc), lambda i, j: (i, j)),
        out_shape=jax.ShapeDtypeStruct((t, n), F32),
        compiler_params=_params(("parallel", "parallel")),
        name=name,
    )(u, u, u, *extra)


def _head_sum_matrix(n):
    idx = np.arange(n) // HEAD_DIM
    return jnp.asarray(idx[:, None] == idx[None, :], BF16)


def _rwkv_prep_kernel(r_ref, k_ref, v_ref, lo_ref, w0_ref, wup_ref, a0_ref, aup_ref, gup_ref, kk_ref_, ka_ref,
                      rk_ref, hs_ref, r_o, v_o, kk_o, lw_o, kd_o, b_o, g_o, bv_o, *, wd_w, ad_w):
    r = r_ref[...]
    k = k_ref[...]
    v = v_ref[...]
    lo = lo_ref[...]
    wd = jnp.tanh(lo[:, 0:wd_w]).astype(BF16)
    ad = lo[:, wd_w:wd_w + ad_w].astype(BF16)
    gs = jax.nn.sigmoid(lo[:, wd_w + ad_w:]).astype(BF16)
    hs = hs_ref[...]
    kkr = k * kk_ref_[...]
    ss = _dot_exact_rhs(kkr * kkr, hs)
    kk = kkr * lax.rsqrt(jnp.maximum(ss, 1e-24))
    ksum = jnp.zeros_like(k)
    for d in range(2):
        z = -(w0_ref[d:d + 1, :] + _dot(wd, wup_ref[d]))
        w_log = -(jnp.maximum(z, 0.0) + jnp.log(1.0 + jnp.exp(-jnp.abs(z)))) - 0.5
        lw_o[d] = -jnp.exp(w_log)
        asig = jax.nn.sigmoid(a0_ref[d:d + 1, :] + _dot(ad, aup_ref[d]))
        kd = k * (1.0 + (asig - 1.0) * ka_ref[...])
        kd_o[d] = kd
        b_o[d] = kk * asig
        ksum = ksum + kd
    bonus = _dot_exact_rhs(r * ksum * rk_ref[...], hs)
    r_o[...] = r
    v_o[...] = v
    kk_o[...] = kk
    g_o[...] = _dot(gs, gup_ref[...])
    bv_o[...] = bonus * v


def _rwkv_prep(us_rkv, us_lora, p, seg):
    t = us_rkv.shape[0]
    a = us_rkv.shape[1] // 3
    nl = us_lora.shape[1]
    tr = _pick(seg, (256, 128, 64, 32, 16, 8))
    tc = _pick(a, (512, 256, 128))
    nj = a // tc
    col = lambda off: (lambda i, j: (i, off * nj + j))
    par = lambda rows: pl.BlockSpec((rows, tc), lambda i, j: (0, j))
    par3 = lambda rows: pl.BlockSpec((2, rows, tc), lambda i, j: (0, 0, j))
    one = jax.ShapeDtypeStruct((t, a), F32)
    two = jax.ShapeDtypeStruct((2, t, a), F32)
    o1 = pl.BlockSpec((tr, tc), lambda i, j: (i, j))
    o2 = pl.BlockSpec((2, tr, tc), lambda i, j: (0, i, j))
    return pl.pallas_call(
        functools.partial(_rwkv_prep_kernel, wd_w=p["wd_w"], ad_w=p["ad_w"]),
        grid=(t // tr, nj),
        in_specs=[pl.BlockSpec((tr, tc), col(0)), pl.BlockSpec((tr, tc), col(1)), pl.BlockSpec((tr, tc), col(2)),
                  pl.BlockSpec((tr, nl), lambda i, j: (i, 0)),
                  par(2), par3(p["wd_w"]), par(2), par3(p["ad_w"]), par(p["g_up"].shape[0]),
                  par(1), par(1), par(1),
                  pl.BlockSpec((tc, tc), lambda i, j: (0, 0))],
        out_specs=[o1, o1, o1, o2, o2, o2, o1, o1],
        out_shape=[one, one, one, two, two, two, one, one],
        compiler_params=_params(("parallel", "parallel")),
        name="rwkv_prep",
    )(us_rkv, us_rkv, us_rkv, us_lora, p["w0"], p["w_up"], p["a0"], p["a_up"], p["g_up"], p["k_k"], p["k_a"],
      p["r_k"], _head_sum_matrix(tc))


def _rwkv_scan_kernel(r_ref, v_ref, kk_ref, lw_ref, kd_ref, b_ref, s0_ref, y_ref, sf_ref, s_scr, *, n_chunks):
    d = pl.program_id(1)
    c = pl.program_id(3)
    cl = RWKV_CHUNK

    @pl.when(c == 0)
    def _():
        s_scr[...] = s0_ref[...]

    sgn = 1 - 2 * d
    row = lax.broadcasted_iota(jnp.int32, (cl, cl), 0)
    col = lax.broadcasted_iota(jnp.int32, (cl, cl), 1)
    order = (row - col) * sgn
    incl = order >= 0
    strict = order > 0
    lw = lw_ref[...]
    tri =jnp.where(incl, 1.0, 0.0).astype(BF16)
    h1, h2, h3 = _split3(lw)
    cum = _dot(tri, h1) + _dot(tri, h2) + _dot(tri, h3)
    cum_last = jnp.where(d == 0, cum[cl - 1:cl, :], cum[0:1, :])
    r = r_ref[...]
    v = v_ref[...]
    kk = kk_ref[...]
    kd = kd_ref[...]
    b = b_ref[...]
    ginv = jnp.exp(-cum)
    to_end = jnp.exp(cum_last - cum)
    at = -kk * jnp.exp(cum - lw)
    rt = r * jnp.exp(cum)
    bk = jnp.concatenate([b * ginv, kd * ginv], axis=0).astype(BF16)
    bk_end = jnp.concatenate([b * to_end, kd * to_end], axis=0).astype(BF16)
    ar = jnp.concatenate([at, rt], axis=0)
    lane = lax.broadcasted_iota(jnp.int32, (cl, LANES), 1)
    first = lane < HEAD_DIM
    lane2 = lax.broadcasted_iota(jnp.int32, (2 * cl, LANES), 1)
    first2 = lane2 < HEAD_DIM
    s = s_scr[...]
    sb = s.astype(BF16)
    vb = v.astype(BF16)
    xs = _dot_nt(ar.astype(BF16), sb)
    x_u = xs[:cl]
    y_s = xs[cl:]
    eye = jnp.where(row == col, 1.0, 0.0)
    u_heads = []
    rbk = []
    for hh in range(2):
        mh = first2 if hh == 0 else jnp.logical_not(first2)
        m = _dot_nt(jnp.where(mh, ar, 0.0).astype(BF16), bk)
        a_ab = jnp.where(strict, m[:cl, :cl], 0.0)
        a_ak = jnp.where(strict, m[:cl, cl:], 0.0)
        a_rb = jnp.where(incl, m[cl:, :cl], 0.0)
        a_rk = jnp.where(incl, m[cl:, cl:], 0.0)
        tinv = eye + a_ab
        pw = a_ab
        for _ in range(int(math.log2(cl)) - 1):
            pwb = pw.astype(BF16)
            pw = _dot(pwb, pwb)
            tinv = tinv + _dot(tinv.astype(BF16), pw.astype(BF16))
        rhs = x_u + _dot(a_ak.astype(BF16), vb)
        u_heads.append(_dot(tinv.astype(BF16), rhs.astype(BF16)))
        rbk.append(jnp.concatenate([a_rb, a_rk], axis=1).astype(BF16))
    u = jnp.where(first, u_heads[0], u_heads[1])
    uv = jnp.concatenate([u, v], axis=0).astype(BF16)
    y = y_s + jnp.where(first, _dot(rbk[0], uv), _dot(rbk[1], uv))
    y_ref[...] = y
    rr = lax.broadcasted_iota(jnp.int32, (LANES, LANES), 0) // HEAD_DIM
    cc = lax.broadcasted_iota(jnp.int32, (LANES, LANES), 1) // HEAD_DIM
    s_new = s * jnp.exp(cum_last) + jnp.where(rr == cc, _dot_tn(uv, bk_end), 0.0)
    s_scr[...] = s_new

    @pl.when(c == n_chunks - 1)
    def _():
        sf_ref[...] = s_new


def _rwkv_scan(r, v, kk, lw, kd, b, s0, n_seg):
    t, a = r.shape
    seg = t // n_seg
    cl = RWKV_CHUNK
    nc = seg // cl
    npair = a // LANES

    def rows(bi, d, p, c):
        return bi * nc + jnp.where(d == 0, c, nc - 1 - c)

    shared = pl.BlockSpec((cl, LANES), lambda bi, d, p, c: (rows(bi, d, p, c), p))
    per_dir = pl.BlockSpec((None, cl, LANES), lambda bi, d, p, c: (d, rows(bi, d, p, c), p))
    state = pl.BlockSpec((None, None, None, LANES, LANES), lambda bi, d, p, c: (bi, d, p, 0, 0))
    return pl.pallas_call(
        functools.partial(_rwkv_scan_kernel, n_chunks=nc),
        grid=(n_seg, 2, npair, nc),
        in_specs=[shared, shared, shared, per_dir, per_dir, per_dir, state],
        out_specs=[per_dir, state],
        out_shape=[jax.ShapeDtypeStruct((2, t, a), F32), jax.ShapeDtypeStruct(s0.shape, F32)],
        scratch_shapes=[pltpu.VMEM((LANES, LANES), F32)],
        compiler_params=_params(("parallel", "parallel", "parallel", "arbitrary")),
        name="rwkv_scan",
    )(r, v, kk, lw, kd, b, s0)


def _rwkv_out_kernel(y_ref, g_ref, bv_ref, lnw_ref, lnb_ref, hs_ref, o_ref):
    y = y_ref[0] + y_ref[1]
    hs = hs_ref[...]
    inv = 1.0 / HEAD_DIM
    mu = _dot_exact_rhs(y, hs) * inv
    yc = y - mu
    var = _dot_exact_rhs(yc * yc, hs) * inv
    yn = yc * lax.rsqrt(var + GN_EPS) * lnw_ref[...] + lnb_ref[...]
    o_ref[...] = ((yn + bv_ref[...]) * g_ref[...]).astype(o_ref.dtype)


def _rwkv_out(y, g, bv, ln_w, ln_b):
    _, t, a = y.shape
    tr = _pick(t, (256, 128, 64, 32, 16, 8))
    tc = _pick(a, (512, 256, 128))
    o1 = pl.BlockSpec((tr, tc), lambda i, j: (i, j))
    par = pl.BlockSpec((1, tc), lambda i, j: (0, j))
    return pl.pallas_call(
        _rwkv_out_kernel,
        grid=(t // tr, a // tc),
        in_specs=[pl.BlockSpec((2, tr, tc), lambda i, j: (0, i, j)), o1, o1, par, par,
                  pl.BlockSpec((tc, tc), lambda i, j: (0, 0))],
        out_specs=o1,
        out_shape=jax.ShapeDtypeStruct((t, a), BF16),
        compiler_params=_params(("parallel", "parallel")),
        name="rwkv_out",
    )(y, g, bv, ln_w.reshape(1, a), ln_b.reshape(1, a), _head_sum_matrix(tc))


def _rwkv_params(shift_mu, w0, w_up, a0, a_up, g_up, k_k, k_a, r_k, a_width):
    dr, ar_, gr = w_up.shape[1], a_up.shape[1], g_up.shape[0]
    pad = lambda n: -(-n // LANES) * LANES
    wd_w, ad_w, gd_w = pad(2 * dr), pad(2 * ar_), pad(gr)

    def up(wu, rank, width):
        out = jnp.zeros((2, width, a_width), F32)
        for d in range(2):
            out = out.at[d, d * rank:(d + 1) * rank].set(wu[d])
        return out.astype(BF16)

    return dict(wd_w=wd_w, ad_w=ad_w, gd_w=gd_w, dr=dr, ar=ar_, gr=gr,
                w0=w0, a0=a0, w_up=up(w_up, dr, wd_w), a_up=up(a_up, ar_, ad_w),
                g_up=jnp.pad(g_up, ((0, gd_w - gr), (0, 0))).astype(BF16),
                k_k=k_k.reshape(1, -1), k_a=k_a.reshape(1, -1), r_k=r_k.reshape(1, -1))


def _pad_lora_cols(w, dr2, ar2, gr, p):
    parts = [(w[..., :dr2], p["wd_w"]), (w[..., dr2:dr2 + ar2], p["ad_w"]), (w[..., dr2 + ar2:], p["gd_w"])]
    return jnp.concatenate([jnp.pad(x, [(0, 0)] * (x.ndim - 1) + [(0, wd - x.shape[-1])]) for x, wd in parts],
                           axis=-1)


def _na_bias(rpb, rows):
    kr = min(WIN_R, rows)
    col = np.arange(GRID_W)
    col_start = np.clip(col - WIN_C // 2, 0, GRID_W - WIN_C)
    kc = np.arange(GRID_W)
    inside = (kc[None, :] >= col_start[:, None]) & (kc[None, :] < col_start[:, None] + WIN_C)
    col_off = np.clip(kc[None, :] - col[:, None] + (WIN_C - 1), 0, 2 * WIN_C - 2)
    pats = np.arange(kr)
    row_off = np.clip(np.arange(kr)[None, :] - pats[:, None] + (WIN_R - 1), 0, 2 * WIN_R - 2)
    bias = rpb[:, row_off[:, None, :, None], col_off[None, :, None, :]]
    bias = jnp.where(inside[None, None, :, None, :], bias, NEG_BIG)
    return bias.reshape(rpb.shape[0], kr, GRID_W, kr * GRID_W).astype(F32)


def _softmax_pv(s_list, v_list):
    m = s_list[0].max(axis=-1, keepdims=True)
    for s in s_list[1:]:
        m = jnp.maximum(m, s.max(axis=-1, keepdims=True))
    den = 0.0
    acc = 0.0
    for s, vv in zip(s_list, v_list):
        p = jnp.exp(s - m)
        den = den + p.sum(axis=-1, keepdims=True)
        acc = acc + _dot(p.astype(BF16), vv)
    return acc / den


def _na_kernel(q_ref, k_ref, v_ref, kc_ref, vc_ref, bias_ref, o_ref, *, rows, kr):
    w = GRID_W
    lane = lax.broadcasted_iota(jnp.int32, (w, LANES), 1)
    first = lane < HEAD_DIM
    kc = kc_ref[...]
    vc = vc_ref[...]
    scale = HEAD_DIM ** -0.5

    def body(rr, carry):
        r0 = jnp.clip(rr - kr // 2, 0, rows - kr)
        pat = rr - r0
        q = (q_ref[pl.ds(pl.multiple_of(rr * w, w), w), :].astype(F32) * scale).astype(BF16)
        kw = k_ref[pl.ds(pl.multiple_of(r0 * w, w), kr * w), :]
        vw = v_ref[pl.ds(pl.multiple_of(r0 * w, w), kr * w), :]
        outs = []
        for hh in range(2):
            mh = first if hh == 0 else jnp.logical_not(first)
            qm = jnp.where(mh, q, jnp.zeros_like(q))
            s_loc = _dot_nt(qm, kw) + bias_ref[hh, pl.ds(pat, 1)][0]
            s_ctx = _dot_nt(qm, kc)
            outs.append(_softmax_pv([s_loc, s_ctx], [vw, vc]))
        o_ref[pl.ds(pl.multiple_of(rr * w, w), w), :] = jnp.where(first, outs[0], outs[1]).astype(o_ref.dtype)
        return carry

    lax.fori_loop(0, rows, body, 0)


def _ctx_attn_kernel(q_ref, k_ref, v_ref, o_ref):
    n = q_ref.shape[0]
    lane = lax.broadcasted_iota(jnp.int32, (n, LANES), 1)
    first = lane < HEAD_DIM
    q = (q_ref[...].astype(F32) * HEAD_DIM ** -0.5).astype(BF16)
    k = k_ref[...]
    v = v_ref[...]
    outs = []
    for hh in range(2):
        mh = first if hh == 0 else jnp.logical_not(first)
        qm = jnp.where(mh, q, jnp.zeros_like(q))
        outs.append(_softmax_pv([_dot_nt(qm, k)], [v]))
    o_ref[...] = jnp.where(first, outs[0], outs[1]).astype(o_ref.dtype)


def _attention(qkv_x, qkv_c, rpb, n_seg, need_ctx):
    tx, w3 = qkv_x.shape
    bw = w3 // 3
    npair = bw // LANES
    lx = tx // n_seg
    lc = qkv_c.shape[0] // n_seg
    rows = lx // GRID_W
    kr = min(WIN_R, rows)
    bias = _na_bias(rpb, rows)
    blk = lambda length, off: pl.BlockSpec((length, LANES), lambda bi, p: (bi, off * npair + p))
    nx = pl.pallas_call(
        functools.partial(_na_kernel, rows=rows, kr=kr),
        grid=(n_seg, npair),
        in_specs=[blk(lx, 0), blk(lx, 1), blk(lx, 2), blk(lc, 1), blk(lc, 2),
                  pl.BlockSpec((2, kr, GRID_W, kr * GRID_W), lambda bi, p: (p, 0, 0, 0))],
        out_specs=pl.BlockSpec((lx, LANES), lambda bi, p: (bi, p)),
        out_shape=jax.ShapeDtypeStruct((tx, bw), BF16),
        compiler_params=_params(("parallel", "parallel")),
        name="neighbourhood_attention",
    )(qkv_x, qkv_x, qkv_x, qkv_c, qkv_c, bias)
    ncx = None
    if need_ctx:
        ncx = pl.pallas_call(
            _ctx_attn_kernel,
            grid=(n_seg, npair),
            in_specs=[blk(lc, 0), blk(lc, 1), blk(lc, 2)],
            out_specs=pl.BlockSpec((lc, LANES), lambda bi, p: (bi, p)),
            out_shape=jax.ShapeDtypeStruct((qkv_c.shape[0], bw), BF16),
            compiler_params=_params(("parallel", "parallel")),
            name="context_attention",
        )(qkv_c, qkv_c, qkv_c)
    return nx, ncx


def _even_mixer(hx, hc, n_seg, w_in, shift_mu, w0, w_up, a0, a_up, g_up, k_k, k_a, r_k, ln_w, ln_b, rpb, w_out,
                need_ctx):
    d = hx.shape[1]
    a_width = k_k.shape[0]
    p = _rwkv_params(shift_mu, w0, w_up, a0, a_up, g_up, k_k, k_a, r_k, a_width)
    dr2, ar2, gr = 2 * p["dr"], 2 * p["ar"], p["gr"]
    a_cols = 3 * a_width + dr2 + ar2 + gr
    w_rkv = w_in[:, :3 * a_width].astype(BF16)
    w_lora = _pad_lora_cols(w_in[:, 3 * a_width:a_cols], dr2, ar2, gr, p).astype(BF16)
    w_qkv = w_in[:, a_cols:].astype(BF16)
    mu_rkv = shift_mu[:3 * a_width].reshape(1, -1)
    mu_lora = _pad_lora_cols(shift_mu[3 * a_width:], dr2, ar2, gr, p).reshape(1, -1)
    w_out_b = w_out.astype(BF16)

    def rwkv_side(h, s0):
        seg = h.shape[0] // n_seg
        us_rkv = _row_neighbour_call(_shift_kernel, _matmul(h, w_rkv), seg, [mu_rkv], "token_shift")
        us_lora = _row_neighbour_call(_shift_kernel, _matmul(h, w_lora), seg, [mu_lora], "token_shift")
        r, v, kk, lw, kd, b, g, bv = _rwkv_prep(us_rkv, us_lora, p, seg)
        y, s_fin = _rwkv_scan(r, v, kk, lw, kd, b, s0, n_seg)
        return (y, g, bv), s_fin

    s_zero = jnp.zeros((n_seg, 2, a_width // LANES, LANES, LANES), F32)
    terms_c, s_ctx = rwkv_side(hc, s_zero)
    terms_x, _ = rwkv_side(hx, s_ctx)
    qkv_x = _matmul(hx, w_qkv, out_dtype=BF16)
    qkv_c = _matmul(hc, w_qkv, out_dtype=BF16)
    nx, ncx = _attention(qkv_x, qkv_c, rpb, n_seg, need_ctx)
    rx = _rwkv_out(*terms_x, ln_w, ln_b)
    ox = _matmul(jnp.concatenate([rx, nx], axis=1), w_out_b)
    oc = None
    if need_ctx:
        rc = _rwkv_out(*terms_c, ln_w, ln_b)
        oc = _matmul(jnp.concatenate([rc, ncx], axis=1), w_out_b)
    return ox, oc


def _softplus(x):
    return jnp.maximum(x, 0.0) + jnp.log(1.0 + jnp.exp(-jnp.abs(x)))


def _ssd_scan_kernel(x_ref, b_ref, c_ref, dt_ref, bias_ref, alog_ref, s0_ref, y_ref, sf_ref, s_scr, *,
                     n_chunks, n_e):
    d = pl.program_id(1)
    g = pl.program_id(2)
    c = pl.program_id(3)
    cl = x_ref.shape[0]
    hp = dt_ref.shape[1]

    @pl.when(c == 0)
    def _():
        s_scr[...] = s0_ref[...]

    sgn = 1 - 2 * d
    row = lax.broadcasted_iota(jnp.int32, (cl, cl), 0)
    col = lax.broadcasted_iota(jnp.int32, (cl, cl), 1)
    incl = (row - col) * sgn >= 0
    tri = jnp.where(incl, 1.0, 0.0).astype(BF16)
    dt_all = _softplus(dt_ref[...] + bias_ref[...])
    dta_all = dt_all * (-jnp.exp(alog_ref[...]))
    hr = lax.broadcasted_iota(jnp.int32, (hp, LANES), 0)
    hc = lax.broadcasted_iota(jnp.int32, (hp, LANES), 1)
    sel = jnp.where(jnp.logical_and(hr == g * n_e + hc, hc < n_e), 1.0, 0.0).astype(BF16)
    dt_g = _dot_exact_rhs(dt_all, sel)
    h1, h2, h3 = _split3(_dot_exact_rhs(dta_all, sel))
    cum = _dot(tri, h1) + _dot(tri, h2) + _dot(tri, h3)
    cum_t = cum.T
    cum_last = jnp.where(d == 0, cum[cl - 1:cl, :], cum[0:1, :])
    e_cum = jnp.exp(cum)
    e_end = jnp.exp(cum_last - cum)
    e_last = jnp.exp(cum_last)
    bm = b_ref[...].astype(BF16)
    cm = c_ref[...].astype(BF16)
    cb = _dot_nt(cm, bm)
    lane = lax.broadcasted_iota(jnp.int32, (cl, LANES), 1)
    first = lane < SSM_HEAD_DIM
    first_row = first[0:1, :]
    state = s_scr[...]
    y_state = _dot(cm, state.astype(BF16))
    xdt_end = []
    decays = []
    for q in range(n_e // 2):
        j0, j1 = 2 * q, 2 * q + 1
        lo, hi = q * LANES, (q + 1) * LANES
        pick = lambda t, rows=first: jnp.where(rows, t[:, j0:j0 + 1], t[:, j1:j1 + 1])
        xdt = x_ref[:, lo:hi] * pick(dt_g)
        xdt_b = xdt.astype(BF16)
        ys = []
        for j in (j0, j1):
            seg = cum[:, j:j + 1] - cum_t[j:j + 1, :]
            m = cb * jnp.exp(jnp.where(incl, seg, NEG_BIG))
            ys.append(_dot(m.astype(BF16), xdt_b))
        y_ref[:, lo:hi] = jnp.where(first, ys[0], ys[1]) + y_state[:, lo:hi] * pick(e_cum)
        xdt_end.append((xdt * pick(e_end)).astype(BF16))
        decays.append(jnp.where(first_row, e_last[:, j0:j0 + 1], e_last[:, j1:j1 + 1]))
    upd = _dot_tn(bm, jnp.concatenate(xdt_end, axis=1))
    s_new = state * jnp.concatenate(decays, axis=1) + upd
    s_scr[...] = s_new

    @pl.when(c == n_chunks - 1)
    def _():
        sf_ref[...] = s_new


def _ssd_scan(xbc, dt_raw, dt_bias, a_log, s0, n_seg, inner):
    t = xbc.shape[0]
    seg = t // n_seg
    cl = min(SSM_CHUNK, seg)
    nc = seg // cl
    ep = inner // SSM_GROUPS
    n_e = ep // SSM_HEAD_DIM
    hp = dt_raw.shape[1] // 2
    nb = inner // SSM_STATE

    def rows(bi, d, g, c):
        return bi * nc + jnp.where(d == 0, c, nc - 1 - c)

    state = pl.BlockSpec((None, None, None, SSM_STATE, ep), lambda bi, d, g, c: (bi, d, g, 0, 0))
    par = pl.BlockSpec((None, 1, hp), lambda bi, d, g, c: (d, 0, 0))
    return pl.pallas_call(
        functools.partial(_ssd_scan_kernel, n_chunks=nc, n_e=n_e),
        grid=(n_seg, 2, SSM_GROUPS, nc),
        in_specs=[pl.BlockSpec((cl, ep), lambda bi, d, g, c: (rows(bi, d, g, c), g)),
                  pl.BlockSpec((cl, SSM_STATE), lambda bi, d, g, c: (rows(bi, d, g, c), nb + g)),
                  pl.BlockSpec((cl, SSM_STATE), lambda bi, d, g, c: (rows(bi, d, g, c), nb + SSM_GROUPS + g)),
                  pl.BlockSpec((cl, hp), lambda bi, d, g, c: (rows(bi, d, g, c), d)),
                  par, par, state],
        out_specs=[pl.BlockSpec((None, cl, ep), lambda bi, d, g, c: (d, rows(bi, d, g, c), g)), state],
        out_shape=[jax.ShapeDtypeStruct((2, t, inner), F32), jax.ShapeDtypeStruct(s0.shape, F32)],
        scratch_shapes=[pltpu.VMEM((SSM_STATE, ep), F32)],
        compiler_params=_params(("parallel", "parallel", "parallel", "arbitrary")),
        name="ssd_scan",
    )(xbc, xbc, xbc, dt_raw, dt_bias, a_log, s0)


def _ssm_out_kernel(y_ref, xs_ref, z_ref, dsk_ref, nw_ref, o_ref):
    z = z_ref[...]
    y = (y_ref[0] + y_ref[1] + dsk_ref[...] * xs_ref[...]) * (z * jax.nn.sigmoid(z))
    o_ref[...] = _rms(y, nw_ref[...]).astype(o_ref.dtype)


def _ssm_out(y, xbc, z, d_skip_cols, norm_w):
    _, t, inner = y.shape
    gw = inner // SSM_GROUPS
    tr = _pick(t, (256, 128, 64, 32, 16, 8))
    blk = pl.BlockSpec((tr, gw), lambda i, g: (i, g))
    par = pl.BlockSpec((1, gw), lambda i, g: (0, g))
    return pl.pallas_call(
        _ssm_out_kernel,
        grid=(t // tr, SSM_GROUPS),
        in_specs=[pl.BlockSpec((2, tr, gw), lambda i, g: (0, i, g)), blk, blk, par, par],
        out_specs=blk,
        out_shape=jax.ShapeDtypeStruct((t, inner), BF16),
        compiler_params=_params(("parallel", "parallel")),
        name="ssm_out",
    )(y, xbc, z, d_skip_cols, norm_w.reshape(1, inner))


def _odd_mixer(hx, hc, n_seg, w_in, conv_w, conv_b, dt_bias, a_log, d_skip, norm_w, w_out, need_ctx):
    inner = norm_w.shape[0]
    n_heads = d_skip.shape[0]
    conv_dim = conv_w.shape[1]
    hp = -(-n_heads // LANES) * LANES
    w_z = w_in[:, :inner].astype(BF16)
    w_xbc = w_in[:, inner:inner + conv_dim].astype(BF16)
    w_dt = jnp.pad(w_in[:, inner + conv_dim:].reshape(-1, 2, n_heads),
                   ((0, 0), (0, 0), (0, hp - n_heads))).reshape(-1, 2 * hp).astype(BF16)
    pad_h = lambda p: jnp.pad(p, ((0, 0), (0, hp - n_heads))).reshape(2, 1, hp)
    bias_p, alog_p = pad_h(dt_bias), pad_h(a_log)
    d_cols = jnp.repeat(d_skip, SSM_HEAD_DIM).reshape(1, inner)
    w_out_b = w_out.astype(BF16)
    conv_b2 = conv_b.reshape(1, conv_dim)

    def side(h, s0, need_out):
        seg = h.shape[0] // n_seg
        z = _matmul(h, w_z)
        xbc = _row_neighbour_call(_conv_kernel, _matmul(h, w_xbc), seg, [conv_w, conv_b2], "conv_silu")
        dt_raw = _matmul(h, w_dt)
        y, s_fin = _ssd_scan(xbc, dt_raw, bias_p, alog_p, s0, n_seg, inner)
        out = _matmul(_ssm_out(y, xbc, z, d_cols, norm_w), w_out_b) if need_out else None
        return out, s_fin

    s_zero = jnp.zeros((n_seg, 2, SSM_GROUPS, SSM_STATE, inner // SSM_GROUPS), F32)
    oc, s_ctx = side(hc, s_zero, need_ctx)
    ox, _ = side(hx, s_ctx, True)
    return ox, oc


def kernel(x, c, ctx, c_ctx, ada_w, ada_b, norm_g, ev_w_in, ev_shift_mu, rk_w0, rk_w_up, rk_a0, rk_a_up, rk_g_up,
           rk_k_k, rk_k_a, rk_r_k, rk_ln_w, rk_ln_b, na_rpb, ev_w_out, od_w_in, od_conv_w, od_conv_b, od_dt_bias,
           od_a_log, od_d, od_norm_w, od_w_out, router_w, router_b, moe_w1, moe_b1, moe_w2, moe_b2):
    n_b, n_seq, d = x.shape
    n_ctx = ctx.shape[1]
    depth = ada_w.shape[0]
    xt = x.reshape(n_b * n_seq, d)
    ct = ctx.reshape(n_b * n_ctx, d)
    n_cond = -(-(n_b + 1) // SUBLANES) * SUBLANES
    cond = jnp.zeros((n_cond, d), F32).at[:n_b].set(c).at[n_b].set(c_ctx)
    mods = _adaln(cond, ada_w, ada_b)
    hx = hc = None
    for layer in range(depth):
        need_ctx = layer < depth - 1
        i = layer // 2
        mod_x = mods[layer, :n_b].reshape(n_b, 6, d)
        mod_c = jnp.broadcast_to(mods[layer, n_b].reshape(1, 6, d), (n_b, 6, d))
        g = norm_g[layer]
        if layer == 0:
            hx = _norm_mod(xt, g[0], mod_x, 0, 1)
            hc = _norm_mod(ct, g[0], mod_c, 0, 1)
        if layer % 2 == 0:
            ox, oc = _even_mixer(hx, hc, n_b, ev_w_in[i], ev_shift_mu[i], rk_w0[i], rk_w_up[i], rk_a0[i],
                                 rk_a_up[i], rk_g_up[i], rk_k_k[i], rk_k_a[i], rk_r_k[i], rk_ln_w[i], rk_ln_b[i],
                                 na_rpb[i], ev_w_out[i], need_ctx)
        else:
            ox, oc = _odd_mixer(hx, hc, n_b, od_w_in[i], od_conv_w[i], od_conv_b[i], od_dt_bias[i], od_a_log[i],
                                od_d[i], od_norm_w[i], od_w_out[i], need_ctx)
        router = (router_w[layer], router_b[layer])
        experts = _moe_weights(moe_w1[layer], moe_b1[layer], moe_w2[layer], moe_b2[layer])
        streams = [(xt, ox, mod_x)] + ([(ct, oc, mod_c)] if need_ctx else [])
        new = []
        for tok, o, mod in streams:
            tok, h2, top_idx, gate = _residual(tok, o, g[1], mod, 2, g_b=g[2], i_shift=3, i_scale=4, router=router)
            slots = _moe(h2, top_idx[:, :TOP_K], gate[:, :TOP_K], *experts)
            if layer + 1 < depth:
                g_next = norm_g[layer + 1]
                mod_next = (mods[layer + 1, :n_b].reshape(n_b, 6, d) if mod is mod_x else
                            jnp.broadcast_to(mods[layer + 1, n_b].reshape(1, 6, d), (n_b, 6, d)))
                tok, h_next = _residual(tok, slots, g[3], mod, 5, n_sum=TOP_K, g_b=g_next[0], i_shift=0, i_scale=1,
                                        h_dtype=BF16, mod_h=mod_next)
            else:
                (tok,) = _residual(tok, slots, g[3], mod, 5, n_sum=TOP_K)
                h_next = None
            new.append((tok, h_next))
        xt, hx = new[0]
        if need_ctx:
            ct, hc = new[1]
    return xt.reshape(n_b, n_seq, d)
```

```python
import functools
import math

import numpy as np
import jax
import jax.numpy as jnp
from jax import lax
from jax.experimental import pallas as pl
from jax.experimental.pallas import tpu as pltpu

F32 = jnp.float32
BF16 = jnp.bfloat16

RMS_EPS = 1e-6
GN_EPS = 64e-5
HEAD_DIM = 64
GRID_W = 64
WIN_R = 8
WIN_C = 16
SSM_HEAD_DIM = 64
SSM_GROUPS = 8
SSM_STATE = 128
SSM_CHUNK = 128
TOP_K = 4
SWIGLU_ALPHA = 1.702
SWIGLU_LIMIT = 7.0
MOE_ROWS = 256
RWKV_CHUNK = 64
NA_ROWS_PER_STEP = 4
RWKV_PAIRS_PER_STEP = 8
LANES = 128
SUBLANES = 8
VMEM_LIMIT = 56 * 1024 * 1024
NEG_BIG = -1e30


def _pick(n, cands):
    for c in cands:
        if n % c == 0:
            return c
    return n


def _params(sem, vmem=VMEM_LIMIT):
    return pltpu.CompilerParams(dimension_semantics=sem, vmem_limit_bytes=vmem)


def _split3(x):
    hi = x.astype(BF16)
    r1 = x - hi.astype(F32)
    mid = r1.astype(BF16)
    lo = (r1 - mid.astype(F32)).astype(BF16)
    return hi, mid, lo


def _dot(a, b):
    return jnp.dot(a, b, preferred_element_type=F32)


def _dot_nt(a, b):
    return lax.dot_general(a, b, (((1,), (1,)), ((), ())), preferred_element_type=F32)


def _dot_tn(a, b):
    return lax.dot_general(a, b, (((0,), (0,)), ((), ())), preferred_element_type=F32)


def _dot_exact_rhs(x, m_bf16):
    hi, mid, lo = _split3(x)
    return _dot(hi, m_bf16) + _dot(mid, m_bf16) + _dot(lo, m_bf16)


def _mm_kernel(a_ref, w_ref, b_ref, o_ref, acc_ref, *, nk, pre):
    k = pl.program_id(2)

    @pl.when(k == 0)
    def _():
        acc_ref[...] = jnp.zeros_like(acc_ref)

    a = a_ref[...]
    if pre == "silu":
        a = a * jax.nn.sigmoid(a)
    acc_ref[...] += _dot(a.astype(BF16), w_ref[...].astype(BF16))

    @pl.when(k == nk - 1)
    def _():
        o_ref[...] = (acc_ref[...] + b_ref[...]).astype(o_ref.dtype)


def _matmul(a, w, bias=None, out_dtype=F32, pre=None):
    m, kdim = a.shape
    n = w.shape[1]
    tm = _pick(m, (1024, 512, 256, 128, 64, 32, 16, 8))
    tn = _pick(n, (1024, 768, 640, 512, 384, 256, 128))
    tk = _pick(kdim, (1024, 512, 256, 128))
    nk = kdim // tk
    if bias is None:
        bias = jnp.zeros((n,), F32)
    return pl.pallas_call(
        functools.partial(_mm_kernel, nk=nk, pre=pre),
        grid=(m // tm, n // tn, nk),
        in_specs=[pl.BlockSpec((tm, tk), lambda i, j, k: (i, k)),
                  pl.BlockSpec((tk, tn), lambda i, j, k: (k, j)),
                  pl.BlockSpec((1, tn), lambda i, j, k: (0, j))],
        out_specs=pl.BlockSpec((tm, tn), lambda i, j, k: (i, j)),
        out_shape=jax.ShapeDtypeStruct((m, n), out_dtype),
        scratch_shapes=[pltpu.VMEM((tm, tn), F32)],
        compiler_params=_params(("parallel", "parallel", "arbitrary")),
        name="matmul",
    )(a, w, bias.reshape(1, n).astype(F32))


def _adaln_kernel(c_ref, w_ref, b_ref, o_ref, acc_ref, *, nk):
    k = pl.program_id(2)

    @pl.when(k == 0)
    def _():
        acc_ref[...] = jnp.zeros_like(acc_ref)

    c = c_ref[...]
    c = c * jax.nn.sigmoid(c)
    acc_ref[...] += _dot(c.astype(BF16), w_ref[...].astype(BF16))

    @pl.when(k == nk - 1)
    def _():
        o_ref[...] = acc_ref[...] + b_ref[...]


def _adaln(cond, ada_w, ada_b):
    depth, d, n = ada_w.shape
    r = cond.shape[0]
    tn = _pick(n, (2048, 1024, 512, 256, 128))
    tk = _pick(d, (1024, 512, 256, 128))
    nk = d // tk
    return pl.pallas_call(
        functools.partial(_adaln_kernel, nk=nk),
        grid=(depth, n // tn, nk),
        in_specs=[pl.BlockSpec((r, tk), lambda l, j, k: (0, k)),
                  pl.BlockSpec((None, tk, tn), lambda l, j, k: (l, k, j)),
                  pl.BlockSpec((None, 1, tn), lambda l, j, k: (l, 0, j))],
        out_specs=pl.BlockSpec((None, r, tn), lambda l, j, k: (l, 0, j)),
        out_shape=jax.ShapeDtypeStruct((depth, r, n), F32),
        scratch_shapes=[pltpu.VMEM((r, tn), F32)],
        compiler_params=_params(("parallel", "parallel", "arbitrary")),
        name="adaln",
    )(cond, ada_w, ada_b.reshape(depth, 1, n))


def _rms(x, g):
    return x * lax.rsqrt(jnp.mean(x * x, axis=-1, keepdims=True) + RMS_EPS) * g


def _norm_mod_kernel(x_ref, g_ref, mod_ref, o_ref, *, i_shift, i_scale):
    y = _rms(x_ref[...], g_ref[...])
    o = y * (1.0 + mod_ref[i_scale:i_scale + 1, :]) + mod_ref[i_shift:i_shift + 1, :]
    o_ref[...] = o.astype(o_ref.dtype)


def _norm_mod(x, g, mod, i_shift, i_scale, out_dtype=BF16):
    t, d = x.shape
    seg = t // mod.shape[0]
    tr = _pick(seg, (256, 128, 64, 32, 16, 8))
    per = seg // tr
    return pl.pallas_call(
        functools.partial(_norm_mod_kernel, i_shift=i_shift, i_scale=i_scale),
        grid=(t // tr,),
        in_specs=[pl.BlockSpec((tr, d), lambda i: (i, 0)),
                  pl.BlockSpec((1, d), lambda i: (0, 0)),
                  pl.BlockSpec((None, 6, d), lambda i: (i // per, 0, 0))],
        out_specs=pl.BlockSpec((tr, d), lambda i: (i, 0)),
        out_shape=jax.ShapeDtypeStruct((t, d), out_dtype),
        compiler_params=_params(("parallel",)),
        name="norm_mod",
    )(x, g.reshape(1, d), mod)


def _residual_kernel(*refs, n_sum, i_gate, i_shift, i_scale, with_h, with_router, n_exp):
    x_ref = refs[0]
    o_refs = refs[1:1 + n_sum]
    ga_ref, mod_ref = refs[1 + n_sum:3 + n_sum]
    pos = 3 + n_sum
    if n_sum > 1:
        sg_ref = refs[pos]
        pos += 1
    if with_h:
        gb_ref, modh_ref = refs[pos:pos + 2]
        pos += 2
    if with_router:
        rw_ref, rb_ref = refs[pos:pos + 2]
        pos += 2
    xo_ref = refs[pos]
    pos += 1
    if n_sum == 1:
        o = o_refs[0][...]
    else:
        sg = sg_ref[...]
        o = o_refs[0][...] * sg[:, 0:1]
        for j in range(1, n_sum):
            o = o + o_refs[j][...] * sg[:, j:j + 1]
    xn =x_ref[...] + mod_ref[i_gate:i_gate + 1, :] * _rms(o, ga_ref[...])
    xo_ref[...] = xn
    if not with_h:
        return
    h_ref = refs[pos]
    pos += 1
    h = _rms(xn, gb_ref[...]) * (1.0 + modh_ref[i_scale:i_scale + 1, :]) + modh_ref[i_shift:i_shift + 1, :]
    h_ref[...] = h.astype(h_ref.dtype)
    if not with_router:
        return
    idx_ref, gate_ref = refs[pos:pos + 2]
    h1, h2, h3 = _split3(h)
    w1 = rw_ref[0]
    w2 = rw_ref[1]
    logits = (_dot(h1, w1) + _dot(h2, w1) + _dot(h1, w2) + _dot(h3, w1) + _dot(h2, w2)) + rb_ref[...]
    lane = lax.broadcasted_iota(jnp.int32, logits.shape, 1)
    logits = jnp.where(lane < n_exp, logits, NEG_BIG)
    vals, idxs = [], []
    for _ in range(TOP_K):
        m = jnp.max(logits, axis=-1, keepdims=True)
        ix = jnp.min(jnp.where(logits == m, lane, 1 << 30), axis=-1, keepdims=True)
        vals.append(m)
        idxs.append(ix)
        logits = jnp.where(lane == ix, NEG_BIG * 2, logits)
    es = [jnp.exp(v - vals[0]) for v in vals]
    den = es[0]
    for e in es[1:]:
        den = den + e
    idx_out = jnp.zeros(lane.shape, jnp.int32)
    gate_out = jnp.zeros(lane.shape, F32)
    for j in range(TOP_K):
        idx_out = jnp.where(lane == j, idxs[j], idx_out)
        gate_out = jnp.where(lane == j, es[j] / den, gate_out)
    idx_ref[...] = idx_out
    gate_ref[...] = gate_out


def _residual(x, o, g_a, mod, i_gate, n_sum=1, g_b=None, i_shift=0, i_scale=0, h_dtype=F32, router=None,
              mod_h=None, slot_gate=None):
    t, d = x.shape
    seg = t // mod.shape[0]
    tr = _pick(seg, (128, 64, 32, 16, 8))
    per = seg // tr
    nt = t // tr
    with_h = g_b is not None
    with_router = router is not None
    row = lambda i: (i, 0)
    fixed = lambda i: (0, 0)
    ins = [x] + [o] * n_sum + [g_a.reshape(1, d), mod]
    in_specs = ([pl.BlockSpec((tr, d), row)]
                + [pl.BlockSpec((tr, d), functools.partial(lambda j, i: (j * nt + i, 0), j)) for j in range(n_sum)]
                + [pl.BlockSpec((1, d), fixed), pl.BlockSpec((None, 6, d), lambda i: (i // per, 0, 0))])
    outs = [jax.ShapeDtypeStruct((t, d), F32)]
    out_specs = [pl.BlockSpec((tr, d), row)]
    n_exp = 0
    if n_sum > 1:
        ins.append(slot_gate)
        in_specs.append(pl.BlockSpec((tr, slot_gate.shape[1]), row))
    if with_h:
        ins += [g_b.reshape(1, d), mod if mod_h is None else mod_h]
        in_specs += [pl.BlockSpec((1, d), fixed), pl.BlockSpec((None, 6, d), lambda i: (i // per, 0, 0))]
        outs.append(jax.ShapeDtypeStruct((t, d), h_dtype))
        out_specs.append(pl.BlockSpec((tr, d), row))
    if with_router:
        rw, rb = router
        n_exp = rw.shape[1]
        rw = jnp.pad(rw, ((0, 0), (0, LANES - n_exp)))
        rw_hi = rw.astype(BF16)
        rw_lo = (rw - rw_hi.astype(F32)).astype(BF16)
        ins += [jnp.stack([rw_hi, rw_lo]), jnp.pad(rb, (0, LANES - n_exp)).reshape(1, LANES)]
        in_specs += [pl.BlockSpec((2, d, LANES), lambda i: (0, 0, 0)), pl.BlockSpec((1, LANES), fixed)]
        outs += [jax.ShapeDtypeStruct((t, LANES), jnp.int32), jax.ShapeDtypeStruct((t, LANES), F32)]
        out_specs += [pl.BlockSpec((tr, LANES), row), pl.BlockSpec((tr, LANES), row)]
    res = pl.pallas_call(
        functools.partial(_residual_kernel, n_sum=n_sum, i_gate=i_gate, i_shift=i_shift, i_scale=i_scale,
                          with_h=with_h, with_router=with_router, n_exp=n_exp),
        grid=(t // tr,),
        in_specs=in_specs,
        out_specs=out_specs,
        out_shape=outs,
        compiler_params=_params(("parallel",)),
        name="residual",
    )(*ins)
    return res


def _moe_kernel(nb_ref, be_ref, qs_ref, cnt_ref, order_ref, h_hbm, w1_ref, b1_ref, w2_ref, b2_ref, out_hbm,
                xbuf, obuf, sem_in, sem_out, *, bm, n_tok, n_assign, f):
    del be_ref
    i = pl.program_id(0)
    nb = nb_ref[0]
    cur = i % 2
    shift = TOP_K.bit_length() - 1

    def assignment(blk, r):
        a = order_ref[jnp.minimum(qs_ref[blk] + r, n_assign - 1)]
        return a, r < cnt_ref[blk]

    def gather_copy(blk, buf, r):
        a, valid = assignment(blk, r)
        tok = jnp.where(valid, lax.shift_right_logical(a, shift), 0)
        return pltpu.make_async_copy(h_hbm.at[pl.ds(tok, 1)], xbuf.at[buf, pl.ds(r, 1)], sem_in.at[buf])

    def scatter_copy(blk, buf, r):
        a, _ = assignment(blk, r)
        slot = (a & (TOP_K - 1)) * n_tok + lax.shift_right_logical(a, shift)
        return pltpu.make_async_copy(obuf.at[buf, pl.ds(r, 1)], out_hbm.at[pl.ds(slot, 1)], sem_out.at[buf])

    def for_rows(fn, n_rows=None):
        def body(r, c):
            fn(r)
            return c
        if n_rows is None:
            lax.fori_loop(0, bm, body, 0, unroll=8)
        else:
            lax.fori_loop(0, n_rows, body, 0)

    @pl.when(i < nb)
    def _():
        @pl.when(i == 0)
        def _():
            for_rows(lambda r: gather_copy(0, 0, r).start())

        @pl.when(i + 1 < nb)
        def _():
            for_rows(lambda r: gather_copy(i + 1, 1 - cur, r).start())

        for_rows(lambda r: gather_copy(i, cur, r).wait())
        x = xbuf[cur].astype(BF16)
        u = _dot(x, w1_ref[...]) + b1_ref[...]
        glu = jnp.minimum(u[:, :f], SWIGLU_LIMIT)
        lin = jnp.clip(u[:, f:], -SWIGLU_LIMIT, SWIGLU_LIMIT)
        act = glu * jax.nn.sigmoid(SWIGLU_ALPHA * glu) * (lin + 1.0)
        out = _dot(act.astype(BF16), w2_ref[...]) + b2_ref[...]

        @pl.when(i >= 1)
        def _():
            for_rows(lambda r: scatter_copy(i - 1, 1 - cur, r).wait(), cnt_ref[i - 1])

        obuf[cur] = out
        for_rows(lambda r: scatter_copy(i, cur, r).start(), cnt_ref[i])

        @pl.when(i == nb - 1)
        def _():
            for_rows(lambda r: scatter_copy(i, cur, r).wait(), cnt_ref[i])


def _moe(h, top_idx, w1, b1, w2, b2):
    assert TOP_K & (TOP_K - 1) == 0
    t, d = h.shape
    n_exp, f = w2.shape[0], w2.shape[1]
    bm = MOE_ROWS
    n_assign = t * TOP_K
    flat_e = top_idx[:, :TOP_K].reshape(-1)
    order = jnp.argsort(flat_e, stable=True).astype(jnp.int32)
    counts = jnp.sum(flat_e[:, None] == jnp.arange(n_exp, dtype=flat_e.dtype)[None, :], axis=0, dtype=jnp.int32)
    padded = (counts + bm - 1) // bm * bm
    pad_end = jnp.cumsum(padded)
    pad_start = pad_end - padded
    sort_start = jnp.cumsum(counts) - counts
    n_blocks = -(-(n_assign + n_exp * (bm - 1)) // bm)
    blk_row = jnp.arange(n_blocks, dtype=jnp.int32) * bm
    block_exp = jnp.minimum(jnp.searchsorted(pad_end, blk_row, side="right"), n_exp - 1).astype(jnp.int32)
    off = blk_row - pad_start[block_exp]
    q_start = (sort_start[block_exp] + off).astype(jnp.int32)
    cnt = jnp.clip(counts[block_exp] - off, 0, bm).astype(jnp.int32)
    n_used = (pad_end[-1:] // bm).astype(jnp.int32)
    pre = lambda i, nb, be, qs, ct, od: (be[i], 0, 0)
    grid_spec = pltpu.PrefetchScalarGridSpec(
        num_scalar_prefetch=5,
        grid=(n_blocks,),
        in_specs=[pl.BlockSpec(memory_space=pl.ANY),
                  pl.BlockSpec((None, d, 2 * f), pre),
                  pl.BlockSpec((None, 1, 2 * f), pre),
                  pl.BlockSpec((None, f, d), pre),
                  pl.BlockSpec((None, 1, d), pre)],
        out_specs=pl.BlockSpec(memory_space=pl.ANY),
        scratch_shapes=[pltpu.VMEM((2, bm, d), F32), pltpu.VMEM((2, bm, d), F32),
                        pltpu.SemaphoreType.DMA((2,)), pltpu.SemaphoreType.DMA((2,))],
    )
    return pl.pallas_call(
        functools.partial(_moe_kernel, bm=bm, n_tok=t, n_assign=n_assign, f=f),
        grid_spec=grid_spec,
        out_shape=jax.ShapeDtypeStruct((n_assign, d), F32),
        compiler_params=_params(("arbitrary",)),
        name="moe_experts",
    )(n_used, block_exp, q_start, cnt, order, h, w1, b1, w2, b2)


def _moe_weights(w1, b1, w2, b2):
    n_exp, d, f2 = w1.shape
    perm = np.concatenate([np.arange(0, f2, 2), np.arange(1, f2, 2)])
    pmat = jnp.asarray(np.arange(f2)[:, None] == perm[None, :], BF16)
    w1p = _matmul(w1.reshape(n_exp * d, f2), pmat, out_dtype=BF16).reshape(n_exp, d, f2)
    return w1p, b1[:, perm].reshape(n_exp, 1, f2), w2.astype(BF16), b2.reshape(n_exp, 1, -1)


def _row_shift(u, prev8, next8, k):
    n = u.shape[0]
    rolled = pltpu.roll(u, k % n, 0)
    sub = lax.broadcasted_iota(jnp.int32, (SUBLANES, u.shape[1]), 0)
    if k > 0:
        halo = pltpu.roll(prev8, k, 0)
        top = jnp.where(sub < k, halo, rolled[0:SUBLANES])
        return jnp.concatenate([top, rolled[SUBLANES:]], axis=0)
    halo = pltpu.roll(next8, SUBLANES + k, 0)
    bot = jnp.where(sub >= SUBLANES + k, halo, rolled[n - SUBLANES:])
    return jnp.concatenate([rolled[:n - SUBLANES], bot], axis=0)


def _halo_tiles(prev_ref, next_ref, per):
    j = pl.program_id(0) % per
    prev8 = jnp.where(j == 0, 0.0, prev_ref[...])
    next8 = jnp.where(j == per - 1, 0.0, next_ref[...])
    return prev8, next8


def _shift_kernel(u_ref, prev_ref, next_ref, mu_ref, o_ref, *, per):
    u = u_ref[...]
    prev8, next8 = _halo_tiles(prev_ref, next_ref, per)
    nb = 0.5 * (_row_shift(u, prev8, next8, 1) + _row_shift(u, prev8, next8, -1))
    o_ref[...] = u + mu_ref[...] * (nb - u)


def _conv_kernel(u_ref, prev_ref, next_ref, w_ref, b_ref, o_ref, *, per):
    u = u_ref[...]
    prev8, next8 = _halo_tiles(prev_ref, next_ref, per)
    acc = (w_ref[0:1, :] * _row_shift(u, prev8, next8, 2) + w_ref[1:2, :] * _row_shift(u, prev8, next8, 1)
           + w_ref[2:3, :] * u + w_ref[3:4, :] * _row_shift(u, prev8, next8, -1)) + b_ref[...]
    o_ref[...] = acc * jax.nn.sigmoid(acc)


def _row_neighbour_call(kernel_fn, u, seg, extra, name):
    t, n = u.shape
    tr = _pick(seg, (256, 128, 64, 32, 16, 8))
    tc = _pick(n, (2048, 1024, 768, 512, 384, 256, 128))
    per = seg // tr
    r8 = tr // SUBLANES
    last8 = t // SUBLANES - 1
    in_specs = [pl.BlockSpec((tr, tc), lambda i, j: (i, j)),
                pl.BlockSpec((SUBLANES, tc), lambda i, j: (jnp.maximum(i * r8 - 1, 0), j)),
                pl.BlockSpec((SUBLANES, tc), lambda i, j: (jnp.minimum((i + 1) * r8, last8), j))]
    for e in extra:
        in_specs.append(pl.BlockSpec((e.shape[0], tc), lambda i, j: (0, j)))
    return pl.pallas_call(
        functools.partial(kernel_fn, per=per),
        grid=(t // tr, n // tc),
        in_specs=in_specs,
        out_specs=pl.BlockSpec((tr, tc), lambda i, j: (i, j)),
        out_shape=jax.ShapeDtypeStruct((t, n), F32),
        compiler_params=_params(("parallel", "parallel")),
        name=name,
    )(u, u, u, *extra)


def _head_sum_matrix(n):
    idx = np.arange(n) // HEAD_DIM
    return jnp.asarray(idx[:, None] == idx[None, :], BF16)


def _rwkv_prep_kernel(r_ref, k_ref, v_ref, lo_ref, w0_ref, wup_ref, a0_ref, aup_ref, gup_ref, kk_ref_, ka_ref,
                      rk_ref, hs_ref, r_o, v_o, kk_o, lw_o, kd_o, b_o, g_o, bv_o, *, wd_w, ad_w):
    r = r_ref[...]
    k = k_ref[...]
    v = v_ref[...]
    lo = lo_ref[...]
    wd = jnp.tanh(lo[:, 0:wd_w]).astype(BF16)
    ad = lo[:, wd_w:wd_w + ad_w].astype(BF16)
    gs = jax.nn.sigmoid(lo[:, wd_w + ad_w:]).astype(BF16)
    hs = hs_ref[...]
    kkr = k * kk_ref_[...]
    ss = _dot_exact_rhs(kkr * kkr, hs)
    kk = kkr * lax.rsqrt(jnp.maximum(ss, 1e-24))
    ksum = jnp.zeros_like(k)
    for d in range(2):
        z = -(w0_ref[d:d + 1, :] + _dot(wd, wup_ref[d]))
        w_log = -(jnp.maximum(z, 0.0) + jnp.log(1.0 + jnp.exp(-jnp.abs(z)))) - 0.5
        lw_o[d] = -jnp.exp(w_log)
        asig = jax.nn.sigmoid(a0_ref[d:d + 1, :] + _dot(ad, aup_ref[d]))
        kd = k * (1.0 + (asig - 1.0) * ka_ref[...])
        kd_o[d] = kd
        b_o[d] = kk * asig
        ksum = ksum + kd
    bonus = _dot_exact_rhs(r * ksum * rk_ref[...], hs)
    r_o[...] = r
    v_o[...] = v
    kk_o[...] = kk
    g_o[...] = _dot(gs, gup_ref[...])
    bv_o[...] = bonus * v


def _rwkv_prep(us_rkv, us_lora, p, seg):
    t = us_rkv.shape[0]
    a = us_rkv.shape[1] // 3
    nl = us_lora.shape[1]
    tr = _pick(seg, (256, 128, 64, 32, 16, 8))
    tc = _pick(a, (512, 256, 128))
    nj = a // tc
    col = lambda off: (lambda i, j: (i, off * nj + j))
    par = lambda rows: pl.BlockSpec((rows, tc), lambda i, j: (0, j))
    par3 = lambda rows: pl.BlockSpec((2, rows, tc), lambda i, j: (0, 0, j))
    one = jax.ShapeDtypeStruct((t, a), F32)
    two = jax.ShapeDtypeStruct((2, t, a), F32)
    o1 = pl.BlockSpec((tr, tc), lambda i, j: (i, j))
    o2 = pl.BlockSpec((2, tr, tc), lambda i, j: (0, i, j))
    return pl.pallas_call(
        functools.partial(_rwkv_prep_kernel, wd_w=p["wd_w"], ad_w=p["ad_w"]),
        grid=(t // tr, nj),
        in_specs=[pl.BlockSpec((tr, tc), col(0)), pl.BlockSpec((tr, tc), col(1)), pl.BlockSpec((tr, tc), col(2)),
                  pl.BlockSpec((tr, nl), lambda i, j: (i, 0)),
                  par(2), par3(p["wd_w"]), par(2), par3(p["ad_w"]), par(p["g_up"].shape[0]),
                  par(1), par(1), par(1),
                  pl.BlockSpec((tc, tc), lambda i, j: (0, 0))],
        out_specs=[o1, o1, o1, o2, o2, o2, o1, o1],
        out_shape=[one, one, one, two, two, two, one, one],
        compiler_params=_params(("parallel", "parallel")),
        name="rwkv_prep",
    )(us_rkv, us_rkv, us_rkv, us_lora, p["w0"], p["w_up"], p["a0"], p["a_up"], p["g_up"], p["k_k"], p["k_a"],
      p["r_k"], _head_sum_matrix(tc))


def _rwkv_chunk_pairs(ins, states, d, masks):
    incl, strict, tri, eye, first, first2, blockdiag = masks
    cl = RWKV_CHUNK
    pairs = range(len(ins))
    heads = [(p, hh) for p in pairs for hh in range(2)]
    splits = [_split3(ins[p][3]) for p in pairs]
    cum = [_dot(tri, h1) + _dot(tri, h2) + _dot(tri, h3) for h1, h2, h3 in splits]
    cum_last, ar, ar_b, bk, bk_end, sb, vb = [], [], [], [], [], [], []
    for p in pairs:
        r, v, kk, lw, kd, b = ins[p]
        cl_p = jnp.where(d == 0, cum[p][cl - 1:cl, :], cum[p][0:1, :])
        ginv = jnp.exp(-cum[p])
        to_end = jnp.exp(cl_p - cum[p])
        at = -kk * jnp.exp(cum[p] - lw)
        rt = r * jnp.exp(cum[p])
        cum_last.append(cl_p)
        bk.append(jnp.concatenate([b * ginv, kd * ginv], axis=0).astype(BF16))
        bk_end.append(jnp.concatenate([b * to_end, kd * to_end], axis=0).astype(BF16))
        ar.append(jnp.concatenate([at, rt], axis=0))
        ar_b.append(ar[p].astype(BF16))
        sb.append(states[p].astype(BF16))
        vb.append(v.astype(BF16))
    ar_h = {(p, hh): jnp.where(first2 if hh == 0 else jnp.logical_not(first2), ar[p], 0.0).astype(BF16)
            for p, hh in heads}
    m = {k: _dot_nt(ar_h[k], bk[k[0]]) for k in heads}
    xs = [_dot_nt(ar_b[p], sb[p]) for p in pairs]
    a_ab = {k: jnp.where(strict, m[k][:cl, :cl], 0.0) for k in heads}
    a_ak = {k: jnp.where(strict, m[k][:cl, cl:], 0.0).astype(BF16) for k in heads}
    rbk = {k: jnp.concatenate([jnp.where(incl, m[k][cl:, :cl], 0.0), jnp.where(incl, m[k][cl:, cl:], 0.0)],
                              axis=1).astype(BF16) for k in heads}
    rhs = {k: (xs[k[0]][:cl] + _dot(a_ak[k], vb[k[0]])).astype(BF16) for k in heads}
    tinv = {k: eye + a_ab[k] for k in heads}
    pw = {k: a_ab[k].astype(BF16) for k in heads}
    for _ in range(int(math.log2(cl)) - 1):
        pw = {k: _dot(pw[k], pw[k]).astype(BF16) for k in heads}
        tinv = {k: tinv[k] + _dot(tinv[k].astype(BF16), pw[k]) for k in heads}
    u_h = {k: _dot(tinv[k].astype(BF16), rhs[k]) for k in heads}
    uv = [jnp.concatenate([jnp.where(first, u_h[(p, 0)], u_h[(p, 1)]), ins[p][1]], axis=0).astype(BF16)
          for p in pairs]
    y_h = {k: _dot(rbk[k], uv[k[0]]) for k in heads}
    upd = [_dot_tn(uv[p], bk_end[p]) for p in pairs]
    out = []
    for p in pairs:
        y = xs[p][cl:] + jnp.where(first, y_h[(p, 0)], y_h[(p, 1)])
        s_new = states[p] * jnp.exp(cum_last[p]) + jnp.where(blockdiag, upd[p], 0.0)
        out.append((y, s_new))
    return out


def _rwkv_scan_kernel(r_ref, v_ref, kk_ref, lw_ref, kd_ref, b_ref, s0_ref, y_ref, sf_ref, s_scr, *, n_chunks,
                      n_pairs):
    d = pl.program_id(1)
    c = pl.program_id(3)
    cl = RWKV_CHUNK

    @pl.when(c == 0)
    def _():
        s_scr[...] = s0_ref[...]

    sgn = 1 - 2 * d
    row = lax.broadcasted_iota(jnp.int32, (cl, cl), 0)
    col = lax.broadcasted_iota(jnp.int32, (cl, cl), 1)
    order = (row - col) * sgn
    incl = order >= 0
    strict = order > 0
    tri = jnp.where(incl, 1.0, 0.0).astype(BF16)
    eye = jnp.where(row == col, 1.0, 0.0)
    first = lax.broadcasted_iota(jnp.int32, (cl, LANES), 1) < HEAD_DIM
    first2 = lax.broadcasted_iota(jnp.int32, (2 * cl, LANES), 1) < HEAD_DIM
    rr = lax.broadcasted_iota(jnp.int32, (LANES, LANES), 0) // HEAD_DIM
    cc = lax.broadcasted_iota(jnp.int32, (LANES, LANES), 1) // HEAD_DIM
    masks = (incl, strict, tri, eye, first, first2, rr == cc)
    ins = []
    for p in range(n_pairs):
        sl = slice(p * LANES, (p + 1) * LANES)
        ins.append((r_ref[:, sl], v_ref[:, sl], kk_ref[:, sl], lw_ref[:, sl], kd_ref[:, sl], b_ref[:, sl]))
    results = _rwkv_chunk_pairs(ins, [s_scr[p] for p in range(n_pairs)], d, masks)
    for p, (y, s_new) in enumerate(results):
        y_ref[:, p * LANES:(p + 1) * LANES] = y
        s_scr[p] = s_new

    @pl.when(c == n_chunks - 1)
    def _():
        sf_ref[...] = s_scr[...]


def _rwkv_scan(r, v, kk, lw, kd, b, s0, n_seg):
    t, a = r.shape
    seg = t // n_seg
    cl = RWKV_CHUNK
    nc = seg // cl
    npair = a // LANES
    pb = _pick(npair, (RWKV_PAIRS_PER_STEP, 4, 2, 1))
    w = pb * LANES

    def rows(bi, d, p, c):
        return bi * nc + jnp.where(d == 0, c, nc - 1 - c)

    shared = pl.BlockSpec((cl, w), lambda bi, d, p, c: (rows(bi, d, p, c), p))
    per_dir = pl.BlockSpec((None, cl, w), lambda bi, d, p, c: (d, rows(bi, d, p, c), p))
    state = pl.BlockSpec((None, None, pb, LANES, LANES), lambda bi, d, p, c: (bi, d, p, 0, 0))
    return pl.pallas_call(
        functools.partial(_rwkv_scan_kernel, n_chunks=nc, n_pairs=pb),
        grid=(n_seg, 2, npair // pb, nc),
        in_specs=[shared, shared, shared, per_dir, per_dir, per_dir, state],
        out_specs=[per_dir, state],
        out_shape=[jax.ShapeDtypeStruct((2, t, a), F32), jax.ShapeDtypeStruct(s0.shape, F32)],
        scratch_shapes=[pltpu.VMEM((pb, LANES, LANES), F32)],
        compiler_params=_params(("parallel", "parallel", "parallel", "arbitrary")),
        name="rwkv_scan",
    )(r, v, kk, lw, kd, b, s0)


def _rwkv_out_kernel(y_ref, g_ref, bv_ref, lnw_ref, lnb_ref, hs_ref, o_ref):
    y = y_ref[0] + y_ref[1]
    hs = hs_ref[...]
    inv = 1.0 / HEAD_DIM
    mu = _dot_exact_rhs(y, hs) * inv
    yc = y - mu
    var = _dot_exact_rhs(yc * yc, hs) * inv
    yn = yc * lax.rsqrt(var + GN_EPS) * lnw_ref[...] + lnb_ref[...]
    o_ref[...] = ((yn + bv_ref[...]) * g_ref[...]).astype(o_ref.dtype)


def _rwkv_out(y, g, bv, ln_w, ln_b):
    _, t, a = y.shape
    tr = _pick(t, (256, 128, 64, 32, 16, 8))
    tc = _pick(a, (512, 256, 128))
    o1 = pl.BlockSpec((tr, tc), lambda i, j: (i, j))
    par = pl.BlockSpec((1, tc), lambda i, j: (0, j))
    return pl.pallas_call(
        _rwkv_out_kernel,
        grid=(t // tr, a // tc),
        in_specs=[pl.BlockSpec((2, tr, tc), lambda i, j: (0, i, j)), o1, o1, par, par,
                  pl.BlockSpec((tc, tc), lambda i, j: (0, 0))],
        out_specs=o1,
        out_shape=jax.ShapeDtypeStruct((t, a), BF16),
        compiler_params=_params(("parallel", "parallel")),
        name="rwkv_out",
    )(y, g, bv, ln_w.reshape(1, a), ln_b.reshape(1, a), _head_sum_matrix(tc))


def _rwkv_params(shift_mu, w0, w_up, a0, a_up, g_up, k_k, k_a, r_k, a_width):
    dr, ar_, gr = w_up.shape[1], a_up.shape[1], g_up.shape[0]
    pad = lambda n: -(-n // LANES) * LANES
    wd_w, ad_w, gd_w = pad(2 * dr), pad(2 * ar_), pad(gr)

    def up(wu, rank, width):
        out = jnp.zeros((2, width, a_width), F32)
        for d in range(2):
            out = out.at[d, d * rank:(d + 1) * rank].set(wu[d])
        return out.astype(BF16)

    return dict(wd_w=wd_w, ad_w=ad_w, gd_w=gd_w, dr=dr, ar=ar_, gr=gr,
                w0=w0, a0=a0, w_up=up(w_up, dr, wd_w), a_up=up(a_up, ar_, ad_w),
                g_up=jnp.pad(g_up, ((0, gd_w - gr), (0, 0))).astype(BF16),
                k_k=k_k.reshape(1, -1), k_a=k_a.reshape(1, -1), r_k=r_k.reshape(1, -1))


def _pad_lora_cols(w, dr2, ar2, gr, p):
    parts = [(w[..., :dr2], p["wd_w"]), (w[..., dr2:dr2 + ar2], p["ad_w"]), (w[..., dr2 + ar2:], p["gd_w"])]
    return jnp.concatenate([jnp.pad(x, [(0, 0)] * (x.ndim - 1) + [(0, wd - x.shape[-1])]) for x, wd in parts],
                           axis=-1)


def _na_bias(rpb, rows):
    kr = min(WIN_R, rows)
    col = np.arange(GRID_W)
    col_start = np.clip(col - WIN_C // 2, 0, GRID_W - WIN_C)
    kc = np.arange(GRID_W)
    inside = (kc[None, :] >= col_start[:, None]) & (kc[None, :] < col_start[:, None] + WIN_C)
    col_off = np.clip(kc[None, :] - col[:, None] + (WIN_C - 1), 0, 2 * WIN_C - 2)
    pats = np.arange(kr)
    row_off = np.clip(np.arange(kr)[None, :] - pats[:, None] + (WIN_R - 1), 0, 2 * WIN_R - 2)
    bias = rpb[:, row_off[:, None, :, None], col_off[None, :, None, :]]
    bias = jnp.where(inside[None, None, :, None, :], bias, NEG_BIG)
    return bias.reshape(rpb.shape[0], kr, GRID_W, kr * GRID_W).astype(F32)


def _softmax_pv(s_list, v_list):
    m = s_list[0].max(axis=-1, keepdims=True)
    for s in s_list[1:]:
        m = jnp.maximum(m, s.max(axis=-1, keepdims=True))
    den = 0.0
    acc = 0.0
    for s, vv in zip(s_list, v_list):
        p = jnp.exp(s - m)
        den = den + p.sum(axis=-1, keepdims=True)
        acc = acc + _dot(p.astype(BF16), vv)
    return acc / den


def _na_kernel(q_ref, k_ref, v_ref, kc_ref, vc_ref, bias_ref, o_ref, *, rows, kr):
    w = GRID_W
    lane = lax.broadcasted_iota(jnp.int32, (w, LANES), 1)
    first = lane < HEAD_DIM
    kc = kc_ref[...]
    vc = vc_ref[...]
    scale = HEAD_DIM ** -0.5

    rb = NA_ROWS_PER_STEP if rows % NA_ROWS_PER_STEP == 0 else 1

    def body(it, carry):
        rr = [it * rb + k for k in range(rb)]
        r0 = [jnp.clip(r - kr // 2, 0, rows - kr) for r in rr]
        q = [(q_ref[pl.ds(pl.multiple_of(r * w, w), w), :].astype(F32) * scale).astype(BF16) for r in rr]
        kw = [k_ref[pl.ds(pl.multiple_of(r * w, w), kr * w), :] for r in r0]
        vw = [v_ref[pl.ds(pl.multiple_of(r * w, w), kr * w), :] for r in r0]
        chains = [(k, hh) for k in range(rb) for hh in range(2)]
        qm = {(k, hh): jnp.where(first if hh == 0 else jnp.logical_not(first), q[k], jnp.zeros_like(q[k]))
              for k, hh in chains}
        s_loc = {c: _dot_nt(qm[c], kw[c[0]]) + bias_ref[c[1], pl.ds(rr[c[0]] - r0[c[0]], 1)][0] for c in chains}
        s_ctx = {c: _dot_nt(qm[c], kc) for c in chains}
        mx = {c: jnp.maximum(s_loc[c].max(axis=-1, keepdims=True), s_ctx[c].max(axis=-1, keepdims=True))
              for c in chains}
        p_loc = {c: jnp.exp(s_loc[c] - mx[c]) for c in chains}
        p_ctx = {c: jnp.exp(s_ctx[c] - mx[c]) for c in chains}
        den = {c: p_loc[c].sum(axis=-1, keepdims=True) + p_ctx[c].sum(axis=-1, keepdims=True) for c in chains}
        acc = {c: _dot(p_loc[c].astype(BF16), vw[c[0]]) + _dot(p_ctx[c].astype(BF16), vc) for c in chains}
        for k in range(rb):
            out = jnp.where(first, acc[(k, 0)] / den[(k, 0)], acc[(k, 1)] / den[(k, 1)])
            o_ref[pl.ds(pl.multiple_of(rr[k] * w, w), w), :] = out.astype(o_ref.dtype)
        return carry

    lax.fori_loop(0, rows // rb, body, 0)


def _ctx_attn_kernel(q_ref, k_ref, v_ref, o_ref):
    n = q_ref.shape[0]
    lane = lax.broadcasted_iota(jnp.int32, (n, LANES), 1)
    first = lane < HEAD_DIM
    q = (q_ref[...].astype(F32) * HEAD_DIM ** -0.5).astype(BF16)
    k = k_ref[...]
    v = v_ref[...]
    outs = []
    for hh in range(2):
        mh = first if hh == 0 else jnp.logical_not(first)
        qm = jnp.where(mh, q, jnp.zeros_like(q))
        outs.append(_softmax_pv([_dot_nt(qm, k)], [v]))
    o_ref[...] = jnp.where(first, outs[0], outs[1]).astype(o_ref.dtype)


def _attention(qkv_x, qkv_c, rpb, n_seg, need_ctx):
    tx, w3 = qkv_x.shape
    bw = w3 // 3
    npair = bw // LANES
    lx = tx // n_seg
    lc = qkv_c.shape[0] // n_seg
    rows = lx // GRID_W
    kr = min(WIN_R, rows)
    bias = _na_bias(rpb, rows)
    blk = lambda length, off: pl.BlockSpec((length, LANES), lambda bi, p: (bi, off * npair + p))
    nx = pl.pallas_call(
        functools.partial(_na_kernel, rows=rows, kr=kr),
        grid=(n_seg, npair),
        in_specs=[blk(lx, 0), blk(lx, 1), blk(lx, 2), blk(lc, 1), blk(lc, 2),
                  pl.BlockSpec((2, kr, GRID_W, kr * GRID_W), lambda bi, p: (p, 0, 0, 0))],
        out_specs=pl.BlockSpec((lx, LANES), lambda bi, p: (bi, p)),
        out_shape=jax.ShapeDtypeStruct((tx, bw), BF16),
        compiler_params=_params(("parallel", "parallel")),
        name="neighbourhood_attention",
    )(qkv_x, qkv_x, qkv_x, qkv_c, qkv_c, bias)
    ncx = None
    if need_ctx:
        ncx = pl.pallas_call(
            _ctx_attn_kernel,
            grid=(n_seg, npair),
            in_specs=[blk(lc, 0), blk(lc, 1), blk(lc, 2)],
            out_specs=pl.BlockSpec((lc, LANES), lambda bi, p: (bi, p)),
            out_shape=jax.ShapeDtypeStruct((qkv_c.shape[0], bw), BF16),
            compiler_params=_params(("parallel", "parallel")),
            name="context_attention",
        )(qkv_c, qkv_c, qkv_c)
    return nx, ncx


def _even_mixer(hx, hc, n_seg, w_in, shift_mu, w0, w_up, a0, a_up, g_up, k_k, k_a, r_k, ln_w, ln_b, rpb, w_out,
                need_ctx):
    d = hx.shape[1]
    a_width = k_k.shape[0]
    p = _rwkv_params(shift_mu, w0, w_up, a0, a_up, g_up, k_k, k_a, r_k, a_width)
    dr2, ar2, gr = 2 * p["dr"], 2 * p["ar"], p["gr"]
    a_cols = 3 * a_width + dr2 + ar2 + gr
    w_rkv = w_in[:, :3 * a_width].astype(BF16)
    w_lora = _pad_lora_cols(w_in[:, 3 * a_width:a_cols], dr2, ar2, gr, p).astype(BF16)
    w_qkv = w_in[:, a_cols:].astype(BF16)
    mu_rkv = shift_mu[:3 * a_width].reshape(1, -1)
    mu_lora = _pad_lora_cols(shift_mu[3 * a_width:], dr2, ar2, gr, p).reshape(1, -1)
    w_out_b = w_out.astype(BF16)

    def rwkv_side(h, s0):
        seg = h.shape[0] // n_seg
        us_rkv = _row_neighbour_call(_shift_kernel, _matmul(h, w_rkv), seg, [mu_rkv], "token_shift")
        us_lora = _row_neighbour_call(_shift_kernel, _matmul(h, w_lora), seg, [mu_lora], "token_shift")
        r, v, kk, lw, kd, b, g, bv = _rwkv_prep(us_rkv, us_lora, p, seg)
        y, s_fin = _rwkv_scan(r, v, kk, lw, kd, b, s0, n_seg)
        return (y, g, bv), s_fin

    s_zero = jnp.zeros((n_seg, 2, a_width // LANES, LANES, LANES), F32)
    terms_c, s_ctx = rwkv_side(hc, s_zero)
    terms_x, _ = rwkv_side(hx, s_ctx)
    qkv_x = _matmul(hx, w_qkv, out_dtype=BF16)
    qkv_c = _matmul(hc, w_qkv, out_dtype=BF16)
    nx, ncx = _attention(qkv_x, qkv_c, rpb, n_seg, need_ctx)
    rx = _rwkv_out(*terms_x, ln_w, ln_b)
    ox = _matmul(jnp.concatenate([rx, nx], axis=1), w_out_b)
    oc = None
    if need_ctx:
        rc = _rwkv_out(*terms_c, ln_w, ln_b)
        oc = _matmul(jnp.concatenate([rc, ncx], axis=1), w_out_b)
    return ox, oc


def _softplus(x):
    return jnp.maximum(x, 0.0) + jnp.log(1.0 + jnp.exp(-jnp.abs(x)))


def _ssd_scan_kernel(x_ref, b_ref, c_ref, dt_ref, bias_ref, alog_ref, s0_ref, y_ref, sf_ref, s_scr, *,
                     n_chunks, n_e):
    d = pl.program_id(1)
    g = pl.program_id(2)
    c = pl.program_id(3)
    cl = x_ref.shape[0]
    hp = dt_ref.shape[1]

    @pl.when(c == 0)
    def _():
        s_scr[...] = s0_ref[...]

    sgn = 1 - 2 * d
    row = lax.broadcasted_iota(jnp.int32, (cl, cl), 0)
    col = lax.broadcasted_iota(jnp.int32, (cl, cl), 1)
    incl = (row - col) * sgn >= 0
    tri = jnp.where(incl, 1.0, 0.0).astype(BF16)
    dt_all = _softplus(dt_ref[...] + bias_ref[...])
    dta_all = dt_all * (-jnp.exp(alog_ref[...]))
    hr = lax.broadcasted_iota(jnp.int32, (hp, LANES), 0)
    hc = lax.broadcasted_iota(jnp.int32, (hp, LANES), 1)
    sel = jnp.where(jnp.logical_and(hr == g * n_e + hc, hc < n_e), 1.0, 0.0).astype(BF16)
    dt_g = _dot_exact_rhs(dt_all, sel)
    h1, h2, h3 = _split3(_dot_exact_rhs(dta_all, sel))
    cum = _dot(tri, h1) + _dot(tri, h2) + _dot(tri, h3)
    cum_t = cum.T
    cum_last = jnp.where(d == 0, cum[cl - 1:cl, :], cum[0:1, :])
    e_cum = jnp.exp(cum)
    e_end = jnp.exp(cum_last - cum)
    e_last = jnp.exp(cum_last)
    bm = b_ref[...].astype(BF16)
    cm = c_ref[...].astype(BF16)
    cb = _dot_nt(cm, bm)
    lane = lax.broadcasted_iota(jnp.int32, (cl, LANES), 1)
    first = lane < SSM_HEAD_DIM
    first_row = first[0:1, :]
    state = s_scr[...]
    y_state = _dot(cm, state.astype(BF16))
    xdt_end = []
    decays = []
    for q in range(n_e // 2):
        j0, j1 = 2 * q, 2 * q + 1
        lo, hi = q * LANES, (q + 1) * LANES
        pick = lambda t, rows=first: jnp.where(rows, t[:, j0:j0 + 1], t[:, j1:j1 + 1])
        xdt = x_ref[:, lo:hi] * pick(dt_g)
        xdt_b = xdt.astype(BF16)
        ys = []
        for j in (j0, j1):
            seg = cum[:, j:j + 1] - cum_t[j:j + 1, :]
            m = cb * jnp.exp(jnp.where(incl, seg, NEG_BIG))
            ys.append(_dot(m.astype(BF16), xdt_b))
        y_ref[:, lo:hi] = jnp.where(first, ys[0], ys[1]) + y_state[:, lo:hi] * pick(e_cum)
        xdt_end.append((xdt * pick(e_end)).astype(BF16))
        decays.append(jnp.where(first_row, e_last[:, j0:j0 + 1], e_last[:, j1:j1 + 1]))
    upd = _dot_tn(bm, jnp.concatenate(xdt_end, axis=1))
    s_new = state * jnp.concatenate(decays, axis=1) + upd
    s_scr[...] = s_new

    @pl.when(c == n_chunks - 1)
    def _():
        sf_ref[...] = s_new


def _ssd_scan(xbc, dt_raw, dt_bias, a_log, s0, n_seg, inner):
    t = xbc.shape[0]
    seg = t // n_seg
    cl = min(SSM_CHUNK, seg)
    nc = seg // cl
    ep = inner // SSM_GROUPS
    n_e = ep // SSM_HEAD_DIM
    hp = dt_raw.shape[1] // 2
    nb = inner // SSM_STATE

    def rows(bi, d, g, c):
        return bi * nc + jnp.where(d == 0, c, nc - 1 - c)

    state = pl.BlockSpec((None, None, None, SSM_STATE, ep), lambda bi, d, g, c: (bi, d, g, 0, 0))
    par = pl.BlockSpec((None, 1, hp), lambda bi, d, g, c: (d, 0, 0))
    return pl.pallas_call(
        functools.partial(_ssd_scan_kernel, n_chunks=nc, n_e=n_e),
        grid=(n_seg, 2, SSM_GROUPS, nc),
        in_specs=[pl.BlockSpec((cl, ep), lambda bi, d, g, c: (rows(bi, d, g, c), g)),
                  pl.BlockSpec((cl, SSM_STATE), lambda bi, d, g, c: (rows(bi, d, g, c), nb + g)),
                  pl.BlockSpec((cl, SSM_STATE), lambda bi, d, g, c: (rows(bi, d, g, c), nb + SSM_GROUPS + g)),
                  pl.BlockSpec((cl, hp), lambda bi, d, g, c: (rows(bi, d, g, c), d)),
                  par, par, state],
        out_specs=[pl.BlockSpec((None, cl, ep), lambda bi, d, g, c: (d, rows(bi, d, g, c), g)), state],
        out_shape=[jax.ShapeDtypeStruct((2, t, inner), F32), jax.ShapeDtypeStruct(s0.shape, F32)],
        scratch_shapes=[pltpu.VMEM((SSM_STATE, ep), F32)],
        compiler_params=_params(("parallel", "parallel", "parallel", "arbitrary")),
        name="ssd_scan",
    )(xbc, xbc, xbc, dt_raw, dt_bias, a_log, s0)


def _ssm_out_kernel(y_ref, xs_ref, z_ref, dsk_ref, nw_ref, o_ref):
    z = z_ref[...]
    y = (y_ref[0] + y_ref[1] + dsk_ref[...] * xs_ref[...]) * (z * jax.nn.sigmoid(z))
    o_ref[...] = _rms(y, nw_ref[...]).astype(o_ref.dtype)


def _ssm_out(y, xbc, z, d_skip_cols, norm_w):
    _, t, inner = y.shape
    gw = inner // SSM_GROUPS
    tr = _pick(t, (256, 128, 64, 32, 16, 8))
    blk = pl.BlockSpec((tr, gw), lambda i, g: (i, g))
    par = pl.BlockSpec((1, gw), lambda i, g: (0, g))
    return pl.pallas_call(
        _ssm_out_kernel,
        grid=(t // tr, SSM_GROUPS),
        in_specs=[pl.BlockSpec((2, tr, gw), lambda i, g: (0, i, g)), blk, blk, par, par],
        out_specs=blk,
        out_shape=jax.ShapeDtypeStruct((t, inner), BF16),
        compiler_params=_params(("parallel", "parallel")),
        name="ssm_out",
    )(y, xbc, z, d_skip_cols, norm_w.reshape(1, inner))


def _odd_mixer(hx, hc, n_seg, w_in, conv_w, conv_b, dt_bias, a_log, d_skip, norm_w, w_out, need_ctx):
    inner = norm_w.shape[0]
    n_heads = d_skip.shape[0]
    conv_dim = conv_w.shape[1]
    hp = -(-n_heads // LANES) * LANES
    w_z = w_in[:, :inner].astype(BF16)
    w_xbc = w_in[:, inner:inner + conv_dim].astype(BF16)
    w_dt = jnp.pad(w_in[:, inner + conv_dim:].reshape(-1, 2, n_heads),
                   ((0, 0), (0, 0), (0, hp - n_heads))).reshape(-1, 2 * hp).astype(BF16)
    pad_h = lambda p: jnp.pad(p, ((0, 0), (0, hp - n_heads))).reshape(2, 1, hp)
    bias_p, alog_p = pad_h(dt_bias), pad_h(a_log)
    d_cols = jnp.repeat(d_skip, SSM_HEAD_DIM).reshape(1, inner)
    w_out_b = w_out.astype(BF16)
    conv_b2 = conv_b.reshape(1, conv_dim)

    def side(h, s0, need_out):
        seg = h.shape[0] // n_seg
        z = _matmul(h, w_z)
        xbc = _row_neighbour_call(_conv_kernel, _matmul(h, w_xbc), seg, [conv_w, conv_b2], "conv_silu")
        dt_raw = _matmul(h, w_dt)
        y, s_fin = _ssd_scan(xbc, dt_raw, bias_p, alog_p, s0, n_seg, inner)
        out = _matmul(_ssm_out(y, xbc, z, d_cols, norm_w), w_out_b) if need_out else None
        return out, s_fin

    s_zero = jnp.zeros((n_seg, 2, SSM_GROUPS, SSM_STATE, inner // SSM_GROUPS), F32)
    oc, s_ctx = side(hc, s_zero, need_ctx)
    ox, _ = side(hx, s_ctx, True)
    return ox, oc


def kernel(x, c, ctx, c_ctx, ada_w, ada_b, norm_g, ev_w_in, ev_shift_mu, rk_w0, rk_w_up, rk_a0, rk_a_up, rk_g_up,
           rk_k_k, rk_k_a, rk_r_k, rk_ln_w, rk_ln_b, na_rpb, ev_w_out, od_w_in, od_conv_w, od_conv_b, od_dt_bias,
           od_a_log, od_d, od_norm_w, od_w_out, router_w, router_b, moe_w1, moe_b1, moe_w2, moe_b2):
    n_b, n_seq, d = x.shape
    n_ctx = ctx.shape[1]
    depth = ada_w.shape[0]
    xt = x.reshape(n_b * n_seq, d)
    ct = ctx.reshape(n_b * n_ctx, d)
    n_cond = -(-(n_b + 1) // SUBLANES) * SUBLANES
    cond = jnp.zeros((n_cond, d), F32).at[:n_b].set(c).at[n_b].set(c_ctx)
    mods = _adaln(cond, ada_w, ada_b)
    hx = hc = None
    for layer in range(depth):
        need_ctx = layer < depth - 1
        i = layer // 2
        mod_x = mods[layer, :n_b].reshape(n_b, 6, d)
        mod_c = jnp.broadcast_to(mods[layer, n_b].reshape(1, 6, d), (n_b, 6, d))
        g = norm_g[layer]
        if layer == 0:
            hx = _norm_mod(xt, g[0], mod_x, 0, 1)
            hc = _norm_mod(ct, g[0], mod_c, 0, 1)
        if layer % 2 == 0:
            ox, oc = _even_mixer(hx, hc, n_b, ev_w_in[i], ev_shift_mu[i], rk_w0[i], rk_w_up[i], rk_a0[i],
                                 rk_a_up[i], rk_g_up[i], rk_k_k[i], rk_k_a[i], rk_r_k[i], rk_ln_w[i], rk_ln_b[i],
                                 na_rpb[i], ev_w_out[i], need_ctx)
        else:
            ox, oc = _odd_mixer(hx, hc, n_b, od_w_in[i], od_conv_w[i], od_conv_b[i], od_dt_bias[i], od_a_log[i],
                                od_d[i], od_norm_w[i], od_w_out[i], need_ctx)
        router = (router_w[layer], router_b[layer])
        experts = _moe_weights(moe_w1[layer], moe_b1[layer], moe_w2[layer], moe_b2[layer])
        streams = [(xt, ox, mod_x)] + ([(ct, oc, mod_c)] if need_ctx else [])
        new = []
        for tok, o, mod in streams:
            tok, h2, top_idx, gate = _residual(tok, o, g[1], mod, 2, g_b=g[2], i_shift=3, i_scale=4, router=router)
            slots = _moe(h2, top_idx, *experts)
            if layer + 1 < depth:
                g_next = norm_g[layer + 1]
                mod_next = (mods[layer + 1, :n_b].reshape(n_b, 6, d) if mod is mod_x else
                            jnp.broadcast_to(mods[layer + 1, n_b].reshape(1, 6, d), (n_b, 6, d)))
                tok, h_next = _residual(tok, slots, g[3], mod, 5, n_sum=TOP_K, g_b=g_next[0], i_shift=0, i_scale=1,
                                        h_dtype=BF16, mod_h=mod_next, slot_gate=gate)
            else:
                (tok,) = _residual(tok, slots, g[3], mod, 5, n_sum=TOP_K, slot_gate=gate)
                h_next = None
            new.append((tok, h_next))
        xt, hx = new[0]
        if need_ctx:
            ct, hc = new[1]
    return xt.reshape(n_b, n_seq, d)
```

```python
import functools
import math

import numpy as np
import jax
import jax.numpy as jnp
from jax import lax
from jax.experimental import pallas as pl
from jax.experimental.pallas import tpu as pltpu

F32 = jnp.float32
BF16 = jnp.bfloat16

RMS_EPS = 1e-6
GN_EPS = 64e-5
HEAD_DIM = 64
GRID_W = 64
WIN_R = 8
WIN_C = 16
SSM_HEAD_DIM = 64
SSM_GROUPS = 8
SSM_STATE = 128
SSM_CHUNK = 128
TOP_K = 4
SWIGLU_ALPHA = 1.702
SWIGLU_LIMIT = 7.0
MOE_ROWS = 256
RWKV_CHUNK = 64
MM_TM = (1024, 512, 256, 128, 64, 32, 16, 8)
MM_TN = (1024, 768, 640, 512, 384, 256, 128)
MM_TK = (4096, 2048, 1024, 512, 256, 128)
SSD_GROUPS_PER_STEP = 4
NA_ROWS_PER_STEP = 4
RWKV_PAIRS_PER_STEP = 8
LANES = 128
SUBLANES = 8
VMEM_LIMIT = 56 * 1024 * 1024
NEG_BIG = -1e30


def _pick(n, cands):
    for c in cands:
        if n % c == 0:
            return c
    return n


def _params(sem, vmem=VMEM_LIMIT):
    return pltpu.CompilerParams(dimension_semantics=sem, vmem_limit_bytes=vmem)


def _split3(x):
    hi = x.astype(BF16)
    r1 = x - hi.astype(F32)
    mid = r1.astype(BF16)
    lo = (r1 - mid.astype(F32)).astype(BF16)
    return hi, mid, lo


def _dot(a, b):
    return jnp.dot(a, b, preferred_element_type=F32)


def _dot_nt(a, b):
    return lax.dot_general(a, b, (((1,), (1,)), ((), ())), preferred_element_type=F32)


def _dot_tn(a, b):
    return lax.dot_general(a, b, (((0,), (0,)), ((), ())), preferred_element_type=F32)


def _dot_exact_rhs(x, m_bf16):
    hi, mid, lo = _split3(x)
    return _dot(hi, m_bf16) + _dot(mid, m_bf16) + _dot(lo, m_bf16)


def _mm_kernel(a_ref, w_ref, b_ref, o_ref, acc_ref, *, nk, pre):
    if nk == 1:
        a = a_ref[...]
        if pre == "silu":
            a = a * jax.nn.sigmoid(a)
        o_ref[...] = (_dot(a.astype(BF16), w_ref[...].astype(BF16)) + b_ref[...]).astype(o_ref.dtype)
        return
    k = pl.program_id(2)

    @pl.when(k == 0)
    def _():
        acc_ref[...] = jnp.zeros_like(acc_ref)

    a = a_ref[...]
    if pre == "silu":
        a = a * jax.nn.sigmoid(a)
    acc_ref[...] += _dot(a.astype(BF16), w_ref[...].astype(BF16))

    @pl.when(k == nk - 1)
    def _():
        o_ref[...] = (acc_ref[...] + b_ref[...]).astype(o_ref.dtype)


def _matmul(a, w, bias=None, out_dtype=F32, pre=None):
    m, kdim = a.shape
    n = w.shape[1]
    tm = _pick(m, MM_TM)
    tn = _pick(n, MM_TN)
    tk = _pick(kdim, MM_TK)
    nk = kdim // tk
    if bias is None:
        bias = jnp.zeros((n,), F32)
    return pl.pallas_call(
        functools.partial(_mm_kernel, nk=nk, pre=pre),
        grid=(m // tm, n // tn, nk),
        in_specs=[pl.BlockSpec((tm, tk), lambda i, j, k: (i, k)),
                  pl.BlockSpec((tk, tn), lambda i, j, k: (k, j)),
                  pl.BlockSpec((1, tn), lambda i, j, k: (0, j))],
        out_specs=pl.BlockSpec((tm, tn), lambda i, j, k: (i, j)),
        out_shape=jax.ShapeDtypeStruct((m, n), out_dtype),
        scratch_shapes=[pltpu.VMEM((tm, tn), F32)],
        compiler_params=_params(("parallel", "parallel", "arbitrary")),
        name="matmul",
    )(a, w, bias.reshape(1, n).astype(F32))


def _adaln_kernel(c_ref, w_ref, b_ref, o_ref, acc_ref, *, nk):
    k = pl.program_id(2)

    @pl.when(k == 0)
    def _():
        acc_ref[...] = jnp.zeros_like(acc_ref)

    c = c_ref[...]
    c = c * jax.nn.sigmoid(c)
    acc_ref[...] += _dot(c.astype(BF16), w_ref[...].astype(BF16))

    @pl.when(k == nk - 1)
    def _():
        o_ref[...] = acc_ref[...] + b_ref[...]


def _adaln(cond, ada_w, ada_b):
    depth, d, n = ada_w.shape
    r = cond.shape[0]
    tn = _pick(n, (2048, 1024, 512, 256, 128))
    tk = _pick(d, (1024, 512, 256, 128))
    nk = d // tk
    return pl.pallas_call(
        functools.partial(_adaln_kernel, nk=nk),
        grid=(depth, n // tn, nk),
        in_specs=[pl.BlockSpec((r, tk), lambda l, j, k: (0, k)),
                  pl.BlockSpec((None, tk, tn), lambda l, j, k: (l, k, j)),
                  pl.BlockSpec((None, 1, tn), lambda l, j, k: (l, 0, j))],
        out_specs=pl.BlockSpec((None, r, tn), lambda l, j, k: (l, 0, j)),
        out_shape=jax.ShapeDtypeStruct((depth, r, n), F32),
        scratch_shapes=[pltpu.VMEM((r, tn), F32)],
        compiler_params=_params(("parallel", "parallel", "arbitrary")),
        name="adaln",
    )(cond, ada_w, ada_b.reshape(depth, 1, n))


def _rms(x, g):
    return x * lax.rsqrt(jnp.mean(x * x, axis=-1, keepdims=True) + RMS_EPS) * g


def _norm_mod_kernel(x_ref, g_ref, mod_ref, o_ref, *, i_shift, i_scale):
    y = _rms(x_ref[...], g_ref[...])
    o = y * (1.0 + mod_ref[i_scale:i_scale + 1, :]) + mod_ref[i_shift:i_shift + 1, :]
    o_ref[...] = o.astype(o_ref.dtype)


def _norm_mod(x, g, mod, i_shift, i_scale, out_dtype=BF16):
    t, d = x.shape
    seg = t // mod.shape[0]
    tr = _pick(seg, (256, 128, 64, 32, 16, 8))
    per = seg // tr
    return pl.pallas_call(
        functools.partial(_norm_mod_kernel, i_shift=i_shift, i_scale=i_scale),
        grid=(t // tr,),
        in_specs=[pl.BlockSpec((tr, d), lambda i: (i, 0)),
                  pl.BlockSpec((1, d), lambda i: (0, 0)),
                  pl.BlockSpec((None, 6, d), lambda i: (i // per, 0, 0))],
        out_specs=pl.BlockSpec((tr, d), lambda i: (i, 0)),
        out_shape=jax.ShapeDtypeStruct((t, d), out_dtype),
        compiler_params=_params(("parallel",)),
        name="norm_mod",
    )(x, g.reshape(1, d), mod)


def _residual_kernel(*refs, n_sum, i_gate, i_shift, i_scale, with_h, with_router, n_exp):
    x_ref = refs[0]
    o_refs = refs[1:1 + n_sum]
    ga_ref, mod_ref = refs[1 + n_sum:3 + n_sum]
    pos = 3 + n_sum
    if n_sum > 1:
        sg_ref = refs[pos]
        pos += 1
    if with_h:
        gb_ref, modh_ref = refs[pos:pos + 2]
        pos += 2
    if with_router:
        rw_ref, rb_ref = refs[pos:pos + 2]
        pos += 2
    xo_ref = refs[pos]
    pos += 1
    if n_sum == 1:
        o = o_refs[0][...]
    else:
        sg = sg_ref[...]
        o = o_refs[0][...] * sg[:, 0:1]
        for j in range(1, n_sum):
            o = o + o_refs[j][...] * sg[:, j:j + 1]
    xn =x_ref[...] + mod_ref[i_gate:i_gate + 1, :] * _rms(o, ga_ref[...])
    xo_ref[...] = xn
    if not with_h:
        return
    h_ref = refs[pos]
    pos += 1
    h = _rms(xn, gb_ref[...]) * (1.0 + modh_ref[i_scale:i_scale + 1, :]) + modh_ref[i_shift:i_shift + 1, :]
    h_ref[...] = h.astype(h_ref.dtype)
    if not with_router:
        return
    idx_ref, gate_ref = refs[pos:pos + 2]
    h1, h2, h3 = _split3(h)
    w1 = rw_ref[0]
    w2 = rw_ref[1]
    logits = (_dot(h1, w1) + _dot(h2, w1) + _dot(h1, w2) + _dot(h3, w1) + _dot(h2, w2)) + rb_ref[...]
    lane = lax.broadcasted_iota(jnp.int32, logits.shape, 1)
    logits = jnp.where(lane < n_exp, logits, NEG_BIG)
    vals, idxs = [], []
    for _ in range(TOP_K):
        m = jnp.max(logits, axis=-1, keepdims=True)
        ix = jnp.min(jnp.where(logits == m, lane, 1 << 30), axis=-1, keepdims=True)
        vals.append(m)
        idxs.append(ix)
        logits = jnp.where(lane == ix, NEG_BIG * 2, logits)
    es = [jnp.exp(v - vals[0]) for v in vals]
    den = es[0]
    for e in es[1:]:
        den = den + e
    idx_out = jnp.zeros(lane.shape, jnp.int32)
    gate_out = jnp.zeros(lane.shape, F32)
    for j in range(TOP_K):
        idx_out = jnp.where(lane == j, idxs[j], idx_out)
        gate_out = jnp.where(lane == j, es[j] / den, gate_out)
    idx_ref[...] = idx_out
    gate_ref[...] = gate_out


def _residual(x, o, g_a, mod, i_gate, n_sum=1, g_b=None, i_shift=0, i_scale=0, h_dtype=F32, router=None,
              mod_h=None, slot_gate=None):
    t, d = x.shape
    seg = t // mod.shape[0]
    tr = _pick(seg, (128, 64, 32, 16, 8))
    per = seg // tr
    nt = t // tr
    with_h = g_b is not None
    with_router = router is not None
    row = lambda i: (i, 0)
    fixed = lambda i: (0, 0)
    ins = [x] + [o] * n_sum + [g_a.reshape(1, d), mod]
    in_specs = ([pl.BlockSpec((tr, d), row)]
                + [pl.BlockSpec((tr, d), functools.partial(lambda j, i: (j * nt + i, 0), j)) for j in range(n_sum)]
                + [pl.BlockSpec((1, d), fixed), pl.BlockSpec((None, 6, d), lambda i: (i // per, 0, 0))])
    outs = [jax.ShapeDtypeStruct((t, d), F32)]
    out_specs = [pl.BlockSpec((tr, d), row)]
    n_exp = 0
    if n_sum > 1:
        ins.append(slot_gate)
        in_specs.append(pl.BlockSpec((tr, slot_gate.shape[1]), row))
    if with_h:
        ins += [g_b.reshape(1, d), mod if mod_h is None else mod_h]
        in_specs += [pl.BlockSpec((1, d), fixed), pl.BlockSpec((None, 6, d), lambda i: (i // per, 0, 0))]
        outs.append(jax.ShapeDtypeStruct((t, d), h_dtype))
        out_specs.append(pl.BlockSpec((tr, d), row))
    if with_router:
        rw, rb = router
        n_exp = rw.shape[1]
        rw = jnp.pad(rw, ((0, 0), (0, LANES - n_exp)))
        rw_hi = rw.astype(BF16)
        rw_lo = (rw - rw_hi.astype(F32)).astype(BF16)
        ins += [jnp.stack([rw_hi, rw_lo]), jnp.pad(rb, (0, LANES - n_exp)).reshape(1, LANES)]
        in_specs += [pl.BlockSpec((2, d, LANES), lambda i: (0, 0, 0)), pl.BlockSpec((1, LANES), fixed)]
        outs += [jax.ShapeDtypeStruct((t, LANES), jnp.int32), jax.ShapeDtypeStruct((t, LANES), F32)]
        out_specs += [pl.BlockSpec((tr, LANES), row), pl.BlockSpec((tr, LANES), row)]
    res = pl.pallas_call(
        functools.partial(_residual_kernel, n_sum=n_sum, i_gate=i_gate, i_shift=i_shift, i_scale=i_scale,
                          with_h=with_h, with_router=with_router, n_exp=n_exp),
        grid=(t // tr,),
        in_specs=in_specs,
        out_specs=out_specs,
        out_shape=outs,
        compiler_params=_params(("parallel",)),
        name="residual",
    )(*ins)
    return res


def _moe_kernel(nb_ref, be_ref, qs_ref, cnt_ref, order_ref, h_hbm, w1_ref, b1_ref, w2_ref, b2_ref, out_hbm,
                xbuf, obuf, sem_in, sem_out, *, bm, n_tok, f):
    del be_ref
    i = pl.program_id(0)
    nb = nb_ref[0]
    cur = i % 2
    shift = TOP_K.bit_length() - 1
    unroll = 8

    def gather_row(q0, buf, r):
        tok = lax.shift_right_logical(order_ref[q0 + r], shift)
        return pltpu.make_async_copy(h_hbm.at[pl.ds(tok, 1)], xbuf.at[buf, pl.ds(r, 1)], sem_in.at[buf])

    def scatter_row(q0, buf, r):
        a = order_ref[q0 + r]
        slot = (a & (TOP_K - 1)) * n_tok + lax.shift_right_logical(a, shift)
        return pltpu.make_async_copy(obuf.at[buf, pl.ds(r, 1)], out_hbm.at[pl.ds(slot, 1)], sem_out.at[buf])

    def for_rows(fn, n_rows):
        n_full = lax.shift_right_logical(n_rows, unroll.bit_length() - 1)

        def group(gi, c):
            for k in range(unroll):
                fn(gi * unroll + k)
            return c

        def single(r, c):
            fn(r)
            return c

        lax.fori_loop(0, n_full, group, 0)
        lax.fori_loop(n_full * unroll, n_rows, single, 0)

    def wait_gathered(buf):
        pltpu.make_async_copy(h_hbm.at[pl.ds(0, bm)], xbuf.at[buf], sem_in.at[buf]).wait()

    def wait_scattered(buf, n_rows):
        row = pltpu.make_async_copy(obuf.at[buf, pl.ds(0, 1)], out_hbm.at[pl.ds(0, 1)], sem_out.at[buf])
        for_rows(lambda r: row.wait(), n_rows)

    @pl.when(i < nb)
    def _():
        @pl.when(i == 0)
        def _():
            for_rows(lambda r: gather_row(qs_ref[0], 0, r).start(), bm)

        wait_gathered(cur)
        q_next = qs_ref[jnp.minimum(i + 1, nb - 1)]
        for r in range(bm):
            gather_row(q_next, 1 - cur, r).start()
        x = xbuf[cur].astype(BF16)
        u = _dot(x, w1_ref[...]) + b1_ref[...]
        glu = jnp.minimum(u[:, :f], SWIGLU_LIMIT)
        lin = jnp.clip(u[:, f:], -SWIGLU_LIMIT, SWIGLU_LIMIT)
        act = glu * jax.nn.sigmoid(SWIGLU_ALPHA * glu) * (lin + 1.0)
        out = _dot(act.astype(BF16), w2_ref[...]) + b2_ref[...]

        @pl.when(i >= 1)
        def _():
            wait_scattered(1 - cur, cnt_ref[jnp.maximum(i - 1, 0)])

        obuf[cur] = out
        q_cur = qs_ref[i]
        for_rows(lambda r: scatter_row(q_cur, cur, r).start(), cnt_ref[i])

        @pl.when(i == nb - 1)
        def _():
            wait_scattered(cur, cnt_ref[i])
            wait_gathered(1 - cur)


def _moe(h, top_idx, w1, b1, w2, b2):
    assert TOP_K & (TOP_K - 1) == 0
    t, d = h.shape
    n_exp, f = w2.shape[0], w2.shape[1]
    bm = MOE_ROWS
    n_assign = t * TOP_K
    flat_e = top_idx[:, :TOP_K].reshape(-1)
    order = jnp.argsort(flat_e, stable=True).astype(jnp.int32)
    counts = jnp.sum(flat_e[:, None] == jnp.arange(n_exp, dtype=flat_e.dtype)[None, :], axis=0, dtype=jnp.int32)
    padded = (counts + bm - 1) // bm * bm
    pad_end = jnp.cumsum(padded)
    pad_start = pad_end - padded
    sort_start = jnp.cumsum(counts) - counts
    n_blocks = -(-(n_assign + n_exp * (bm - 1)) // bm)
    blk_row = jnp.arange(n_blocks, dtype=jnp.int32) * bm
    block_exp = jnp.minimum(jnp.searchsorted(pad_end, blk_row, side="right"), n_exp - 1).astype(jnp.int32)
    off = blk_row - pad_start[block_exp]
    q_start = (sort_start[block_exp] + off).astype(jnp.int32)
    cnt = jnp.clip(counts[block_exp] - off, 0, bm).astype(jnp.int32)
    n_used = (pad_end[-1:] // bm).astype(jnp.int32)
    pre = lambda i, nb, be, qs, ct, od: (be[i], 0, 0)
    grid_spec = pltpu.PrefetchScalarGridSpec(
        num_scalar_prefetch=5,
        grid=(n_blocks,),
        in_specs=[pl.BlockSpec(memory_space=pl.ANY),
                  pl.BlockSpec((None, d, 2 * f), pre),
                  pl.BlockSpec((None, 1, 2 * f), pre),
                  pl.BlockSpec((None, f, d), pre),
                  pl.BlockSpec((None, 1, d), pre)],
        out_specs=pl.BlockSpec(memory_space=pl.ANY),
        scratch_shapes=[pltpu.VMEM((2, bm, d), F32), pltpu.VMEM((2, bm, d), F32),
                        pltpu.SemaphoreType.DMA((2,)), pltpu.SemaphoreType.DMA((2,))],
    )
    return pl.pallas_call(
        functools.partial(_moe_kernel, bm=bm, n_tok=t, f=f),
        grid_spec=grid_spec,
        out_shape=jax.ShapeDtypeStruct((n_assign, d), F32),
        compiler_params=_params(("arbitrary",)),
        name="moe_experts",
    )(n_used, block_exp, q_start, cnt, jnp.pad(order, (0, bm)), h, w1, b1, w2, b2)


def _moe_weights(w1, b1, w2, b2):
    n_exp, d, f2 = w1.shape
    perm = np.concatenate([np.arange(0, f2, 2), np.arange(1, f2, 2)])
    pmat = jnp.asarray(np.arange(f2)[:, None] == perm[None, :], BF16)
    w1p = _matmul(w1.reshape(n_exp * d, f2), pmat, out_dtype=BF16).reshape(n_exp, d, f2)
    return w1p, b1[:, perm].reshape(n_exp, 1, f2), w2.astype(BF16), b2.reshape(n_exp, 1, -1)


def _row_shift(u, prev8, next8, k):
    n = u.shape[0]
    rolled = pltpu.roll(u, k % n, 0)
    sub = lax.broadcasted_iota(jnp.int32, (SUBLANES, u.shape[1]), 0)
    if k > 0:
        halo = pltpu.roll(prev8, k, 0)
        top = jnp.where(sub < k, halo, rolled[0:SUBLANES])
        return jnp.concatenate([top, rolled[SUBLANES:]], axis=0)
    halo = pltpu.roll(next8, SUBLANES + k, 0)
    bot = jnp.where(sub >= SUBLANES + k, halo, rolled[n - SUBLANES:])
    return jnp.concatenate([rolled[:n - SUBLANES], bot], axis=0)


def _halo_tiles(prev_ref, next_ref, per):
    j = pl.program_id(0) % per
    prev8 = jnp.where(j == 0, 0.0, prev_ref[...])
    next8 = jnp.where(j == per - 1, 0.0, next_ref[...])
    return prev8, next8


def _shift_kernel(u_ref, prev_ref, next_ref, mu_ref, o_ref, *, per):
    u = u_ref[...]
    prev8, next8 = _halo_tiles(prev_ref, next_ref, per)
    nb = 0.5 * (_row_shift(u, prev8, next8, 1) + _row_shift(u, prev8, next8, -1))
    o_ref[...] = u + mu_ref[...] * (nb - u)


def _conv_kernel(u_ref, prev_ref, next_ref, w_ref, b_ref, o_ref, *, per):
    u = u_ref[...]
    prev8, next8 = _halo_tiles(prev_ref, next_ref, per)
    acc = (w_ref[0:1, :] * _row_shift(u, prev8, next8, 2) + w_ref[1:2, :] * _row_shift(u, prev8, next8, 1)
           + w_ref[2:3, :] * u + w_ref[3:4, :] * _row_shift(u, prev8, next8, -1)) + b_ref[...]
    o_ref[...] = acc * jax.nn.sigmoid(acc)


def _row_neighbour_call(kernel_fn, u, seg, extra, name):
    t, n = u.shape
    tr = _pick(seg, (256, 128, 64, 32, 16, 8))
    tc = _pick(n, (2048, 1024, 768, 512, 384, 256, 128))
    per = seg // tr
    r8 = tr // SUBLANES
    last8 = t // SUBLANES - 1
    in_specs = [pl.BlockSpec((tr, tc), lambda i, j: (i, j)),
                pl.BlockSpec((SUBLANES, tc), lambda i, j: (jnp.maximum(i * r8 - 1, 0), j)),
                pl.BlockSpec((SUBLANES, tc), lambda i, j: (jnp.minimum((i + 1) * r8, last8), j))]
    for e in extra:
        in_specs.append(pl.BlockSpec((e.shape[0], tc), lambda i, j: (0, j)))
    return pl.pallas_call(
        functools.partial(kernel_fn, per=per),
        grid=(t // tr, n // tc),
        in_specs=in_specs,
        out_specs=pl.BlockSpec((tr, tc), lambda i, j: (i, j)),
        out_shape=jax.ShapeDtypeStruct((t, n), F32),
        compiler_params=_params(("parallel", "parallel")),
        name=name,
    )(u, u, u, *extra)


def _head_sum_matrix(n):
    idx = np.arange(n) // HEAD_DIM
    return jnp.asarray(idx[:, None] == idx[None, :], BF16)


def _rwkv_prep_kernel(r_ref, k_ref, v_ref, lo_ref, w0_ref, wup_ref, a0_ref, aup_ref, gup_ref, kk_ref_, ka_ref,
                      rk_ref, hs_ref, r_o, v_o, kk_o, lw_o, kd_o, b_o, g_o, bv_o, *, wd_w, ad_w):
    r = r_ref[...]
    k = k_ref[...]
    v = v_ref[...]
    lo = lo_ref[...]
    wd = jnp.tanh(lo[:, 0:wd_w]).astype(BF16)
    ad = lo[:, wd_w:wd_w + ad_w].astype(BF16)
    gs = jax.nn.sigmoid(lo[:, wd_w + ad_w:]).astype(BF16)
    hs = hs_ref[...]
    kkr = k * kk_ref_[...]
    ss = _dot_exact_rhs(kkr * kkr, hs)
    kk = kkr * lax.rsqrt(jnp.maximum(ss, 1e-24))
    ksum = jnp.zeros_like(k)
    for d in range(2):
        z = -(w0_ref[d:d + 1, :] + _dot(wd, wup_ref[d]))
        w_log = -(jnp.maximum(z, 0.0) + jnp.log(1.0 + jnp.exp(-jnp.abs(z)))) - 0.5
        lw_o[d] = -jnp.exp(w_log)
        asig = jax.nn.sigmoid(a0_ref[d:d + 1, :] + _dot(ad, aup_ref[d]))
        kd = k * (1.0 + (asig - 1.0) * ka_ref[...])
        kd_o[d] = kd
        b_o[d] = kk * asig
        ksum = ksum + kd
    bonus = _dot_exact_rhs(r * ksum * rk_ref[...], hs)
    r_o[...] = r
    v_o[...] = v
    kk_o[...] = kk
    g_o[...] = _dot(gs, gup_ref[...])
    bv_o[...] = bonus * v


def _rwkv_prep(us_rkv, us_lora, p, seg):
    t = us_rkv.shape[0]
    a = us_rkv.shape[1] // 3
    nl = us_lora.shape[1]
    tr = _pick(seg, (256, 128, 64, 32, 16, 8))
    tc = _pick(a, (512, 256, 128))
    nj = a // tc
    col = lambda off: (lambda i, j: (i, off * nj + j))
    par = lambda rows: pl.BlockSpec((rows, tc), lambda i, j: (0, j))
    par3 = lambda rows: pl.BlockSpec((2, rows, tc), lambda i, j: (0, 0, j))
    one = jax.ShapeDtypeStruct((t, a), F32)
    two = jax.ShapeDtypeStruct((2, t, a), F32)
    o1 = pl.BlockSpec((tr, tc), lambda i, j: (i, j))
    o2 = pl.BlockSpec((2, tr, tc), lambda i, j: (0, i, j))
    return pl.pallas_call(
        functools.partial(_rwkv_prep_kernel, wd_w=p["wd_w"], ad_w=p["ad_w"]),
        grid=(t // tr, nj),
        in_specs=[pl.BlockSpec((tr, tc), col(0)), pl.BlockSpec((tr, tc), col(1)), pl.BlockSpec((tr, tc), col(2)),
                  pl.BlockSpec((tr, nl), lambda i, j: (i, 0)),
                  par(2), par3(p["wd_w"]), par(2), par3(p["ad_w"]), par(p["g_up"].shape[0]),
                  par(1), par(1), par(1),
                  pl.BlockSpec((tc, tc), lambda i, j: (0, 0))],
        out_specs=[o1, o1, o1, o2, o2, o2, o1, o1],
        out_shape=[one, one, one, two, two, two, one, one],
        compiler_params=_params(("parallel", "parallel")),
        name="rwkv_prep",
    )(us_rkv, us_rkv, us_rkv, us_lora, p["w0"], p["w_up"], p["a0"], p["a_up"], p["g_up"], p["k_k"], p["k_a"],
      p["r_k"], _head_sum_matrix(tc))


def _rwkv_chunk_pairs(ins, states, d, masks):
    incl, strict, tri, eye, first, first2, blockdiag = masks
    cl = RWKV_CHUNK
    pairs = range(len(ins))
    heads = [(p, hh) for p in pairs for hh in range(2)]
    splits = [_split3(ins[p][3]) for p in pairs]
    cum = [_dot(tri, h1) + _dot(tri, h2) + _dot(tri, h3) for h1, h2, h3 in splits]
    cum_last, ar, ar_b, bk, bk_end, sb, vb = [], [], [], [], [], [], []
    for p in pairs:
        r, v, kk, lw, kd, b = ins[p]
        cl_p = jnp.where(d == 0, cum[p][cl - 1:cl, :], cum[p][0:1, :])
        ginv = jnp.exp(-cum[p])
        to_end = jnp.exp(cl_p - cum[p])
        at = -kk * jnp.exp(cum[p] - lw)
        rt = r * jnp.exp(cum[p])
        cum_last.append(cl_p)
        bk.append(jnp.concatenate([b * ginv, kd * ginv], axis=0).astype(BF16))
        bk_end.append(jnp.concatenate([b * to_end, kd * to_end], axis=0).astype(BF16))
        ar.append(jnp.concatenate([at, rt], axis=0))
        ar_b.append(ar[p].astype(BF16))
        sb.append(states[p].astype(BF16))
        vb.append(v.astype(BF16))
    ar_h = {(p, hh): jnp.where(first2 if hh == 0 else jnp.logical_not(first2), ar[p], 0.0).astype(BF16)
            for p, hh in heads}
    m = {k: _dot_nt(ar_h[k], bk[k[0]]) for k in heads}
    xs = [_dot_nt(ar_b[p], sb[p]) for p in pairs]
    a_ab = {k: jnp.where(strict, m[k][:cl, :cl], 0.0) for k in heads}
    a_ak = {k: jnp.where(strict, m[k][:cl, cl:], 0.0).astype(BF16) for k in heads}
    rbk = {k: jnp.concatenate([jnp.where(incl, m[k][cl:, :cl], 0.0), jnp.where(incl, m[k][cl:, cl:], 0.0)],
                              axis=1).astype(BF16) for k in heads}
    rhs = {k: (xs[k[0]][:cl] + _dot(a_ak[k], vb[k[0]])).astype(BF16) for k in heads}
    tinv = {k: eye + a_ab[k] for k in heads}
    pw = {k: a_ab[k].astype(BF16) for k in heads}
    for _ in range(int(math.log2(cl)) - 1):
        pw = {k: _dot(pw[k], pw[k]).astype(BF16) for k in heads}
        tinv = {k: tinv[k] + _dot(tinv[k].astype(BF16), pw[k]) for k in heads}
    u_h = {k: _dot(tinv[k].astype(BF16), rhs[k]) for k in heads}
    uv = [jnp.concatenate([jnp.where(first, u_h[(p, 0)], u_h[(p, 1)]), ins[p][1]], axis=0).astype(BF16)
          for p in pairs]
    y_h = {k: _dot(rbk[k], uv[k[0]]) for k in heads}
    upd = [_dot_tn(uv[p], bk_end[p]) for p in pairs]
    out = []
    for p in pairs:
        y = xs[p][cl:] + jnp.where(first, y_h[(p, 0)], y_h[(p, 1)])
        s_new = states[p] * jnp.exp(cum_last[p]) + jnp.where(blockdiag, upd[p], 0.0)
        out.append((y, s_new))
    return out


def _rwkv_scan_kernel(r_ref, v_ref, kk_ref, lw_ref, kd_ref, b_ref, s0_ref, y_ref, sf_ref, s_scr, *, n_chunks,
                      n_pairs):
    d = pl.program_id(1)
    c = pl.program_id(3)
    cl = RWKV_CHUNK

    @pl.when(c == 0)
    def _():
        s_scr[...] = s0_ref[...]

    sgn = 1 - 2 * d
    row = lax.broadcasted_iota(jnp.int32, (cl, cl), 0)
    col = lax.broadcasted_iota(jnp.int32, (cl, cl), 1)
    order = (row - col) * sgn
    incl = order >= 0
    strict = order > 0
    tri = jnp.where(incl, 1.0, 0.0).astype(BF16)
    eye = jnp.where(row == col, 1.0, 0.0)
    first = lax.broadcasted_iota(jnp.int32, (cl, LANES), 1) < HEAD_DIM
    first2 = lax.broadcasted_iota(jnp.int32, (2 * cl, LANES), 1) < HEAD_DIM
    rr = lax.broadcasted_iota(jnp.int32, (LANES, LANES), 0) // HEAD_DIM
    cc = lax.broadcasted_iota(jnp.int32, (LANES, LANES), 1) // HEAD_DIM
    masks = (incl, strict, tri, eye, first, first2, rr == cc)
    ins = []
    for p in range(n_pairs):
        sl = slice(p * LANES, (p + 1) * LANES)
        ins.append((r_ref[:, sl], v_ref[:, sl], kk_ref[:, sl], lw_ref[:, sl], kd_ref[:, sl], b_ref[:, sl]))
    results = _rwkv_chunk_pairs(ins, [s_scr[p] for p in range(n_pairs)], d, masks)
    for p, (y, s_new) in enumerate(results):
        y_ref[:, p * LANES:(p + 1) * LANES] = y
        s_scr[p] = s_new

    @pl.when(c == n_chunks - 1)
    def _():
        sf_ref[...] = s_scr[...]


def _rwkv_scan(r, v, kk, lw, kd, b, s0, n_seg):
    t, a = r.shape
    seg = t // n_seg
    cl = RWKV_CHUNK
    nc = seg // cl
    npair = a // LANES
    pb = _pick(npair, (RWKV_PAIRS_PER_STEP, 4, 2, 1))
    w = pb * LANES

    def rows(bi, d, p, c):
        return bi * nc + jnp.where(d == 0, c, nc - 1 - c)

    shared = pl.BlockSpec((cl, w), lambda bi, d, p, c: (rows(bi, d, p, c), p))
    per_dir = pl.BlockSpec((None, cl, w), lambda bi, d, p, c: (d, rows(bi, d, p, c), p))
    state = pl.BlockSpec((None, None, pb, LANES, LANES), lambda bi, d, p, c: (bi, d, p, 0, 0))
    return pl.pallas_call(
        functools.partial(_rwkv_scan_kernel, n_chunks=nc, n_pairs=pb),
        grid=(n_seg, 2, npair // pb, nc),
        in_specs=[shared, shared, shared, per_dir, per_dir, per_dir, state],
        out_specs=[per_dir, state],
        out_shape=[jax.ShapeDtypeStruct((2, t, a), F32), jax.ShapeDtypeStruct(s0.shape, F32)],
        scratch_shapes=[pltpu.VMEM((pb, LANES, LANES), F32)],
        compiler_params=_params(("parallel", "parallel", "parallel", "arbitrary")),
        name="rwkv_scan",
    )(r, v, kk, lw, kd, b, s0)


def _rwkv_out_kernel(y_ref, g_ref, bv_ref, lnw_ref, lnb_ref, hs_ref, o_ref):
    y = y_ref[0] + y_ref[1]
    hs = hs_ref[...]
    inv = 1.0 / HEAD_DIM
    mu = _dot_exact_rhs(y, hs) * inv
    yc = y - mu
    var = _dot_exact_rhs(yc * yc, hs) * inv
    yn = yc * lax.rsqrt(var + GN_EPS) * lnw_ref[...] + lnb_ref[...]
    o_ref[...] = ((yn + bv_ref[...]) * g_ref[...]).astype(o_ref.dtype)


def _rwkv_out(y, g, bv, ln_w, ln_b):
    _, t, a = y.shape
    tr = _pick(t, (256, 128, 64, 32, 16, 8))
    tc = _pick(a, (512, 256, 128))
    o1 = pl.BlockSpec((tr, tc), lambda i, j: (i, j))
    par = pl.BlockSpec((1, tc), lambda i, j: (0, j))
    return pl.pallas_call(
        _rwkv_out_kernel,
        grid=(t // tr, a // tc),
        in_specs=[pl.BlockSpec((2, tr, tc), lambda i, j: (0, i, j)), o1, o1, par, par,
                  pl.BlockSpec((tc, tc), lambda i, j: (0, 0))],
        out_specs=o1,
        out_shape=jax.ShapeDtypeStruct((t, a), BF16),
        compiler_params=_params(("parallel", "parallel")),
        name="rwkv_out",
    )(y, g, bv, ln_w.reshape(1, a), ln_b.reshape(1, a), _head_sum_matrix(tc))


def _rwkv_params(shift_mu, w0, w_up, a0, a_up, g_up, k_k, k_a, r_k, a_width):
    dr, ar_, gr = w_up.shape[1], a_up.shape[1], g_up.shape[0]
    pad = lambda n: -(-n // LANES) * LANES
    wd_w, ad_w, gd_w = pad(2 * dr), pad(2 * ar_), pad(gr)

    def up(wu, rank, width):
        out = jnp.zeros((2, width, a_width), F32)
        for d in range(2):
            out = out.at[d, d * rank:(d + 1) * rank].set(wu[d])
        return out.astype(BF16)

    return dict(wd_w=wd_w, ad_w=ad_w, gd_w=gd_w, dr=dr, ar=ar_, gr=gr,
                w0=w0, a0=a0, w_up=up(w_up, dr, wd_w), a_up=up(a_up, ar_, ad_w),
                g_up=jnp.pad(g_up, ((0, gd_w - gr), (0, 0))).astype(BF16),
                k_k=k_k.reshape(1, -1), k_a=k_a.reshape(1, -1), r_k=r_k.reshape(1, -1))


def _pad_lora_cols(w, dr2, ar2, gr, p):
    parts = [(w[..., :dr2], p["wd_w"]), (w[..., dr2:dr2 + ar2], p["ad_w"]), (w[..., dr2 + ar2:], p["gd_w"])]
    return jnp.concatenate([jnp.pad(x, [(0, 0)] * (x.ndim - 1) + [(0, wd - x.shape[-1])]) for x, wd in parts],
                           axis=-1)


def _na_bias(rpb, rows):
    kr = min(WIN_R, rows)
    col = np.arange(GRID_W)
    col_start = np.clip(col - WIN_C // 2, 0, GRID_W - WIN_C)
    kc = np.arange(GRID_W)
    inside = (kc[None, :] >= col_start[:, None]) & (kc[None, :] < col_start[:, None] + WIN_C)
    col_off = np.clip(kc[None, :] - col[:, None] + (WIN_C - 1), 0, 2 * WIN_C - 2)
    pats = np.arange(kr)
    row_off = np.clip(np.arange(kr)[None, :] - pats[:, None] + (WIN_R - 1), 0, 2 * WIN_R - 2)
    n_off = 2 * WIN_C - 1
    select = (np.arange(n_off)[:, None, None] == col_off[None]).reshape(n_off, GRID_W * GRID_W)
    by_col = jnp.dot(rpb[:, row_off].astype(F32), jnp.asarray(select, F32), precision=lax.Precision.HIGHEST)
    bias = jnp.transpose(by_col.reshape(rpb.shape[0], kr, kr, GRID_W, GRID_W), (0, 1, 3, 2, 4))
    bias = jnp.where(inside[None, None, :, None, :], bias, NEG_BIG)
    return bias.reshape(rpb.shape[0], kr, GRID_W, kr * GRID_W).astype(F32)


def _softmax_pv(s_list, v_list):
    m = s_list[0].max(axis=-1, keepdims=True)
    for s in s_list[1:]:
        m = jnp.maximum(m, s.max(axis=-1, keepdims=True))
    den = 0.0
    acc = 0.0
    for s, vv in zip(s_list, v_list):
        p = jnp.exp(s - m)
        den = den + p.sum(axis=-1, keepdims=True)
        acc = acc + _dot(p.astype(BF16), vv)
    return acc / den


def _na_kernel(q_ref, k_ref, v_ref, kc_ref, vc_ref, bias_ref, o_ref, *, rows, kr):
    w = GRID_W
    lane = lax.broadcasted_iota(jnp.int32, (w, LANES), 1)
    first = lane < HEAD_DIM
    kc = kc_ref[...]
    vc = vc_ref[...]
    scale = HEAD_DIM ** -0.5

    rb = NA_ROWS_PER_STEP if rows % NA_ROWS_PER_STEP == 0 else 1

    def body(it, carry):
        rr = [it * rb + k for k in range(rb)]
        r0 = [jnp.clip(r - kr // 2, 0, rows - kr) for r in rr]
        q = [(q_ref[pl.ds(pl.multiple_of(r * w, w), w), :].astype(F32) * scale).astype(BF16) for r in rr]
        kw = [k_ref[pl.ds(pl.multiple_of(r * w, w), kr * w), :] for r in r0]
        vw = [v_ref[pl.ds(pl.multiple_of(r * w, w), kr * w), :] for r in r0]
        chains = [(k, hh) for k in range(rb) for hh in range(2)]
        qm = {(k, hh): jnp.where(first if hh == 0 else jnp.logical_not(first), q[k], jnp.zeros_like(q[k]))
              for k, hh in chains}
        s_loc = {c: _dot_nt(qm[c], kw[c[0]]) + bias_ref[c[1], pl.ds(rr[c[0]] - r0[c[0]], 1)][0] for c in chains}
        s_ctx = {c: _dot_nt(qm[c], kc) for c in chains}
        mx = {c: jnp.maximum(s_loc[c].max(axis=-1, keepdims=True), s_ctx[c].max(axis=-1, keepdims=True))
              for c in chains}
        p_loc = {c: jnp.exp(s_loc[c] - mx[c]) for c in chains}
        p_ctx = {c: jnp.exp(s_ctx[c] - mx[c]) for c in chains}
        den = {c: p_loc[c].sum(axis=-1, keepdims=True) + p_ctx[c].sum(axis=-1, keepdims=True) for c in chains}
        acc = {c: _dot(p_loc[c].astype(BF16), vw[c[0]]) + _dot(p_ctx[c].astype(BF16), vc) for c in chains}
        for k in range(rb):
            out = jnp.where(first, acc[(k, 0)] / den[(k, 0)], acc[(k, 1)] / den[(k, 1)])
            o_ref[pl.ds(pl.multiple_of(rr[k] * w, w), w), :] = out.astype(o_ref.dtype)
        return carry

    lax.fori_loop(0, rows // rb, body, 0)


def _ctx_attn_kernel(q_ref, k_ref, v_ref, o_ref):
    n = q_ref.shape[0]
    lane = lax.broadcasted_iota(jnp.int32, (n, LANES), 1)
    first = lane < HEAD_DIM
    q = (q_ref[...].astype(F32) * HEAD_DIM ** -0.5).astype(BF16)
    k = k_ref[...]
    v = v_ref[...]
    outs = []
    for hh in range(2):
        mh = first if hh == 0 else jnp.logical_not(first)
        qm = jnp.where(mh, q, jnp.zeros_like(q))
        outs.append(_softmax_pv([_dot_nt(qm, k)], [v]))
    o_ref[...] = jnp.where(first, outs[0], outs[1]).astype(o_ref.dtype)


def _attention(qkv_x, qkv_c, rpb, n_seg, need_ctx):
    tx, w3 = qkv_x.shape
    bw = w3 // 3
    npair = bw // LANES
    lx = tx // n_seg
    lc = qkv_c.shape[0] // n_seg
    rows = lx // GRID_W
    kr = min(WIN_R, rows)
    bias = _na_bias(rpb, rows)
    blk = lambda length, off: pl.BlockSpec((length, LANES), lambda bi, p: (bi, off * npair + p))
    nx = pl.pallas_call(
        functools.partial(_na_kernel, rows=rows, kr=kr),
        grid=(n_seg, npair),
        in_specs=[blk(lx, 0), blk(lx, 1), blk(lx, 2), blk(lc, 1), blk(lc, 2),
                  pl.BlockSpec((2, kr, GRID_W, kr * GRID_W), lambda bi, p: (p, 0, 0, 0))],
        out_specs=pl.BlockSpec((lx, LANES), lambda bi, p: (bi, p)),
        out_shape=jax.ShapeDtypeStruct((tx, bw), BF16),
        compiler_params=_params(("parallel", "parallel")),
        name="neighbourhood_attention",
    )(qkv_x, qkv_x, qkv_x, qkv_c, qkv_c, bias)
    ncx = None
    if need_ctx:
        ncx = pl.pallas_call(
            _ctx_attn_kernel,
            grid=(n_seg, npair),
            in_specs=[blk(lc, 0), blk(lc, 1), blk(lc, 2)],
            out_specs=pl.BlockSpec((lc, LANES), lambda bi, p: (bi, p)),
            out_shape=jax.ShapeDtypeStruct((qkv_c.shape[0], bw), BF16),
            compiler_params=_params(("parallel", "parallel")),
            name="context_attention",
        )(qkv_c, qkv_c, qkv_c)
    return nx, ncx


def _even_mixer(hx, hc, n_seg, w_in, shift_mu, w0, w_up, a0, a_up, g_up, k_k, k_a, r_k, ln_w, ln_b, rpb, w_out,
                need_ctx):
    d = hx.shape[1]
    a_width = k_k.shape[0]
    p = _rwkv_params(shift_mu, w0, w_up, a0, a_up, g_up, k_k, k_a, r_k, a_width)
    dr2, ar2, gr = 2 * p["dr"], 2 * p["ar"], p["gr"]
    a_cols = 3 * a_width + dr2 + ar2 + gr
    w_rkv = w_in[:, :3 * a_width].astype(BF16)
    w_lora = _pad_lora_cols(w_in[:, 3 * a_width:a_cols], dr2, ar2, gr, p).astype(BF16)
    w_qkv = w_in[:, a_cols:].astype(BF16)
    mu_rkv = shift_mu[:3 * a_width].reshape(1, -1)
    mu_lora = _pad_lora_cols(shift_mu[3 * a_width:], dr2, ar2, gr, p).reshape(1, -1)
    w_out_b = w_out.astype(BF16)

    def rwkv_side(h, s0):
        seg = h.shape[0] // n_seg
        us_rkv = _row_neighbour_call(_shift_kernel, _matmul(h, w_rkv), seg, [mu_rkv], "token_shift")
        us_lora = _row_neighbour_call(_shift_kernel, _matmul(h, w_lora), seg, [mu_lora], "token_shift")
        r, v, kk, lw, kd, b, g, bv = _rwkv_prep(us_rkv, us_lora, p, seg)
        y, s_fin = _rwkv_scan(r, v, kk, lw, kd, b, s0, n_seg)
        return (y, g, bv), s_fin

    s_zero = jnp.zeros((n_seg, 2, a_width // LANES, LANES, LANES), F32)
    terms_c, s_ctx = rwkv_side(hc, s_zero)
    terms_x, _ = rwkv_side(hx, s_ctx)
    qkv_x = _matmul(hx, w_qkv, out_dtype=BF16)
    qkv_c = _matmul(hc, w_qkv, out_dtype=BF16)
    nx, ncx = _attention(qkv_x, qkv_c, rpb, n_seg, need_ctx)
    rx = _rwkv_out(*terms_x, ln_w, ln_b)
    ox = _matmul(jnp.concatenate([rx, nx], axis=1), w_out_b)
    oc = None
    if need_ctx:
        rc = _rwkv_out(*terms_c, ln_w, ln_b)
        oc = _matmul(jnp.concatenate([rc, ncx], axis=1), w_out_b)
    return ox, oc


def _softplus(x):
    return jnp.maximum(x, 0.0) + jnp.log(1.0 + jnp.exp(-jnp.abs(x)))


def _ssd_scan_kernel(x_ref, b_ref, c_ref, dt_ref, bias_ref, alog_ref, s0_ref, y_ref, sf_ref, s_scr, *,
                     n_chunks, n_e, n_grp):
    d = pl.program_id(1)
    gb = pl.program_id(2)
    c = pl.program_id(3)
    cl = x_ref.shape[0]
    hp = dt_ref.shape[1]
    ep = n_e * SSM_HEAD_DIM
    grp = range(n_grp)

    @pl.when(c == 0)
    def _():
        s_scr[...] = s0_ref[...]

    sgn = 1 - 2 * d
    row = lax.broadcasted_iota(jnp.int32, (cl, cl), 0)
    col = lax.broadcasted_iota(jnp.int32, (cl, cl), 1)
    incl = (row - col) * sgn >= 0
    tri = jnp.where(incl, 1.0, 0.0).astype(BF16)
    dt_all = _softplus(dt_ref[...] + bias_ref[...])
    dta_all = dt_all * (-jnp.exp(alog_ref[...]))
    dt_parts = _split3(dt_all)
    dta_parts = _split3(dta_all)
    hr = lax.broadcasted_iota(jnp.int32, (hp, LANES), 0)
    hc = lax.broadcasted_iota(jnp.int32, (hp, LANES), 1)
    sel = [jnp.where(jnp.logical_and(hr == (gb * n_grp + gi) * n_e + hc, hc < n_e), 1.0, 0.0).astype(BF16)
           for gi in grp]
    dt_g = [sum(_dot(p, sel[gi]) for p in dt_parts) for gi in grp]
    dta_g = [sum(_dot(p, sel[gi]) for p in dta_parts) for gi in grp]
    dta_split = [_split3(dta_g[gi]) for gi in grp]
    cum = [sum(_dot(tri, p) for p in dta_split[gi]) for gi in grp]
    bm = [b_ref[:, gi * SSM_STATE:(gi + 1) * SSM_STATE].astype(BF16) for gi in grp]
    cm = [c_ref[:, gi * SSM_STATE:(gi + 1) * SSM_STATE].astype(BF16) for gi in grp]
    cb = [_dot_nt(cm[gi], bm[gi]) for gi in grp]
    state = [s_scr[gi] for gi in grp]
    y_state = [_dot(cm[gi], state[gi].astype(BF16)) for gi in grp]
    cum_t = [cum[gi].T for gi in grp]
    dt_t = [dt_g[gi].T for gi in grp]
    cum_last = [jnp.where(d == 0, cum[gi][cl - 1:cl, :], cum[gi][0:1, :]) for gi in grp]
    e_cum = [jnp.exp(cum[gi]) for gi in grp]
    dt_end = [dt_g[gi] * jnp.exp(cum_last[gi] - cum[gi]) for gi in grp]
    e_last = [jnp.exp(cum_last[gi]) for gi in grp]
    first = lax.broadcasted_iota(jnp.int32, (cl, LANES), 1) < SSM_HEAD_DIM
    first_row = first[0:1, :]
    x_end = [[] for _ in grp]
    decays = [[] for _ in grp]
    for q in range(n_e // 2):
        j0, j1 = 2 * q, 2 * q + 1
        pick = lambda t: jnp.where(first, t[:, j0:j0 + 1], t[:, j1:j1 + 1])
        xq = [x_ref[:, gi * ep + q * LANES:gi * ep + (q + 1) * LANES] for gi in grp]
        x_b = [xq[gi].astype(BF16) for gi in grp]
        m = {(gi, j): (cb[gi] * jnp.exp(jnp.where(incl, cum[gi][:, j:j + 1] - cum_t[gi][j:j + 1, :], NEG_BIG))
                       * dt_t[gi][j:j + 1, :]).astype(BF16) for gi in grp for j in (j0, j1)}
        ys = {k: _dot(m[k], x_b[k[0]]) for k in m}
        for gi in grp:
            lo = gi * ep + q * LANES
            y_ref[:, lo:lo + LANES] = (jnp.where(first, ys[(gi, j0)], ys[(gi, j1)])
                                       + y_state[gi][:, q * LANES:(q + 1) * LANES] * pick(e_cum[gi]))
            x_end[gi].append((xq[gi] * pick(dt_end[gi])).astype(BF16))
            decays[gi].append(jnp.where(first_row, e_last[gi][:, j0:j0 + 1], e_last[gi][:, j1:j1 + 1]))
    upd = [_dot_tn(bm[gi], jnp.concatenate(x_end[gi], axis=1)) for gi in grp]
    for gi in grp:
        s_scr[gi] = state[gi] * jnp.concatenate(decays[gi], axis=1) + upd[gi]

    @pl.when(c == n_chunks - 1)
    def _():
        sf_ref[...] = s_scr[...]


def _ssd_scan(xbc, dt_raw, dt_bias, a_log, s0, n_seg, inner):
    t = xbc.shape[0]
    seg = t // n_seg
    cl = min(SSM_CHUNK, seg)
    nc = seg // cl
    ep = inner // SSM_GROUPS
    n_e = ep // SSM_HEAD_DIM
    hp = dt_raw.shape[1] // 2
    ng = SSD_GROUPS_PER_STEP
    assert SSM_GROUPS % ng == 0 and (inner // SSM_STATE) % ng == 0
    bc_w = ng * SSM_STATE
    nb = inner // bc_w

    def rows(bi, d, g, c):
        return bi * nc + jnp.where(d == 0, c, nc - 1 - c)

    state = pl.BlockSpec((None, None, ng, SSM_STATE, ep), lambda bi, d, g, c: (bi, d, g, 0, 0))
    par = pl.BlockSpec((None, 1, hp), lambda bi, d, g, c: (d, 0, 0))
    return pl.pallas_call(
        functools.partial(_ssd_scan_kernel, n_chunks=nc, n_e=n_e, n_grp=ng),
        grid=(n_seg, 2, SSM_GROUPS // ng, nc),
        in_specs=[pl.BlockSpec((cl, ng * ep), lambda bi, d, g, c: (rows(bi, d, g, c), g)),
                  pl.BlockSpec((cl, bc_w), lambda bi, d, g, c: (rows(bi, d, g, c), nb + g)),
                  pl.BlockSpec((cl, bc_w), lambda bi, d, g, c: (rows(bi, d, g, c), nb + SSM_GROUPS // ng + g)),
                  pl.BlockSpec((cl, hp), lambda bi, d, g, c: (rows(bi, d, g, c), d)),
                  par, par, state],
        out_specs=[pl.BlockSpec((None, cl, ng * ep), lambda bi, d, g, c: (d, rows(bi, d, g, c), g)), state],
        out_shape=[jax.ShapeDtypeStruct((2, t, inner), F32), jax.ShapeDtypeStruct(s0.shape, F32)],
        scratch_shapes=[pltpu.VMEM((ng, SSM_STATE, ep), F32)],
        compiler_params=_params(("parallel", "parallel", "parallel", "arbitrary")),
        name="ssd_scan",
    )(xbc, xbc, xbc, dt_raw, dt_bias, a_log, s0)


def _ssm_out_kernel(y_ref, xs_ref, z_ref, dsk_ref, nw_ref, o_ref):
    z = z_ref[...]
    y = (y_ref[0] + y_ref[1] + dsk_ref[...] * xs_ref[...]) * (z * jax.nn.sigmoid(z))
    o_ref[...] = _rms(y, nw_ref[...]).astype(o_ref.dtype)


def _ssm_out(y, xbc, z, d_skip_cols, norm_w):
    _, t, inner = y.shape
    gw = inner // SSM_GROUPS
    tr = _pick(t, (256, 128, 64, 32, 16, 8))
    blk = pl.BlockSpec((tr, gw), lambda i, g: (i, g))
    par = pl.BlockSpec((1, gw), lambda i, g: (0, g))
    return pl.pallas_call(
        _ssm_out_kernel,
        grid=(t // tr, SSM_GROUPS),
        in_specs=[pl.BlockSpec((2, tr, gw), lambda i, g: (0, i, g)), blk, blk, par, par],
        out_specs=blk,
        out_shape=jax.ShapeDtypeStruct((t, inner), BF16),
        compiler_params=_params(("parallel", "parallel")),
        name="ssm_out",
    )(y, xbc, z, d_skip_cols, norm_w.reshape(1, inner))


def _odd_mixer(hx, hc, n_seg, w_in, conv_w, conv_b, dt_bias, a_log, d_skip, norm_w, w_out, need_ctx):
    inner = norm_w.shape[0]
    n_heads = d_skip.shape[0]
    conv_dim = conv_w.shape[1]
    hp = -(-n_heads // LANES) * LANES
    w_z = w_in[:, :inner].astype(BF16)
    w_xbc = w_in[:, inner:inner + conv_dim].astype(BF16)
    w_dt = jnp.pad(w_in[:, inner + conv_dim:].reshape(-1, 2, n_heads),
                   ((0, 0), (0, 0), (0, hp - n_heads))).reshape(-1, 2 * hp).astype(BF16)
    pad_h = lambda p: jnp.pad(p, ((0, 0), (0, hp - n_heads))).reshape(2, 1, hp)
    bias_p, alog_p = pad_h(dt_bias), pad_h(a_log)
    d_cols = jnp.repeat(d_skip, SSM_HEAD_DIM).reshape(1, inner)
    w_out_b = w_out.astype(BF16)
    conv_b2 = conv_b.reshape(1, conv_dim)

    def side(h, s0, need_out):
        seg = h.shape[0] // n_seg
        z = _matmul(h, w_z)
        xbc = _row_neighbour_call(_conv_kernel, _matmul(h, w_xbc), seg, [conv_w, conv_b2], "conv_silu")
        dt_raw = _matmul(h, w_dt)
        y, s_fin = _ssd_scan(xbc, dt_raw, bias_p, alog_p, s0, n_seg, inner)
        out = _matmul(_ssm_out(y, xbc, z, d_cols, norm_w), w_out_b) if need_out else None
        return out, s_fin

    s_zero = jnp.zeros((n_seg, 2, SSM_GROUPS, SSM_STATE, inner // SSM_GROUPS), F32)
    oc, s_ctx = side(hc, s_zero, need_ctx)
    ox, _ = side(hx, s_ctx, True)
    return ox, oc


def kernel(x, c, ctx, c_ctx, ada_w, ada_b, norm_g, ev_w_in, ev_shift_mu, rk_w0, rk_w_up, rk_a0, rk_a_up, rk_g_up,
           rk_k_k, rk_k_a, rk_r_k, rk_ln_w, rk_ln_b, na_rpb, ev_w_out, od_w_in, od_conv_w, od_conv_b, od_dt_bias,
           od_a_log, od_d, od_norm_w, od_w_out, router_w, router_b, moe_w1, moe_b1, moe_w2, moe_b2):
    n_b, n_seq, d = x.shape
    n_ctx = ctx.shape[1]
    depth = ada_w.shape[0]
    xt = x.reshape(n_b * n_seq, d)
    ct = ctx.reshape(n_b * n_ctx, d)
    n_cond = -(-(n_b + 1) // SUBLANES) * SUBLANES
    cond = jnp.zeros((n_cond, d), F32).at[:n_b].set(c).at[n_b].set(c_ctx)
    mods = _adaln(cond, ada_w, ada_b)
    hx = hc = None
    for layer in range(depth):
        need_ctx = layer < depth - 1
        i = layer // 2
        mod_x = mods[layer, :n_b].reshape(n_b, 6, d)
        mod_c = jnp.broadcast_to(mods[layer, n_b].reshape(1, 6, d), (n_b, 6, d))
        g = norm_g[layer]
        if layer == 0:
            hx = _norm_mod(xt, g[0], mod_x, 0, 1)
            hc = _norm_mod(ct, g[0], mod_c, 0, 1)
        if layer % 2 == 0:
            ox, oc = _even_mixer(hx, hc, n_b, ev_w_in[i], ev_shift_mu[i], rk_w0[i], rk_w_up[i], rk_a0[i],
                                 rk_a_up[i], rk_g_up[i], rk_k_k[i], rk_k_a[i], rk_r_k[i], rk_ln_w[i], rk_ln_b[i],
                                 na_rpb[i], ev_w_out[i], need_ctx)
        else:
            ox, oc = _odd_mixer(hx, hc, n_b, od_w_in[i], od_conv_w[i], od_conv_b[i], od_dt_bias[i], od_a_log[i],
                                od_d[i], od_norm_w[i], od_w_out[i], need_ctx)
        router = (router_w[layer], router_b[layer])
        experts = _moe_weights(moe_w1[layer], moe_b1[layer], moe_w2[layer], moe_b2[layer])
        streams = [(xt, ox, mod_x)] + ([(ct, oc, mod_c)] if need_ctx else [])
        new = []
        for tok, o, mod in streams:
            tok, h2, top_idx, gate = _residual(tok, o, g[1], mod, 2, g_b=g[2], i_shift=3, i_scale=4, router=router)
            slots = _moe(h2, top_idx, *experts)
            if layer + 1 < depth:
                g_next = norm_g[layer + 1]
                mod_next = (mods[layer + 1, :n_b].reshape(n_b, 6, d) if mod is mod_x else
                            jnp.broadcast_to(mods[layer + 1, n_b].reshape(1, 6, d), (n_b, 6, d)))
                tok, h_next = _residual(tok, slots, g[3], mod, 5, n_sum=TOP_K, g_b=g_next[0], i_shift=0, i_scale=1,
                                        h_dtype=BF16, mod_h=mod_next, slot_gate=gate)
            else:
                (tok,) = _residual(tok, slots, g[3], mod, 5, n_sum=TOP_K, slot_gate=gate)
                h_next = None
            new.append((tok, h_next))
        xt, hx = new[0]
        if need_ctx:
            ct, hc = new[1]
    return xt.reshape(n_b, n_seq, d)
```

```python
import functools
import math

import numpy as np
import jax
import jax.numpy as jnp
from jax import lax
from jax.experimental import pallas as pl
from jax.experimental.pallas import tpu as pltpu

F32 = jnp.float32
BF16 = jnp.bfloat16

RMS_EPS = 1e-6
GN_EPS = 64e-5
HEAD_DIM = 64
GRID_W = 64
WIN_R = 8
WIN_C = 16
SSM_HEAD_DIM = 64
SSM_GROUPS = 8
SSM_STATE = 128
SSM_CHUNK = 128
TOP_K = 4
SWIGLU_ALPHA = 1.702
SWIGLU_LIMIT = 7.0
MOE_ROWS = 256
RWKV_CHUNK = 64
MM_TM = (1024, 512, 256, 128, 64, 32, 16, 8)
MM_TN = (1024, 768, 640, 512, 384, 256, 128)
MM_TK = (4096, 2048, 1024, 512, 256, 128)
SSD_GROUPS_PER_STEP = 4
NA_ROWS_PER_STEP = 8
RWKV_PAIRS_PER_STEP = 16
LANES = 128
SUBLANES = 8
VMEM_LIMIT = 56 * 1024 * 1024
NEG_BIG = -1e30


def _pick(n, cands):
    for c in cands:
        if n % c == 0:
            return c
    return n


def _params(sem, vmem=VMEM_LIMIT):
    return pltpu.CompilerParams(dimension_semantics=sem, vmem_limit_bytes=vmem)


def _split3(x):
    hi = x.astype(BF16)
    r1 = x - hi.astype(F32)
    mid = r1.astype(BF16)
    lo = (r1 - mid.astype(F32)).astype(BF16)
    return hi, mid, lo


def _dot(a, b):
    return jnp.dot(a, b, preferred_element_type=F32)


def _dot_nt(a, b):
    return lax.dot_general(a, b, (((1,), (1,)), ((), ())), preferred_element_type=F32)


def _dot_tn(a, b):
    return lax.dot_general(a, b, (((0,), (0,)), ((), ())), preferred_element_type=F32)


def _dot_exact_rhs(x, m_bf16):
    hi, mid, lo = _split3(x)
    return _dot(hi, m_bf16) + _dot(mid, m_bf16) + _dot(lo, m_bf16)


def _mm_kernel(a_ref, w_ref, b_ref, o_ref, acc_ref, *, nk, pre):
    if nk == 1:
        a = a_ref[...]
        if pre == "silu":
            a = a * jax.nn.sigmoid(a)
        o_ref[...] = (_dot(a.astype(BF16), w_ref[...].astype(BF16)) + b_ref[...]).astype(o_ref.dtype)
        return
    k = pl.program_id(2)

    @pl.when(k == 0)
    def _():
        acc_ref[...] = jnp.zeros_like(acc_ref)

    a = a_ref[...]
    if pre == "silu":
        a = a * jax.nn.sigmoid(a)
    acc_ref[...] += _dot(a.astype(BF16), w_ref[...].astype(BF16))

    @pl.when(k == nk - 1)
    def _():
        o_ref[...] = (acc_ref[...] + b_ref[...]).astype(o_ref.dtype)


def _matmul(a, w, bias=None, out_dtype=F32, pre=None):
    m, kdim = a.shape
    n = w.shape[1]
    tm = _pick(m, MM_TM)
    tn = _pick(n, MM_TN)
    tk = _pick(kdim, MM_TK)
    nk = kdim // tk
    if bias is None:
        bias = jnp.zeros((n,), F32)
    return pl.pallas_call(
        functools.partial(_mm_kernel, nk=nk, pre=pre),
        grid=(m // tm, n // tn, nk),
        in_specs=[pl.BlockSpec((tm, tk), lambda i, j, k: (i, k)),
                  pl.BlockSpec((tk, tn), lambda i, j, k: (k, j)),
                  pl.BlockSpec((1, tn), lambda i, j, k: (0, j))],
        out_specs=pl.BlockSpec((tm, tn), lambda i, j, k: (i, j)),
        out_shape=jax.ShapeDtypeStruct((m, n), out_dtype),
        scratch_shapes=[pltpu.VMEM((tm, tn), F32)],
        compiler_params=_params(("parallel", "parallel", "arbitrary")),
        name="matmul",
    )(a, w, bias.reshape(1, n).astype(F32))


def _adaln_kernel(c_ref, w_ref, b_ref, o_ref, acc_ref, *, nk):
    k = pl.program_id(2)

    @pl.when(k == 0)
    def _():
        acc_ref[...] = jnp.zeros_like(acc_ref)

    c = c_ref[...]
    c = c * jax.nn.sigmoid(c)
    acc_ref[...] += _dot(c.astype(BF16), w_ref[...].astype(BF16))

    @pl.when(k == nk - 1)
    def _():
        o_ref[...] = acc_ref[...] + b_ref[...]


def _adaln(cond, ada_w, ada_b):
    depth, d, n = ada_w.shape
    r = cond.shape[0]
    tn = _pick(n, (2048, 1024, 512, 256, 128))
    tk = _pick(d, (1024, 512, 256, 128))
    nk = d // tk
    return pl.pallas_call(
        functools.partial(_adaln_kernel, nk=nk),
        grid=(depth, n // tn, nk),
        in_specs=[pl.BlockSpec((r, tk), lambda l, j, k: (0, k)),
                  pl.BlockSpec((None, tk, tn), lambda l, j, k: (l, k, j)),
                  pl.BlockSpec((None, 1, tn), lambda l, j, k: (l, 0, j))],
        out_specs=pl.BlockSpec((None, r, tn), lambda l, j, k: (l, 0, j)),
        out_shape=jax.ShapeDtypeStruct((depth, r, n), F32),
        scratch_shapes=[pltpu.VMEM((r, tn), F32)],
        compiler_params=_params(("parallel", "parallel", "arbitrary")),
        name="adaln",
    )(cond, ada_w, ada_b.reshape(depth, 1, n))


def _rms(x, g):
    return x * lax.rsqrt(jnp.mean(x * x, axis=-1, keepdims=True) + RMS_EPS) * g


def _norm_mod_kernel(x_ref, g_ref, mod_ref, o_ref, *, i_shift, i_scale):
    y = _rms(x_ref[...], g_ref[...])
    o = y * (1.0 + mod_ref[i_scale:i_scale + 1, :]) + mod_ref[i_shift:i_shift + 1, :]
    o_ref[...] = o.astype(o_ref.dtype)


def _norm_mod(x, g, mod, i_shift, i_scale, out_dtype=BF16):
    t, d = x.shape
    seg = t // mod.shape[0]
    tr = _pick(seg, (256, 128, 64, 32, 16, 8))
    per = seg // tr
    return pl.pallas_call(
        functools.partial(_norm_mod_kernel, i_shift=i_shift, i_scale=i_scale),
        grid=(t // tr,),
        in_specs=[pl.BlockSpec((tr, d), lambda i: (i, 0)),
                  pl.BlockSpec((1, d), lambda i: (0, 0)),
                  pl.BlockSpec((None, 6, d), lambda i: (i // per, 0, 0))],
        out_specs=pl.BlockSpec((tr, d), lambda i: (i, 0)),
        out_shape=jax.ShapeDtypeStruct((t, d), out_dtype),
        compiler_params=_params(("parallel",)),
        name="norm_mod",
    )(x, g.reshape(1, d), mod)


def _residual_kernel(*refs, n_sum, i_gate, i_shift, i_scale, with_h, with_router, n_exp):
    x_ref = refs[0]
    o_refs = refs[1:1 + n_sum]
    ga_ref, mod_ref = refs[1 + n_sum:3 + n_sum]
    pos = 3 + n_sum
    if n_sum > 1:
        sg_ref = refs[pos]
        pos += 1
    if with_h:
        gb_ref, modh_ref = refs[pos:pos + 2]
        pos += 2
    if with_router:
        rw_ref, rb_ref = refs[pos:pos + 2]
        pos += 2
    xo_ref = refs[pos]
    pos += 1
    if n_sum == 1:
        o = o_refs[0][...]
    else:
        sg = sg_ref[...]
        o = o_refs[0][...] * sg[:, 0:1]
        for j in range(1, n_sum):
            o = o + o_refs[j][...] * sg[:, j:j + 1]
    xn =x_ref[...] + mod_ref[i_gate:i_gate + 1, :] * _rms(o, ga_ref[...])
    xo_ref[...] = xn
    if not with_h:
        return
    h_ref = refs[pos]
    pos += 1
    h = _rms(xn, gb_ref[...]) * (1.0 + modh_ref[i_scale:i_scale + 1, :]) + modh_ref[i_shift:i_shift + 1, :]
    h_ref[...] = h.astype(h_ref.dtype)
    if not with_router:
        return
    idx_ref, gate_ref = refs[pos:pos + 2]
    h1, h2, h3 = _split3(h)
    w1 = rw_ref[0]
    w2 = rw_ref[1]
    logits = (_dot(h1, w1) + _dot(h2, w1) + _dot(h1, w2) + _dot(h3, w1) + _dot(h2, w2)) + rb_ref[...]
    lane = lax.broadcasted_iota(jnp.int32, logits.shape, 1)
    logits = jnp.where(lane < n_exp, logits, NEG_BIG)
    vals, idxs = [], []
    for _ in range(TOP_K):
        m = jnp.max(logits, axis=-1, keepdims=True)
        ix = jnp.min(jnp.where(logits == m, lane, 1 << 30), axis=-1, keepdims=True)
        vals.append(m)
        idxs.append(ix)
        logits = jnp.where(lane == ix, NEG_BIG * 2, logits)
    es = [jnp.exp(v - vals[0]) for v in vals]
    den = es[0]
    for e in es[1:]:
        den = den + e
    idx_out = jnp.zeros(lane.shape, jnp.int32)
    gate_out = jnp.zeros(lane.shape, F32)
    for j in range(TOP_K):
        idx_out = jnp.where(lane == j, idxs[j], idx_out)
        gate_out = jnp.where(lane == j, es[j] / den, gate_out)
    idx_ref[...] = idx_out
    gate_ref[...] = gate_out


def _residual(x, o, g_a, mod, i_gate, n_sum=1, g_b=None, i_shift=0, i_scale=0, h_dtype=F32, router=None,
              mod_h=None, slot_gate=None):
    t, d = x.shape
    seg = t // mod.shape[0]
    tr = _pick(seg, (128, 64, 32, 16, 8))
    per = seg // tr
    nt = t // tr
    with_h = g_b is not None
    with_router = router is not None
    row = lambda i: (i, 0)
    fixed = lambda i: (0, 0)
    ins = [x] + [o] * n_sum + [g_a.reshape(1, d), mod]
    in_specs = ([pl.BlockSpec((tr, d), row)]
                + [pl.BlockSpec((tr, d), functools.partial(lambda j, i: (j * nt + i, 0), j)) for j in range(n_sum)]
                + [pl.BlockSpec((1, d), fixed), pl.BlockSpec((None, 6, d), lambda i: (i // per, 0, 0))])
    outs = [jax.ShapeDtypeStruct((t, d), F32)]
    out_specs = [pl.BlockSpec((tr, d), row)]
    n_exp = 0
    if n_sum > 1:
        ins.append(slot_gate)
        in_specs.append(pl.BlockSpec((tr, slot_gate.shape[1]), row))
    if with_h:
        ins += [g_b.reshape(1, d), mod if mod_h is None else mod_h]
        in_specs += [pl.BlockSpec((1, d), fixed), pl.BlockSpec((None, 6, d), lambda i: (i // per, 0, 0))]
        outs.append(jax.ShapeDtypeStruct((t, d), h_dtype))
        out_specs.append(pl.BlockSpec((tr, d), row))
    if with_router:
        rw, rb = router
        n_exp = rw.shape[1]
        rw = jnp.pad(rw, ((0, 0), (0, LANES - n_exp)))
        rw_hi = rw.astype(BF16)
        rw_lo = (rw - rw_hi.astype(F32)).astype(BF16)
        ins += [jnp.stack([rw_hi, rw_lo]), jnp.pad(rb, (0, LANES - n_exp)).reshape(1, LANES)]
        in_specs += [pl.BlockSpec((2, d, LANES), lambda i: (0, 0, 0)), pl.BlockSpec((1, LANES), fixed)]
        outs += [jax.ShapeDtypeStruct((t, LANES), jnp.int32), jax.ShapeDtypeStruct((t, LANES), F32)]
        out_specs += [pl.BlockSpec((tr, LANES), row), pl.BlockSpec((tr, LANES), row)]
    res = pl.pallas_call(
        functools.partial(_residual_kernel, n_sum=n_sum, i_gate=i_gate, i_shift=i_shift, i_scale=i_scale,
                          with_h=with_h, with_router=with_router, n_exp=n_exp),
        grid=(t // tr,),
        in_specs=in_specs,
        out_specs=out_specs,
        out_shape=outs,
        compiler_params=_params(("parallel",)),
        name="residual",
    )(*ins)
    return res


def _moe_kernel(nb_ref, be_ref, qs_ref, cnt_ref, order_ref, h_hbm, w1_ref, b1_ref, w2_ref, b2_ref, out_hbm,
                xbuf, obuf, sem_in, sem_out, *, bm, n_tok, f):
    del be_ref
    i = pl.program_id(0)
    nb = nb_ref[0]
    cur = i % 2
    shift = TOP_K.bit_length() - 1
    unroll = 8

    def gather_row(q0, buf, r):
        tok = lax.shift_right_logical(order_ref[q0 + r], shift)
        return pltpu.make_async_copy(h_hbm.at[pl.ds(tok, 1)], xbuf.at[buf, pl.ds(r, 1)], sem_in.at[buf])

    def scatter_row(q0, buf, r):
        a = order_ref[q0 + r]
        slot = (a & (TOP_K - 1)) * n_tok + lax.shift_right_logical(a, shift)
        return pltpu.make_async_copy(obuf.at[buf, pl.ds(r, 1)], out_hbm.at[pl.ds(slot, 1)], sem_out.at[buf])

    def for_rows(fn, n_rows):
        n_full = lax.shift_right_logical(n_rows, unroll.bit_length() - 1)

        def group(gi, c):
            for k in range(unroll):
                fn(gi * unroll + k)
            return c

        def single(r, c):
            fn(r)
            return c

        lax.fori_loop(0, n_full, group, 0)
        lax.fori_loop(n_full * unroll, n_rows, single, 0)

    def wait_gathered(buf):
        pltpu.make_async_copy(h_hbm.at[pl.ds(0, bm)], xbuf.at[buf], sem_in.at[buf]).wait()

    def wait_scattered(buf, n_rows):
        row = pltpu.make_async_copy(obuf.at[buf, pl.ds(0, 1)], out_hbm.at[pl.ds(0, 1)], sem_out.at[buf])
        for_rows(lambda r: row.wait(), n_rows)

    @pl.when(i < nb)
    def _():
        @pl.when(i == 0)
        def _():
            for_rows(lambda r: gather_row(qs_ref[0], 0, r).start(), bm)

        wait_gathered(cur)
        q_next = qs_ref[jnp.minimum(i + 1, nb - 1)]
        for r in range(bm):
            gather_row(q_next, 1 - cur, r).start()
        x = xbuf[cur].astype(BF16)
        u = _dot(x, w1_ref[...]) + b1_ref[...]
        glu = jnp.minimum(u[:, :f], SWIGLU_LIMIT)
        lin = jnp.clip(u[:, f:], -SWIGLU_LIMIT, SWIGLU_LIMIT)
        act = glu * jax.nn.sigmoid(SWIGLU_ALPHA * glu) * (lin + 1.0)
        out = _dot(act.astype(BF16), w2_ref[...]) + b2_ref[...]

        @pl.when(i >= 1)
        def _():
            wait_scattered(1 - cur, cnt_ref[jnp.maximum(i - 1, 0)])

        obuf[cur] = out
        q_cur = qs_ref[i]
        for_rows(lambda r: scatter_row(q_cur, cur, r).start(), cnt_ref[i])

        @pl.when(i == nb - 1)
        def _():
            wait_scattered(cur, cnt_ref[i])
            wait_gathered(1 - cur)


def _moe(h, top_idx, w1, b1, w2, b2):
    assert TOP_K & (TOP_K - 1) == 0
    t, d = h.shape
    n_exp, f = w2.shape[0], w2.shape[1]
    bm = MOE_ROWS
    n_assign = t * TOP_K
    flat_e = top_idx[:, :TOP_K].reshape(-1)
    order = jnp.argsort(flat_e, stable=True).astype(jnp.int32)
    counts = jnp.sum(flat_e[:, None] == jnp.arange(n_exp, dtype=flat_e.dtype)[None, :], axis=0, dtype=jnp.int32)
    padded = (counts + bm - 1) // bm * bm
    pad_end = jnp.cumsum(padded)
    pad_start = pad_end - padded
    sort_start = jnp.cumsum(counts) - counts
    n_blocks = -(-(n_assign + n_exp * (bm - 1)) // bm)
    blk_row = jnp.arange(n_blocks, dtype=jnp.int32) * bm
    block_exp = jnp.minimum(jnp.searchsorted(pad_end, blk_row, side="right"), n_exp - 1).astype(jnp.int32)
    off = blk_row - pad_start[block_exp]
    q_start = (sort_start[block_exp] + off).astype(jnp.int32)
    cnt = jnp.clip(counts[block_exp] - off, 0, bm).astype(jnp.int32)
    n_used = (pad_end[-1:] // bm).astype(jnp.int32)
    pre = lambda i, nb, be, qs, ct, od: (be[i], 0, 0)
    grid_spec = pltpu.PrefetchScalarGridSpec(
        num_scalar_prefetch=5,
        grid=(n_blocks,),
        in_specs=[pl.BlockSpec(memory_space=pl.ANY),
                  pl.BlockSpec((None, d, 2 * f), pre),
                  pl.BlockSpec((None, 1, 2 * f), pre),
                  pl.BlockSpec((None, f, d), pre),
                  pl.BlockSpec((None, 1, d), pre)],
        out_specs=pl.BlockSpec(memory_space=pl.ANY),
        scratch_shapes=[pltpu.VMEM((2, bm, d), F32), pltpu.VMEM((2, bm, d), F32),
                        pltpu.SemaphoreType.DMA((2,)), pltpu.SemaphoreType.DMA((2,))],
    )
    return pl.pallas_call(
        functools.partial(_moe_kernel, bm=bm, n_tok=t, f=f),
        grid_spec=grid_spec,
        out_shape=jax.ShapeDtypeStruct((n_assign, d), F32),
        compiler_params=_params(("arbitrary",)),
        name="moe_experts",
    )(n_used, block_exp, q_start, cnt, jnp.pad(order, (0, bm)), h, w1, b1, w2, b2)


def _moe_weights(w1, b1, w2, b2):
    n_exp, d, f2 = w1.shape
    perm = np.concatenate([np.arange(0, f2, 2), np.arange(1, f2, 2)])
    pmat = jnp.asarray(np.arange(f2)[:, None] == perm[None, :], BF16)
    w1p = _matmul(w1.reshape(n_exp * d, f2), pmat, out_dtype=BF16).reshape(n_exp, d, f2)
    return w1p, b1[:, perm].reshape(n_exp, 1, f2), w2.astype(BF16), b2.reshape(n_exp, 1, -1)


def _row_shift(u, prev8, next8, k):
    n = u.shape[0]
    rolled = pltpu.roll(u, k % n, 0)
    sub = lax.broadcasted_iota(jnp.int32, (SUBLANES, u.shape[1]), 0)
    if k > 0:
        halo = pltpu.roll(prev8, k, 0)
        top = jnp.where(sub < k, halo, rolled[0:SUBLANES])
        return jnp.concatenate([top, rolled[SUBLANES:]], axis=0)
    halo = pltpu.roll(next8, SUBLANES + k, 0)
    bot = jnp.where(sub >= SUBLANES + k, halo, rolled[n - SUBLANES:])
    return jnp.concatenate([rolled[:n - SUBLANES], bot], axis=0)


def _halo_tiles(prev_ref, next_ref, per):
    j = pl.program_id(0) % per
    prev8 = jnp.where(j == 0, 0.0, prev_ref[...])
    next8 = jnp.where(j == per - 1, 0.0, next_ref[...])
    return prev8, next8


def _conv_kernel(u_ref, prev_ref, next_ref, w_ref, b_ref, o_ref, *, per):
    u = u_ref[...]
    prev8, next8 = _halo_tiles(prev_ref, next_ref, per)
    acc = (w_ref[0:1, :] * _row_shift(u, prev8, next8, 2) + w_ref[1:2, :] * _row_shift(u, prev8, next8, 1)
           + w_ref[2:3, :] * u + w_ref[3:4, :] * _row_shift(u, prev8, next8, -1)) + b_ref[...]
    o_ref[...] = acc * jax.nn.sigmoid(acc)


def _row_neighbour_call(kernel_fn, u, seg, extra, name):
    t, n = u.shape
    tr = _pick(seg, (256, 128, 64, 32, 16, 8))
    tc = _pick(n, (2048, 1024, 768, 512, 384, 256, 128))
    per = seg // tr
    r8 = tr // SUBLANES
    last8 = t // SUBLANES - 1
    in_specs = [pl.BlockSpec((tr, tc), lambda i, j: (i, j)),
                pl.BlockSpec((SUBLANES, tc), lambda i, j: (jnp.maximum(i * r8 - 1, 0), j)),
                pl.BlockSpec((SUBLANES, tc), lambda i, j: (jnp.minimum((i + 1) * r8, last8), j))]
    for e in extra:
        in_specs.append(pl.BlockSpec((e.shape[0], tc), lambda i, j: (0, j)))
    return pl.pallas_call(
        functools.partial(kernel_fn, per=per),
        grid=(t // tr, n // tc),
        in_specs=in_specs,
        out_specs=pl.BlockSpec((tr, tc), lambda i, j: (i, j)),
        out_shape=jax.ShapeDtypeStruct((t, n), F32),
        compiler_params=_params(("parallel", "parallel")),
        name=name,
    )(u, u, u, *extra)


def _head_sum_matrix(n):
    idx = np.arange(n) // HEAD_DIM
    return jnp.asarray(idx[:, None] == idx[None, :], BF16)


def _centred_shift(u_ref, prev_ref, next_ref, mu_ref, per):
    u = u_ref[...]
    prev8, next8 = _halo_tiles(prev_ref, next_ref, per)
    nb = 0.5 * (_row_shift(u, prev8, next8, 1) + _row_shift(u, prev8, next8, -1))
    return u + mu_ref[...] * (nb - u)


def _rwkv_prep_kernel(*refs, wd_w, ad_w, per):
    r, k, v, lo = [_centred_shift(*refs[4 * n:4 * n + 4], per) for n in range(4)]
    (w0_ref, wup_ref, a0_ref, aup_ref, gup_ref, kk_ref_, ka_ref, rk_ref, hs_ref,
     r_o, v_o, kk_o, lw_o, kd_o, b_o, g_o, bv_o) = refs[16:]
    wd =jnp.tanh(lo[:, 0:wd_w]).astype(BF16)
    ad = lo[:, wd_w:wd_w + ad_w].astype(BF16)
    gs = jax.nn.sigmoid(lo[:, wd_w + ad_w:]).astype(BF16)
    hs = hs_ref[...]
    kkr = k * kk_ref_[...]
    ss = _dot_exact_rhs(kkr * kkr, hs)
    kk = kkr * lax.rsqrt(jnp.maximum(ss, 1e-24))
    ksum = jnp.zeros_like(k)
    for d in range(2):
        z = -(w0_ref[d:d + 1, :] + _dot(wd, wup_ref[d]))
        w_log = -(jnp.maximum(z, 0.0) + jnp.log(1.0 + jnp.exp(-jnp.abs(z)))) - 0.5
        lw_o[d] = -jnp.exp(w_log)
        asig = jax.nn.sigmoid(a0_ref[d:d + 1, :] + _dot(ad, aup_ref[d]))
        kd = k * (1.0 + (asig - 1.0) * ka_ref[...])
        kd_o[d] = kd
        b_o[d] = kk * asig
        ksum = ksum + kd
    bonus = _dot_exact_rhs(r * ksum * rk_ref[...], hs)
    r_o[...] = r
    v_o[...] = v
    kk_o[...] = kk
    g_o[...] = _dot(gs, gup_ref[...])
    bv_o[...] = bonus * v


def _rwkv_prep(p_rkv, p_lora, mu_rkv, mu_lora, p, seg):
    t = p_rkv.shape[0]
    a = p_rkv.shape[1] // 3
    nl = p_lora.shape[1]
    tr = _pick(seg, (256, 128, 64, 32, 16, 8))
    tc = _pick(a, (512, 256, 128))
    nj = a // tc
    per = seg // tr
    r8 = tr // SUBLANES
    last8 = t // SUBLANES - 1

    def shifted_specs(width, colmap):
        return [pl.BlockSpec((tr, width), lambda i, j: (i, colmap(j))),
                pl.BlockSpec((SUBLANES, width), lambda i, j: (jnp.maximum(i * r8 - 1, 0), colmap(j))),
                pl.BlockSpec((SUBLANES, width), lambda i, j: (jnp.minimum((i + 1) * r8, last8), colmap(j))),
                pl.BlockSpec((1, width), lambda i, j: (0, colmap(j)))]

    shifted_in = []
    shifted_args = []
    for off in range(3):
        shifted_in += shifted_specs(tc, functools.partial(lambda o, j: o * nj + j, off))
        shifted_args += [p_rkv, p_rkv, p_rkv, mu_rkv]
    shifted_in += shifted_specs(nl, lambda j: 0)
    shifted_args += [p_lora, p_lora, p_lora, mu_lora]
    par = lambda rows: pl.BlockSpec((rows, tc), lambda i, j: (0, j))
    par3 = lambda rows: pl.BlockSpec((2, rows, tc), lambda i, j: (0, 0, j))
    one = jax.ShapeDtypeStruct((t, a), F32)
    two = jax.ShapeDtypeStruct((2, t, a), F32)
    o1 = pl.BlockSpec((tr, tc), lambda i, j: (i, j))
    o2 = pl.BlockSpec((2, tr, tc), lambda i, j: (0, i, j))
    return pl.pallas_call(
        functools.partial(_rwkv_prep_kernel, wd_w=p["wd_w"], ad_w=p["ad_w"], per=per),
        grid=(t // tr, nj),
        in_specs=shifted_in + [par(2), par3(p["wd_w"]), par(2), par3(p["ad_w"]), par(p["g_up"].shape[0]),
                               par(1), par(1), par(1),
                               pl.BlockSpec((tc, tc), lambda i, j: (0, 0))],
        out_specs=[o1, o1, o1, o2, o2, o2, o1, o1],
        out_shape=[one, one, one, two, two, two, one, one],
        compiler_params=_params(("parallel", "parallel")),
        name="rwkv_prep",
    )(*shifted_args, p["w0"], p["w_up"], p["a0"], p["a_up"], p["g_up"], p["k_k"], p["k_a"],
      p["r_k"], _head_sum_matrix(tc))


def _rwkv_chunk_pairs(ins, states, d, masks):
    incl, strict, tri, eye, first, first2, blockdiag = masks
    cl = RWKV_CHUNK
    pairs = range(len(ins))
    heads = [(p, hh) for p in pairs for hh in range(2)]
    splits = [_split3(ins[p][3]) for p in pairs]
    cum = [_dot(tri, h1) + _dot(tri, h2) + _dot(tri, h3) for h1, h2, h3 in splits]
    cum_last, ar, ar_b, bk, bk_end, sb, vb = [], [], [], [], [], [], []
    for p in pairs:
        r, v, kk, lw, kd, b = ins[p]
        cl_p = jnp.where(d == 0, cum[p][cl - 1:cl, :], cum[p][0:1, :])
        ginv = jnp.exp(-cum[p])
        to_end = jnp.exp(cl_p - cum[p])
        at = -kk * jnp.exp(cum[p] - lw)
        rt = r * jnp.exp(cum[p])
        cum_last.append(cl_p)
        bk.append(jnp.concatenate([b * ginv, kd * ginv], axis=0).astype(BF16))
        bk_end.append(jnp.concatenate([b * to_end, kd * to_end], axis=0).astype(BF16))
        ar.append(jnp.concatenate([at, rt], axis=0))
        ar_b.append(ar[p].astype(BF16))
        sb.append(states[p].astype(BF16))
        vb.append(v.astype(BF16))
    ar_h = {(p, hh): jnp.where(first2 if hh == 0 else jnp.logical_not(first2), ar[p], 0.0).astype(BF16)
            for p, hh in heads}
    m = {k: _dot_nt(ar_h[k], bk[k[0]]) for k in heads}
    xs = [_dot_nt(ar_b[p], sb[p]) for p in pairs]
    a_ab = {k: jnp.where(strict, m[k][:cl, :cl], 0.0) for k in heads}
    a_ak = {k: jnp.where(strict, m[k][:cl, cl:], 0.0).astype(BF16) for k in heads}
    rbk = {k: jnp.concatenate([jnp.where(incl, m[k][cl:, :cl], 0.0), jnp.where(incl, m[k][cl:, cl:], 0.0)],
                              axis=1).astype(BF16) for k in heads}
    rhs = {k: (xs[k[0]][:cl] + _dot(a_ak[k], vb[k[0]])).astype(BF16) for k in heads}
    tinv = {k: eye + a_ab[k] for k in heads}
    pw = {k: a_ab[k].astype(BF16) for k in heads}
    for _ in range(int(math.log2(cl)) - 1):
        pw = {k: _dot(pw[k], pw[k]).astype(BF16) for k in heads}
        tinv = {k: tinv[k] + _dot(tinv[k].astype(BF16), pw[k]) for k in heads}
    u_h = {k: _dot(tinv[k].astype(BF16), rhs[k]) for k in heads}
    uv = [jnp.concatenate([jnp.where(first, u_h[(p, 0)], u_h[(p, 1)]), ins[p][1]], axis=0).astype(BF16)
          for p in pairs]
    y_h = {k: _dot(rbk[k], uv[k[0]]) for k in heads}
    upd = [_dot_tn(uv[p], bk_end[p]) for p in pairs]
    out = []
    for p in pairs:
        y = xs[p][cl:] + jnp.where(first, y_h[(p, 0)], y_h[(p, 1)])
        s_new = states[p] * jnp.exp(cum_last[p]) + jnp.where(blockdiag, upd[p], 0.0)
        out.append((y, s_new))
    return out


def _rwkv_scan_kernel(r_ref, v_ref, kk_ref, lw_ref, kd_ref, b_ref, s0_ref, y_ref, sf_ref, s_scr, *, n_chunks,
                      n_pairs):
    d = pl.program_id(1)
    c = pl.program_id(3)
    cl = RWKV_CHUNK

    @pl.when(c == 0)
    def _():
        s_scr[...] = s0_ref[...]

    sgn = 1 - 2 * d
    row = lax.broadcasted_iota(jnp.int32, (cl, cl), 0)
    col = lax.broadcasted_iota(jnp.int32, (cl, cl), 1)
    order = (row - col) * sgn
    incl = order >= 0
    strict = order > 0
    tri = jnp.where(incl, 1.0, 0.0).astype(BF16)
    eye = jnp.where(row == col, 1.0, 0.0)
    first = lax.broadcasted_iota(jnp.int32, (cl, LANES), 1) < HEAD_DIM
    first2 = lax.broadcasted_iota(jnp.int32, (2 * cl, LANES), 1) < HEAD_DIM
    rr = lax.broadcasted_iota(jnp.int32, (LANES, LANES), 0) // HEAD_DIM
    cc = lax.broadcasted_iota(jnp.int32, (LANES, LANES), 1) // HEAD_DIM
    masks = (incl, strict, tri, eye, first, first2, rr == cc)
    ins = []
    for p in range(n_pairs):
        sl = slice(p * LANES, (p + 1) * LANES)
        ins.append((r_ref[:, sl], v_ref[:, sl], kk_ref[:, sl], lw_ref[:, sl], kd_ref[:, sl], b_ref[:, sl]))
    results = _rwkv_chunk_pairs(ins, [s_scr[p] for p in range(n_pairs)], d, masks)
    for p, (y, s_new) in enumerate(results):
        y_ref[:, p * LANES:(p + 1) * LANES] = y
        s_scr[p] = s_new

    @pl.when(c == n_chunks - 1)
    def _():
        sf_ref[...] = s_scr[...]


def _rwkv_scan(r, v, kk, lw, kd, b, s0, n_seg):
    t, a = r.shape
    seg = t // n_seg
    cl = RWKV_CHUNK
    nc = seg // cl
    npair = a // LANES
    pb = _pick(npair, (RWKV_PAIRS_PER_STEP, 4, 2, 1))
    w = pb * LANES

    def rows(bi, d, p, c):
        return bi * nc + jnp.where(d == 0, c, nc - 1 - c)

    shared = pl.BlockSpec((cl, w), lambda bi, d, p, c: (rows(bi, d, p, c), p))
    per_dir = pl.BlockSpec((None, cl, w), lambda bi, d, p, c: (d, rows(bi, d, p, c), p))
    state = pl.BlockSpec((None, None, pb, LANES, LANES), lambda bi, d, p, c: (bi, d, p, 0, 0))
    return pl.pallas_call(
        functools.partial(_rwkv_scan_kernel, n_chunks=nc, n_pairs=pb),
        grid=(n_seg, 2, npair // pb, nc),
        in_specs=[shared, shared, shared, per_dir, per_dir, per_dir, state],
        out_specs=[per_dir, state],
        out_shape=[jax.ShapeDtypeStruct((2, t, a), F32), jax.ShapeDtypeStruct(s0.shape, F32)],
        scratch_shapes=[pltpu.VMEM((pb, LANES, LANES), F32)],
        compiler_params=_params(("parallel", "parallel", "parallel", "arbitrary")),
        name="rwkv_scan",
    )(r, v, kk, lw, kd, b, s0)


def _rwkv_out_kernel(y_ref, g_ref, bv_ref, lnw_ref, lnb_ref, hs_ref, o_ref):
    y = y_ref[0] + y_ref[1]
    hs = hs_ref[...]
    inv = 1.0 / HEAD_DIM
    mu = _dot_exact_rhs(y, hs) * inv
    yc = y - mu
    var = _dot_exact_rhs(yc * yc, hs) * inv
    yn = yc * lax.rsqrt(var + GN_EPS) * lnw_ref[...] + lnb_ref[...]
    o_ref[...] = ((yn + bv_ref[...]) * g_ref[...]).astype(o_ref.dtype)


def _rwkv_out(y, g, bv, ln_w, ln_b):
    _, t, a = y.shape
    tr = _pick(t, (256, 128, 64, 32, 16, 8))
    tc = _pick(a, (512, 256, 128))
    o1 = pl.BlockSpec((tr, tc), lambda i, j: (i, j))
    par = pl.BlockSpec((1, tc), lambda i, j: (0, j))
    return pl.pallas_call(
        _rwkv_out_kernel,
        grid=(t // tr, a // tc),
        in_specs=[pl.BlockSpec((2, tr, tc), lambda i, j: (0, i, j)), o1, o1, par, par,
                  pl.BlockSpec((tc, tc), lambda i, j: (0, 0))],
        out_specs=o1,
        out_shape=jax.ShapeDtypeStruct((t, a), BF16),
        compiler_params=_params(("parallel", "parallel")),
        name="rwkv_out",
    )(y, g, bv, ln_w.reshape(1, a), ln_b.reshape(1, a), _head_sum_matrix(tc))


def _rwkv_params(shift_mu, w0, w_up, a0, a_up, g_up, k_k, k_a, r_k, a_width):
    dr, ar_, gr = w_up.shape[1], a_up.shape[1], g_up.shape[0]
    pad = lambda n: -(-n // LANES) * LANES
    wd_w, ad_w, gd_w = pad(2 * dr), pad(2 * ar_), pad(gr)

    def up(wu, rank, width):
        out = jnp.zeros((2, width, a_width), F32)
        for d in range(2):
            out = out.at[d, d * rank:(d + 1) * rank].set(wu[d])
        return out.astype(BF16)

    return dict(wd_w=wd_w, ad_w=ad_w, gd_w=gd_w, dr=dr, ar=ar_, gr=gr,
                w0=w0, a0=a0, w_up=up(w_up, dr, wd_w), a_up=up(a_up, ar_, ad_w),
                g_up=jnp.pad(g_up, ((0, gd_w - gr), (0, 0))).astype(BF16),
                k_k=k_k.reshape(1, -1), k_a=k_a.reshape(1, -1), r_k=r_k.reshape(1, -1))


def _pad_lora_cols(w, dr2, ar2, gr, p):
    parts = [(w[..., :dr2], p["wd_w"]), (w[..., dr2:dr2 + ar2], p["ad_w"]), (w[..., dr2 + ar2:], p["gd_w"])]
    return jnp.concatenate([jnp.pad(x, [(0, 0)] * (x.ndim - 1) + [(0, wd - x.shape[-1])]) for x, wd in parts],
                           axis=-1)


def _na_bias(rpb, rows):
    kr = min(WIN_R, rows)
    n_heads, n_ro, n_off = rpb.shape
    assert LANES % GRID_W == 0 and (kr * GRID_W) % LANES == 0
    return pl.pallas_call(
        functools.partial(_na_bias_kernel, kr=kr, n_ro=n_ro, n_off=n_off),
        grid=(n_heads, kr),
        in_specs=[pl.BlockSpec(memory_space=pltpu.SMEM)],
        out_specs=pl.BlockSpec((None, None, GRID_W, kr * GRID_W), lambda h, p: (h, p, 0, 0)),
        out_shape=jax.ShapeDtypeStruct((n_heads, kr, GRID_W, kr * GRID_W), F32),
        compiler_params=_params(("parallel", "parallel")),
        name="na_bias_table",
    )(rpb.reshape(-1).astype(F32))


def _na_bias_kernel(rpb_ref, o_ref, *, kr, n_ro, n_off):
    h = pl.program_id(0)
    pat = pl.program_id(1)
    w = GRID_W
    per_tile = LANES // w
    q = lax.broadcasted_iota(jnp.int32, (w, LANES), 0)
    lane = lax.broadcasted_iota(jnp.int32, (w, LANES), 1)
    i_local = lane // w
    c = lane - i_local * w
    off = c - q + (WIN_C - 1)
    start = jnp.clip(q - WIN_C // 2, 0, w - WIN_C)
    inside = jnp.logical_and(c >= start, c < start + WIN_C)
    for t in range(kr // per_tile):
        base = [(h * n_ro + (t * per_tile + k - pat + WIN_R - 1)) * n_off for k in range(per_tile)]
        val = jnp.zeros((w, LANES), F32)
        for u in range(n_off):
            s = rpb_ref[base[per_tile - 1] + u]
            for k in range(per_tile - 2, -1, -1):
                s = jnp.where(i_local == k, rpb_ref[base[k] + u], s)
            val = jnp.where(off == u, s, val)
        o_ref[:, t * LANES:(t + 1) * LANES] = jnp.where(inside, val, NEG_BIG)


def _softmax_pv(s_list, v_list):
    m = s_list[0].max(axis=-1, keepdims=True)
    for s in s_list[1:]:
        m = jnp.maximum(m, s.max(axis=-1, keepdims=True))
    den = 0.0
    acc = 0.0
    for s, vv in zip(s_list, v_list):
        p = jnp.exp(s - m)
        den = den + p.sum(axis=-1, keepdims=True)
        acc = acc + _dot(p.astype(BF16), vv)
    return acc / den


def _na_kernel(q_ref, k_ref, v_ref, kc_ref, vc_ref, bias_ref, o_ref, *, rows, kr):
    w = GRID_W
    lane = lax.broadcasted_iota(jnp.int32, (w, LANES), 1)
    first = lane < HEAD_DIM
    kc = kc_ref[...]
    vc = vc_ref[...]
    scale = HEAD_DIM ** -0.5

    rb = NA_ROWS_PER_STEP if rows % NA_ROWS_PER_STEP == 0 else 1

    def body(it, carry):
        rr = [it * rb + k for k in range(rb)]
        r0 = [jnp.clip(r - kr // 2, 0, rows - kr) for r in rr]
        q = [(q_ref[pl.ds(pl.multiple_of(r * w, w), w), :].astype(F32) * scale).astype(BF16) for r in rr]
        kw = [k_ref[pl.ds(pl.multiple_of(r * w, w), kr * w), :] for r in r0]
        vw = [v_ref[pl.ds(pl.multiple_of(r * w, w), kr * w), :] for r in r0]
        chains = [(k, hh) for k in range(rb) for hh in range(2)]
        qm = {(k, hh): jnp.where(first if hh == 0 else jnp.logical_not(first), q[k], jnp.zeros_like(q[k]))
              for k, hh in chains}
        s_loc = {c: _dot_nt(qm[c], kw[c[0]]) + bias_ref[c[1], pl.ds(rr[c[0]] - r0[c[0]], 1)][0] for c in chains}
        s_ctx = {c: _dot_nt(qm[c], kc) for c in chains}
        mx = {c: jnp.maximum(s_loc[c].max(axis=-1, keepdims=True), s_ctx[c].max(axis=-1, keepdims=True))
              for c in chains}
        p_loc = {c: jnp.exp(s_loc[c] - mx[c]) for c in chains}
        p_ctx = {c: jnp.exp(s_ctx[c] - mx[c]) for c in chains}
        den = {c: p_loc[c].sum(axis=-1, keepdims=True) + p_ctx[c].sum(axis=-1, keepdims=True) for c in chains}
        acc = {c: _dot(p_loc[c].astype(BF16), vw[c[0]]) + _dot(p_ctx[c].astype(BF16), vc) for c in chains}
        for k in range(rb):
            out = jnp.where(first, acc[(k, 0)] / den[(k, 0)], acc[(k, 1)] / den[(k, 1)])
            o_ref[pl.ds(pl.multiple_of(rr[k] * w, w), w), :] = out.astype(o_ref.dtype)
        return carry

    lax.fori_loop(0, rows // rb, body, 0)


def _ctx_attn_kernel(q_ref, k_ref, v_ref, o_ref):
    n = q_ref.shape[0]
    lane = lax.broadcasted_iota(jnp.int32, (n, LANES), 1)
    first = lane < HEAD_DIM
    q = (q_ref[...].astype(F32) * HEAD_DIM ** -0.5).astype(BF16)
    k = k_ref[...]
    v = v_ref[...]
    outs = []
    for hh in range(2):
        mh = first if hh == 0 else jnp.logical_not(first)
        qm = jnp.where(mh, q, jnp.zeros_like(q))
        outs.append(_softmax_pv([_dot_nt(qm, k)], [v]))
    o_ref[...] = jnp.where(first, outs[0], outs[1]).astype(o_ref.dtype)


def _attention(qkv_x, qkv_c, rpb, n_seg, need_ctx):
    tx, w3 = qkv_x.shape
    bw = w3 // 3
    npair = bw // LANES
    lx = tx // n_seg
    lc = qkv_c.shape[0] // n_seg
    rows = lx // GRID_W
    kr = min(WIN_R, rows)
    bias = _na_bias(rpb, rows)
    blk = lambda length, off: pl.BlockSpec((length, LANES), lambda bi, p: (bi, off * npair + p))
    nx = pl.pallas_call(
        functools.partial(_na_kernel, rows=rows, kr=kr),
        grid=(n_seg, npair),
        in_specs=[blk(lx, 0), blk(lx, 1), blk(lx, 2), blk(lc, 1), blk(lc, 2),
                  pl.BlockSpec((2, kr, GRID_W, kr * GRID_W), lambda bi, p: (p, 0, 0, 0))],
        out_specs=pl.BlockSpec((lx, LANES), lambda bi, p: (bi, p)),
        out_shape=jax.ShapeDtypeStruct((tx, bw), BF16),
        compiler_params=_params(("parallel", "parallel")),
        name="neighbourhood_attention",
    )(qkv_x, qkv_x, qkv_x, qkv_c, qkv_c, bias)
    ncx = None
    if need_ctx:
        ncx = pl.pallas_call(
            _ctx_attn_kernel,
            grid=(n_seg, npair),
            in_specs=[blk(lc, 0), blk(lc, 1), blk(lc, 2)],
            out_specs=pl.BlockSpec((lc, LANES), lambda bi, p: (bi, p)),
            out_shape=jax.ShapeDtypeStruct((qkv_c.shape[0], bw), BF16),
            compiler_params=_params(("parallel", "parallel")),
            name="context_attention",
        )(qkv_c, qkv_c, qkv_c)
    return nx, ncx


def _even_mixer(hx, hc, n_seg, w_in, shift_mu, w0, w_up, a0, a_up, g_up, k_k, k_a, r_k, ln_w, ln_b, rpb, w_out,
                need_ctx):
    d = hx.shape[1]
    a_width = k_k.shape[0]
    p = _rwkv_params(shift_mu, w0, w_up, a0, a_up, g_up, k_k, k_a, r_k, a_width)
    dr2, ar2, gr = 2 * p["dr"], 2 * p["ar"], p["gr"]
    a_cols = 3 * a_width + dr2 + ar2 + gr
    w_rkv = w_in[:, :3 * a_width].astype(BF16)
    w_lora = _pad_lora_cols(w_in[:, 3 * a_width:a_cols], dr2, ar2, gr, p).astype(BF16)
    w_qkv = w_in[:, a_cols:].astype(BF16)
    mu_rkv = shift_mu[:3 * a_width].reshape(1, -1)
    mu_lora = _pad_lora_cols(shift_mu[3 * a_width:], dr2, ar2, gr, p).reshape(1, -1)
    w_out_b = w_out.astype(BF16)

    def rwkv_side(h, s0):
        seg = h.shape[0] // n_seg
        r, v, kk, lw, kd, b, g, bv = _rwkv_prep(_matmul(h, w_rkv), _matmul(h, w_lora), mu_rkv, mu_lora, p, seg)
        y, s_fin = _rwkv_scan(r, v, kk, lw, kd, b, s0, n_seg)
        return (y, g, bv), s_fin

    s_zero = jnp.zeros((n_seg, 2, a_width // LANES, LANES, LANES), F32)
    terms_c, s_ctx = rwkv_side(hc, s_zero)
    terms_x, _ = rwkv_side(hx, s_ctx)
    qkv_x = _matmul(hx, w_qkv, out_dtype=BF16)
    qkv_c = _matmul(hc, w_qkv, out_dtype=BF16)
    nx, ncx = _attention(qkv_x, qkv_c, rpb, n_seg, need_ctx)
    rx = _rwkv_out(*terms_x, ln_w, ln_b)
    ox = _matmul(jnp.concatenate([rx, nx], axis=1), w_out_b)
    oc = None
    if need_ctx:
        rc = _rwkv_out(*terms_c, ln_w, ln_b)
        oc = _matmul(jnp.concatenate([rc, ncx], axis=1), w_out_b)
    return ox, oc


def _softplus(x):
    return jnp.maximum(x, 0.0) + jnp.log(1.0 + jnp.exp(-jnp.abs(x)))


def _ssd_scan_kernel(x_ref, b_ref, c_ref, dt_ref, bias_ref, alog_ref, s0_ref, y_ref, sf_ref, s_scr, *,
                     n_chunks, n_e, n_grp):
    d = pl.program_id(1)
    gb = pl.program_id(2)
    c = pl.program_id(3)
    cl = x_ref.shape[0]
    hp = dt_ref.shape[1]
    ep = n_e * SSM_HEAD_DIM
    grp = range(n_grp)

    @pl.when(c == 0)
    def _():
        s_scr[...] = s0_ref[...]

    sgn = 1 - 2 * d
    row = lax.broadcasted_iota(jnp.int32, (cl, cl), 0)
    col = lax.broadcasted_iota(jnp.int32, (cl, cl), 1)
    incl = (row - col) * sgn >= 0
    tri = jnp.where(incl, 1.0, 0.0).astype(BF16)
    dt_all = _softplus(dt_ref[...] + bias_ref[...])
    dta_all = dt_all * (-jnp.exp(alog_ref[...]))
    dt_parts = _split3(dt_all)
    dta_parts = _split3(dta_all)
    hr = lax.broadcasted_iota(jnp.int32, (hp, LANES), 0)
    hc = lax.broadcasted_iota(jnp.int32, (hp, LANES), 1)
    sel = [jnp.where(jnp.logical_and(hr == (gb * n_grp + gi) * n_e + hc, hc < n_e), 1.0, 0.0).astype(BF16)
           for gi in grp]
    dt_g = [sum(_dot(p, sel[gi]) for p in dt_parts) for gi in grp]
    dta_g = [sum(_dot(p, sel[gi]) for p in dta_parts) for gi in grp]
    dta_split = [_split3(dta_g[gi]) for gi in grp]
    cum = [sum(_dot(tri, p) for p in dta_split[gi]) for gi in grp]
    bm = [b_ref[:, gi * SSM_STATE:(gi + 1) * SSM_STATE].astype(BF16) for gi in grp]
    cm = [c_ref[:, gi * SSM_STATE:(gi + 1) * SSM_STATE].astype(BF16) for gi in grp]
    cb = [_dot_nt(cm[gi], bm[gi]) for gi in grp]
    state = [s_scr[gi] for gi in grp]
    y_state = [_dot(cm[gi], state[gi].astype(BF16)) for gi in grp]
    cum_t = [cum[gi].T for gi in grp]
    dt_t = [dt_g[gi].T for gi in grp]
    cum_last = [jnp.where(d == 0, cum[gi][cl - 1:cl, :], cum[gi][0:1, :]) for gi in grp]
    e_cum = [jnp.exp(cum[gi]) for gi in grp]
    dt_end = [dt_g[gi] * jnp.exp(cum_last[gi] - cum[gi]) for gi in grp]
    e_last = [jnp.exp(cum_last[gi]) for gi in grp]
    first = lax.broadcasted_iota(jnp.int32, (cl, LANES), 1) < SSM_HEAD_DIM
    first_row = first[0:1, :]
    x_end = [[] for _ in grp]
    decays = [[] for _ in grp]
    for q in range(n_e // 2):
        j0, j1 = 2 * q, 2 * q + 1
        pick = lambda t: jnp.where(first, t[:, j0:j0 + 1], t[:, j1:j1 + 1])
        xq = [x_ref[:, gi * ep + q * LANES:gi * ep + (q + 1) * LANES] for gi in grp]
        x_b = [xq[gi].astype(BF16) for gi in grp]
        m = {(gi, j): (cb[gi] * jnp.exp(jnp.where(incl, cum[gi][:, j:j + 1] - cum_t[gi][j:j + 1, :], NEG_BIG))
                       * dt_t[gi][j:j + 1, :]).astype(BF16) for gi in grp for j in (j0, j1)}
        ys = {k: _dot(m[k], x_b[k[0]]) for k in m}
        for gi in grp:
            lo = gi * ep + q * LANES
            y_ref[:, lo:lo + LANES] = (jnp.where(first, ys[(gi, j0)], ys[(gi, j1)])
                                       + y_state[gi][:, q * LANES:(q + 1) * LANES] * pick(e_cum[gi]))
            x_end[gi].append((xq[gi] * pick(dt_end[gi])).astype(BF16))
            decays[gi].append(jnp.where(first_row, e_last[gi][:, j0:j0 + 1], e_last[gi][:, j1:j1 + 1]))
    upd = [_dot_tn(bm[gi], jnp.concatenate(x_end[gi], axis=1)) for gi in grp]
    for gi in grp:
        s_scr[gi] = state[gi] * jnp.concatenate(decays[gi], axis=1) + upd[gi]

    @pl.when(c == n_chunks - 1)
    def _():
        sf_ref[...] = s_scr[...]


def _ssd_scan(xbc, dt_raw, dt_bias, a_log, s0, n_seg, inner):
    t = xbc.shape[0]
    seg = t // n_seg
    cl = min(SSM_CHUNK, seg)
    nc = seg // cl
    ep = inner // SSM_GROUPS
    n_e = ep // SSM_HEAD_DIM
    hp = dt_raw.shape[1] // 2
    ng = SSD_GROUPS_PER_STEP
    assert SSM_GROUPS % ng == 0 and (inner // SSM_STATE) % ng == 0
    bc_w = ng * SSM_STATE
    nb = inner // bc_w

    def rows(bi, d, g, c):
        return bi * nc + jnp.where(d == 0, c, nc - 1 - c)

    state = pl.BlockSpec((None, None, ng, SSM_STATE, ep), lambda bi, d, g, c: (bi, d, g, 0, 0))
    par = pl.BlockSpec((None, 1, hp), lambda bi, d, g, c: (d, 0, 0))
    return pl.pallas_call(
        functools.partial(_ssd_scan_kernel, n_chunks=nc, n_e=n_e, n_grp=ng),
        grid=(n_seg, 2, SSM_GROUPS // ng, nc),
        in_specs=[pl.BlockSpec((cl, ng * ep), lambda bi, d, g, c: (rows(bi, d, g, c), g)),
                  pl.BlockSpec((cl, bc_w), lambda bi, d, g, c: (rows(bi, d, g, c), nb + g)),
                  pl.BlockSpec((cl, bc_w), lambda bi, d, g, c: (rows(bi, d, g, c), nb + SSM_GROUPS // ng + g)),
                  pl.BlockSpec((cl, hp), lambda bi, d, g, c: (rows(bi, d, g, c), d)),
                  par, par, state],
        out_specs=[pl.BlockSpec((None, cl, ng * ep), lambda bi, d, g, c: (d, rows(bi, d, g, c), g)), state],
        out_shape=[jax.ShapeDtypeStruct((2, t, inner), F32), jax.ShapeDtypeStruct(s0.shape, F32)],
        scratch_shapes=[pltpu.VMEM((ng, SSM_STATE, ep), F32)],
        compiler_params=_params(("parallel", "parallel", "parallel", "arbitrary")),
        name="ssd_scan",
    )(xbc, xbc, xbc, dt_raw, dt_bias, a_log, s0)


def _ssm_out_kernel(y_ref, xs_ref, z_ref, dsk_ref, nw_ref, o_ref):
    z = z_ref[...]
    y = (y_ref[0] + y_ref[1] + dsk_ref[...] * xs_ref[...]) * (z * jax.nn.sigmoid(z))
    o_ref[...] = _rms(y, nw_ref[...]).astype(o_ref.dtype)


def _ssm_out(y, xbc, z, d_skip_cols, norm_w):
    _, t, inner = y.shape
    gw = inner // SSM_GROUPS
    tr = _pick(t, (256, 128, 64, 32, 16, 8))
    blk = pl.BlockSpec((tr, gw), lambda i, g: (i, g))
    par = pl.BlockSpec((1, gw), lambda i, g: (0, g))
    return pl.pallas_call(
        _ssm_out_kernel,
        grid=(t // tr, SSM_GROUPS),
        in_specs=[pl.BlockSpec((2, tr, gw), lambda i, g: (0, i, g)), blk, blk, par, par],
        out_specs=blk,
        out_shape=jax.ShapeDtypeStruct((t, inner), BF16),
        compiler_params=_params(("parallel", "parallel")),
        name="ssm_out",
    )(y, xbc, z, d_skip_cols, norm_w.reshape(1, inner))


def _odd_mixer(hx, hc, n_seg, w_in, conv_w, conv_b, dt_bias, a_log, d_skip, norm_w, w_out, need_ctx):
    inner = norm_w.shape[0]
    n_heads = d_skip.shape[0]
    conv_dim = conv_w.shape[1]
    hp = -(-n_heads // LANES) * LANES
    w_z = w_in[:, :inner].astype(BF16)
    w_xbc = w_in[:, inner:inner + conv_dim].astype(BF16)
    w_dt = jnp.pad(w_in[:, inner + conv_dim:].reshape(-1, 2, n_heads),
                   ((0, 0), (0, 0), (0, hp - n_heads))).reshape(-1, 2 * hp).astype(BF16)
    pad_h = lambda p: jnp.pad(p, ((0, 0), (0, hp - n_heads))).reshape(2, 1, hp)
    bias_p, alog_p = pad_h(dt_bias), pad_h(a_log)
    d_cols = jnp.repeat(d_skip, SSM_HEAD_DIM).reshape(1, inner)
    w_out_b = w_out.astype(BF16)
    conv_b2 = conv_b.reshape(1, conv_dim)

    def side(h, s0, need_out):
        seg = h.shape[0] // n_seg
        z = _matmul(h, w_z)
        xbc = _row_neighbour_call(_conv_kernel, _matmul(h, w_xbc), seg, [conv_w, conv_b2], "conv_silu")
        dt_raw = _matmul(h, w_dt)
        y, s_fin = _ssd_scan(xbc, dt_raw, bias_p, alog_p, s0, n_seg, inner)
        out = _matmul(_ssm_out(y, xbc, z, d_cols, norm_w), w_out_b) if need_out else None
        return out, s_fin

    s_zero = jnp.zeros((n_seg, 2, SSM_GROUPS, SSM_STATE, inner // SSM_GROUPS), F32)
    oc, s_ctx = side(hc, s_zero, need_ctx)
    ox, _ = side(hx, s_ctx, True)
    return ox, oc


def kernel(x, c, ctx, c_ctx, ada_w, ada_b, norm_g, ev_w_in, ev_shift_mu, rk_w0, rk_w_up, rk_a0, rk_a_up, rk_g_up,
           rk_k_k, rk_k_a, rk_r_k, rk_ln_w, rk_ln_b, na_rpb, ev_w_out, od_w_in, od_conv_w, od_conv_b, od_dt_bias,
           od_a_log, od_d, od_norm_w, od_w_out, router_w, router_b, moe_w1, moe_b1, moe_w2, moe_b2):
    n_b, n_seq, d = x.shape
    n_ctx = ctx.shape[1]
    depth = ada_w.shape[0]
    xt = x.reshape(n_b * n_seq, d)
    ct = ctx.reshape(n_b * n_ctx, d)
    n_cond = -(-(n_b + 1) // SUBLANES) * SUBLANES
    cond = jnp.zeros((n_cond, d), F32).at[:n_b].set(c).at[n_b].set(c_ctx)
    mods = _adaln(cond, ada_w, ada_b)
    hx = hc = None
    for layer in range(depth):
        need_ctx = layer < depth - 1
        i = layer // 2
        mod_x = mods[layer, :n_b].reshape(n_b, 6, d)
        mod_c = jnp.broadcast_to(mods[layer, n_b].reshape(1, 6, d), (n_b, 6, d))
        g = norm_g[layer]
        if layer == 0:
            hx = _norm_mod(xt, g[0], mod_x, 0, 1)
            hc = _norm_mod(ct, g[0], mod_c, 0, 1)
        if layer % 2 == 0:
            ox, oc = _even_mixer(hx, hc, n_b, ev_w_in[i], ev_shift_mu[i], rk_w0[i], rk_w_up[i], rk_a0[i],
                                 rk_a_up[i], rk_g_up[i], rk_k_k[i], rk_k_a[i], rk_r_k[i], rk_ln_w[i], rk_ln_b[i],
                                 na_rpb[i], ev_w_out[i], need_ctx)
        else:
            ox, oc = _odd_mixer(hx, hc, n_b, od_w_in[i], od_conv_w[i], od_conv_b[i], od_dt_bias[i], od_a_log[i],
                                od_d[i], od_norm_w[i], od_w_out[i], need_ctx)
        router = (router_w[layer], router_b[layer])
        experts = _moe_weights(moe_w1[layer], moe_b1[layer], moe_w2[layer], moe_b2[layer])
        streams = [(xt, ox, mod_x)] + ([(ct, oc, mod_c)] if need_ctx else [])
        new = []
        for tok, o, mod in streams:
            tok, h2, top_idx, gate = _residual(tok, o, g[1], mod, 2, g_b=g[2], i_shift=3, i_scale=4, router=router)
            slots = _moe(h2, top_idx, *experts)
            if layer + 1 < depth:
                g_next = norm_g[layer + 1]
                mod_next = (mods[layer + 1, :n_b].reshape(n_b, 6, d) if mod is mod_x else
                            jnp.broadcast_to(mods[layer + 1, n_b].reshape(1, 6, d), (n_b, 6, d)))
                tok, h_next = _residual(tok, slots, g[3], mod, 5, n_sum=TOP_K, g_b=g_next[0], i_shift=0, i_scale=1,
                                        h_dtype=BF16, mod_h=mod_next, slot_gate=gate)
            else:
                (tok,) = _residual(tok, slots, g[3], mod, 5, n_sum=TOP_K, slot_gate=gate)
                h_next = None
            new.append((tok, h_next))
        xt, hx = new[0]
        if need_ctx:
            ct, hc = new[1]
    return xt.reshape(n_b, n_seq, d)
```

```python
import functools
import math

import numpy as np
import jax
import jax.numpy as jnp
from jax import lax
from jax.experimental import pallas as pl
from jax.experimental.pallas import tpu as pltpu

F32 = jnp.float32
BF16 = jnp.bfloat16

RMS_EPS = 1e-6
GN_EPS = 64e-5
HEAD_DIM = 64
GRID_W = 64
WIN_R = 8
WIN_C = 16
SSM_HEAD_DIM = 64
SSM_GROUPS = 8
SSM_STATE = 128
SSM_CHUNK = 128
TOP_K = 4
SWIGLU_ALPHA = 1.702
SWIGLU_LIMIT = 7.0
MOE_ROWS = 256
RWKV_CHUNK = 64
MM_TM = (1024, 512, 256, 128, 64, 32, 16, 8)
MM_TN = (1024, 768, 640, 512, 384, 256, 128)
MM_TK = (4096, 2048, 1024, 512, 256, 128)
SSD_GROUPS_PER_STEP = 4
NA_ROWS_PER_STEP = 16
RWKV_PAIRS_PER_STEP = 16
LANES = 128
SUBLANES = 8
VMEM_LIMIT = 56 * 1024 * 1024
NEG_BIG = -1e30


def _pick(n, cands):
    for c in cands:
        if n % c == 0:
            return c
    return n


def _params(sem, vmem=VMEM_LIMIT):
    return pltpu.CompilerParams(dimension_semantics=sem, vmem_limit_bytes=vmem)


def _split3(x):
    hi = x.astype(BF16)
    r1 = x - hi.astype(F32)
    mid = r1.astype(BF16)
    lo = (r1 - mid.astype(F32)).astype(BF16)
    return hi, mid, lo


def _dot(a, b):
    return jnp.dot(a, b, preferred_element_type=F32)


def _dot_nt(a, b):
    return lax.dot_general(a, b, (((1,), (1,)), ((), ())), preferred_element_type=F32)


def _dot_tn(a, b):
    return lax.dot_general(a, b, (((0,), (0,)), ((), ())), preferred_element_type=F32)


def _dot_exact_rhs(x, m_bf16):
    hi, mid, lo = _split3(x)
    return _dot(hi, m_bf16) + _dot(mid, m_bf16) + _dot(lo, m_bf16)


def _mm_kernel(a_ref, w_ref, b_ref, o_ref, acc_ref, *, nk, pre):
    if nk == 1:
        a = a_ref[...]
        if pre == "silu":
            a = a * jax.nn.sigmoid(a)
        o_ref[...] = (_dot(a.astype(BF16), w_ref[...].astype(BF16)) + b_ref[...]).astype(o_ref.dtype)
        return
    k = pl.program_id(2)

    @pl.when(k == 0)
    def _():
        acc_ref[...] = jnp.zeros_like(acc_ref)

    a = a_ref[...]
    if pre == "silu":
        a = a * jax.nn.sigmoid(a)
    acc_ref[...] += _dot(a.astype(BF16), w_ref[...].astype(BF16))

    @pl.when(k == nk - 1)
    def _():
        o_ref[...] = (acc_ref[...] + b_ref[...]).astype(o_ref.dtype)


def _matmul(a, w, bias=None, out_dtype=F32, pre=None, rows=None):
    start, m = (0, a.shape[0]) if rows is None else rows
    kdim = a.shape[1]
    n = w.shape[1]
    tm = _pick(math.gcd(m, start) if start else m, MM_TM)
    tn = _pick(n, MM_TN)
    tk = _pick(kdim, MM_TK)
    nk = kdim // tk
    first = start // tm
    if bias is None:
        bias = jnp.zeros((n,), F32)
    return pl.pallas_call(
        functools.partial(_mm_kernel, nk=nk, pre=pre),
        grid=(m // tm, n // tn, nk),
        in_specs=[pl.BlockSpec((tm, tk), lambda i, j, k: (first + i, k)),
                  pl.BlockSpec((tk, tn), lambda i, j, k: (k, j)),
                  pl.BlockSpec((1, tn), lambda i, j, k: (0, j))],
        out_specs=pl.BlockSpec((tm, tn), lambda i, j, k: (i, j)),
        out_shape=jax.ShapeDtypeStruct((m, n), out_dtype),
        scratch_shapes=[pltpu.VMEM((tm, tn), F32)],
        compiler_params=_params(("parallel", "parallel", "arbitrary")),
        name="matmul",
    )(a, w, bias.reshape(1, n).astype(F32))


def _adaln_kernel(c_ref, w_ref, b_ref, o_ref, acc_ref, *, nk):
    k = pl.program_id(2)

    @pl.when(k == 0)
    def _():
        acc_ref[...] = jnp.zeros_like(acc_ref)

    c = c_ref[...]
    c = c * jax.nn.sigmoid(c)
    acc_ref[...] += _dot(c.astype(BF16), w_ref[...].astype(BF16))

    @pl.when(k == nk - 1)
    def _():
        o_ref[...] = acc_ref[...] + b_ref[...]


def _adaln(cond, ada_w, ada_b):
    depth, d, n = ada_w.shape
    r = cond.shape[0]
    tn = _pick(n, (2048, 1024, 512, 256, 128))
    tk = _pick(d, (1024, 512, 256, 128))
    nk = d // tk
    return pl.pallas_call(
        functools.partial(_adaln_kernel, nk=nk),
        grid=(depth, n // tn, nk),
        in_specs=[pl.BlockSpec((r, tk), lambda l, j, k: (0, k)),
                  pl.BlockSpec((None, tk, tn), lambda l, j, k: (l, k, j)),
                  pl.BlockSpec((None, 1, tn), lambda l, j, k: (l, 0, j))],
        out_specs=pl.BlockSpec((None, r, tn), lambda l, j, k: (l, 0, j)),
        out_shape=jax.ShapeDtypeStruct((depth, r, n), F32),
        scratch_shapes=[pltpu.VMEM((r, tn), F32)],
        compiler_params=_params(("parallel", "parallel", "arbitrary")),
        name="adaln",
    )(cond, ada_w, ada_b.reshape(depth, 1, n))


def _rms(x, g):
    return x * lax.rsqrt(jnp.mean(x * x, axis=-1, keepdims=True) + RMS_EPS) * g


def _norm_mod_kernel(x_ref, g_ref, mod_ref, o_ref, *, i_shift, i_scale):
    y = _rms(x_ref[...], g_ref[...])
    o = y * (1.0 + mod_ref[i_scale:i_scale + 1, :]) + mod_ref[i_shift:i_shift + 1, :]
    o_ref[...] = o.astype(o_ref.dtype)


def _norm_mod(x, g, mod, i_shift, i_scale, out_dtype=BF16):
    t, d = x.shape
    seg = t // mod.shape[0]
    tr = _pick(seg, (256, 128, 64, 32, 16, 8))
    per = seg // tr
    return pl.pallas_call(
        functools.partial(_norm_mod_kernel, i_shift=i_shift, i_scale=i_scale),
        grid=(t // tr,),
        in_specs=[pl.BlockSpec((tr, d), lambda i: (i, 0)),
                  pl.BlockSpec((1, d), lambda i: (0, 0)),
                  pl.BlockSpec((None, 6, d), lambda i: (i // per, 0, 0))],
        out_specs=pl.BlockSpec((tr, d), lambda i: (i, 0)),
        out_shape=jax.ShapeDtypeStruct((t, d), out_dtype),
        compiler_params=_params(("parallel",)),
        name="norm_mod",
    )(x, g.reshape(1, d), mod)


def _residual_kernel(*refs, n_sum, i_gate, i_shift, i_scale, with_h, with_router, n_exp):
    x_ref = refs[0]
    o_refs = refs[1:1 + n_sum]
    ga_ref, mod_ref = refs[1 + n_sum:3 + n_sum]
    pos = 3 + n_sum
    if n_sum > 1:
        sg_ref = refs[pos]
        pos += 1
    if with_h:
        gb_ref, modh_ref = refs[pos:pos + 2]
        pos += 2
    if with_router:
        rw_ref, rb_ref = refs[pos:pos + 2]
        pos += 2
    xo_ref = refs[pos]
    pos += 1
    if n_sum == 1:
        o = o_refs[0][...]
    else:
        sg = sg_ref[...]
        o = o_refs[0][...] * sg[:, 0:1]
        for j in range(1, n_sum):
            o = o + o_refs[j][...] * sg[:, j:j + 1]
    xn =x_ref[...] + mod_ref[i_gate:i_gate + 1, :] * _rms(o, ga_ref[...])
    xo_ref[...] = xn
    if not with_h:
        return
    h_ref = refs[pos]
    pos += 1
    h = _rms(xn, gb_ref[...]) * (1.0 + modh_ref[i_scale:i_scale + 1, :]) + modh_ref[i_shift:i_shift + 1, :]
    h_ref[...] = h.astype(h_ref.dtype)
    if not with_router:
        return
    idx_ref, gate_ref = refs[pos:pos + 2]
    h1, h2, h3 = _split3(h)
    w1 = rw_ref[0]
    w2 = rw_ref[1]
    logits = (_dot(h1, w1) + _dot(h2, w1) + _dot(h1, w2) + _dot(h3, w1) + _dot(h2, w2)) + rb_ref[...]
    lane = lax.broadcasted_iota(jnp.int32, logits.shape, 1)
    logits = jnp.where(lane < n_exp, logits, NEG_BIG)
    vals, idxs = [], []
    for _ in range(TOP_K):
        m = jnp.max(logits, axis=-1, keepdims=True)
        ix = jnp.min(jnp.where(logits == m, lane, 1 << 30), axis=-1, keepdims=True)
        vals.append(m)
        idxs.append(ix)
        logits = jnp.where(lane == ix, NEG_BIG * 2, logits)
    es = [jnp.exp(v - vals[0]) for v in vals]
    den = es[0]
    for e in es[1:]:
        den = den + e
    idx_out = jnp.zeros(lane.shape, jnp.int32)
    gate_out = jnp.zeros(lane.shape, F32)
    for j in range(TOP_K):
        idx_out = jnp.where(lane == j, idxs[j], idx_out)
        gate_out = jnp.where(lane == j, es[j] / den, gate_out)
    idx_ref[...] = idx_out
    gate_ref[...] = gate_out


def _residual(x, o, g_a, mod, i_gate, n_sum=1, g_b=None, i_shift=0, i_scale=0, h_dtype=F32, router=None,
              mod_h=None, slot_gate=None):
    t, d = x.shape
    seg = t // mod.shape[0]
    tr = _pick(seg, (128, 64, 32, 16, 8))
    per = seg // tr
    nt = t // tr
    with_h = g_b is not None
    with_router = router is not None
    row = lambda i: (i, 0)
    fixed = lambda i: (0, 0)
    ins = [x] + [o] * n_sum + [g_a.reshape(1, d), mod]
    in_specs = ([pl.BlockSpec((tr, d), row)]
                + [pl.BlockSpec((tr, d), functools.partial(lambda j, i: (j * nt + i, 0), j)) for j in range(n_sum)]
                + [pl.BlockSpec((1, d), fixed), pl.BlockSpec((None, 6, d), lambda i: (i // per, 0, 0))])
    outs = [jax.ShapeDtypeStruct((t, d), F32)]
    out_specs = [pl.BlockSpec((tr, d), row)]
    n_exp = 0
    if n_sum > 1:
        ins.append(slot_gate)
        in_specs.append(pl.BlockSpec((tr, slot_gate.shape[1]), row))
    if with_h:
        ins += [g_b.reshape(1, d), mod if mod_h is None else mod_h]
        in_specs += [pl.BlockSpec((1, d), fixed), pl.BlockSpec((None, 6, d), lambda i: (i // per, 0, 0))]
        outs.append(jax.ShapeDtypeStruct((t, d), h_dtype))
        out_specs.append(pl.BlockSpec((tr, d), row))
    if with_router:
        rw, rb = router
        n_exp = rw.shape[1]
        rw = jnp.pad(rw, ((0, 0), (0, LANES - n_exp)))
        rw_hi = rw.astype(BF16)
        rw_lo = (rw - rw_hi.astype(F32)).astype(BF16)
        ins += [jnp.stack([rw_hi, rw_lo]), jnp.pad(rb, (0, LANES - n_exp)).reshape(1, LANES)]
        in_specs += [pl.BlockSpec((2, d, LANES), lambda i: (0, 0, 0)), pl.BlockSpec((1, LANES), fixed)]
        outs += [jax.ShapeDtypeStruct((t, LANES), jnp.int32), jax.ShapeDtypeStruct((t, LANES), F32)]
        out_specs += [pl.BlockSpec((tr, LANES), row), pl.BlockSpec((tr, LANES), row)]
    res = pl.pallas_call(
        functools.partial(_residual_kernel, n_sum=n_sum, i_gate=i_gate, i_shift=i_shift, i_scale=i_scale,
                          with_h=with_h, with_router=with_router, n_exp=n_exp),
        grid=(t // tr,),
        in_specs=in_specs,
        out_specs=out_specs,
        out_shape=outs,
        compiler_params=_params(("parallel",)),
        name="residual",
    )(*ins)
    return res


def _moe_kernel(nb_ref, be_ref, qs_ref, cnt_ref, order_ref, h_hbm, w1_ref, b1_ref, w2_ref, b2_ref, out_hbm,
                xbuf_a, xbuf_b, obuf_a, obuf_b, sem_in, sem_out, *, bm, n_tok, f):
    del be_ref
    i = pl.program_id(0)
    nb = nb_ref[0]
    xbuf = (xbuf_a, xbuf_b)
    obuf = (obuf_a, obuf_b)
    shift = TOP_K.bit_length() - 1
    unroll = 8

    def gather_row(q0, buf, r):
        tok = lax.shift_right_logical(order_ref[q0 + r], shift)
        return pltpu.make_async_copy(h_hbm.at[pl.ds(tok, 1)], xbuf[buf].at[pl.ds(r, 1)], sem_in.at[buf])

    def scatter_row(q0, buf, r):
        a = order_ref[q0 + r]
        slot = (a & (TOP_K - 1)) * n_tok + lax.shift_right_logical(a, shift)
        return pltpu.make_async_copy(obuf[buf].at[pl.ds(r, 1)], out_hbm.at[pl.ds(slot, 1)], sem_out.at[buf])

    def for_rows(fn, n_rows):
        n_full = lax.shift_right_logical(n_rows, unroll.bit_length() - 1)

        def group(gi, c):
            for k in range(unroll):
                fn(gi * unroll + k)
            return c

        def single(r, c):
            fn(r)
            return c

        lax.fori_loop(0, n_full, group, 0)
        lax.fori_loop(n_full * unroll, n_rows, single, 0)

    def wait_gathered(buf):
        pltpu.make_async_copy(h_hbm.at[pl.ds(0, bm)], xbuf[buf], sem_in.at[buf]).wait()

    def wait_scattered(buf, n_rows):
        row = pltpu.make_async_copy(obuf[buf].at[pl.ds(0, 1)], out_hbm.at[pl.ds(0, 1)], sem_out.at[buf])
        for_rows(lambda r: row.wait(), n_rows)

    def step(cur):
        if cur == 0:
            @pl.when(i == 0)
            def _():
                for_rows(lambda r: gather_row(qs_ref[0], 0, r).start(), bm)

        wait_gathered(cur)
        q_next = qs_ref[jnp.minimum(i + 1, nb - 1)]
        for r in range(bm):
            gather_row(q_next, 1 - cur, r).start()
        x = xbuf[cur][...].astype(BF16)
        u = _dot(x, w1_ref[...]) + b1_ref[...]
        glu = jnp.minimum(u[:, :f], SWIGLU_LIMIT)
        lin = jnp.clip(u[:, f:], -SWIGLU_LIMIT, SWIGLU_LIMIT)
        act = glu * jax.nn.sigmoid(SWIGLU_ALPHA * glu) * (lin + 1.0)
        out = _dot(act.astype(BF16), w2_ref[...]) + b2_ref[...]

        @pl.when(i >= 1)
        def _():
            wait_scattered(1 - cur, cnt_ref[jnp.maximum(i - 1, 0)])

        obuf[cur][...] = out
        q_cur = qs_ref[i]
        for_rows(lambda r: scatter_row(q_cur, cur, r).start(), cnt_ref[i])

        @pl.when(i == nb - 1)
        def _():
            wait_scattered(cur, cnt_ref[i])
            wait_gathered(1 - cur)

    for parity in range(2):
        pl.when(jnp.logical_and(i < nb, i % 2 == parity))(functools.partial(step, parity))


def _moe(h, top_idx, w1, b1, w2, b2):
    assert TOP_K & (TOP_K - 1) == 0
    t, d = h.shape
    n_exp, f = w2.shape[0], w2.shape[1]
    bm = MOE_ROWS
    n_assign = t * TOP_K
    flat_e = top_idx[:, :TOP_K].reshape(-1)
    order = jnp.argsort(flat_e, stable=True).astype(jnp.int32)
    counts = jnp.sum(flat_e[:, None] == jnp.arange(n_exp, dtype=flat_e.dtype)[None, :], axis=0, dtype=jnp.int32)
    padded = (counts + bm - 1) // bm * bm
    pad_end = jnp.cumsum(padded)
    pad_start = pad_end - padded
    sort_start = jnp.cumsum(counts) - counts
    n_blocks = -(-(n_assign + n_exp * (bm - 1)) // bm)
    blk_row = jnp.arange(n_blocks, dtype=jnp.int32) * bm
    block_exp = jnp.minimum(jnp.searchsorted(pad_end, blk_row, side="right"), n_exp - 1).astype(jnp.int32)
    off = blk_row - pad_start[block_exp]
    q_start = (sort_start[block_exp] + off).astype(jnp.int32)
    cnt = jnp.clip(counts[block_exp] - off, 0, bm).astype(jnp.int32)
    n_used = (pad_end[-1:] // bm).astype(jnp.int32)
    pre = lambda i, nb, be, qs, ct, od: (be[i], 0, 0)
    grid_spec = pltpu.PrefetchScalarGridSpec(
        num_scalar_prefetch=5,
        grid=(n_blocks,),
        in_specs=[pl.BlockSpec(memory_space=pl.ANY),
                  pl.BlockSpec((None, d, 2 * f), pre),
                  pl.BlockSpec((None, 1, 2 * f), pre),
                  pl.BlockSpec((None, f, d), pre),
                  pl.BlockSpec((None, 1, d), pre)],
        out_specs=pl.BlockSpec(memory_space=pl.ANY),
        scratch_shapes=[pltpu.VMEM((bm, d), F32), pltpu.VMEM((bm, d), F32),
                        pltpu.VMEM((bm, d), F32), pltpu.VMEM((bm, d), F32),
                        pltpu.SemaphoreType.DMA((2,)), pltpu.SemaphoreType.DMA((2,))],
    )
    return pl.pallas_call(
        functools.partial(_moe_kernel, bm=bm, n_tok=t, f=f),
        grid_spec=grid_spec,
        out_shape=jax.ShapeDtypeStruct((n_assign, d), F32),
        compiler_params=_params(("arbitrary",)),
        name="moe_experts",
    )(n_used, block_exp, q_start, cnt, jnp.pad(order, (0, bm)), h, w1, b1, w2, b2)


def _moe_weights(w1_all, layer, b1, w2, b2):
    _, n_exp, d, f2 = w1_all.shape
    perm = np.concatenate([np.arange(0, f2, 2), np.arange(1, f2, 2)])
    pmat = jnp.asarray(np.arange(f2)[:, None] == perm[None, :], BF16)
    w1p = _matmul(w1_all.reshape(-1, f2), pmat, out_dtype=BF16,
                  rows=(layer * n_exp * d, n_exp * d)).reshape(n_exp, d, f2)
    return w1p, b1[:, perm].reshape(n_exp, 1, f2), w2.astype(BF16), b2.reshape(n_exp, 1, -1)


def _row_shift(u, prev8, next8, k):
    n = u.shape[0]
    rolled = pltpu.roll(u, k % n, 0)
    sub = lax.broadcasted_iota(jnp.int32, (SUBLANES, u.shape[1]), 0)
    if k > 0:
        halo = pltpu.roll(prev8, k, 0)
        top = jnp.where(sub < k, halo, rolled[0:SUBLANES])
        return jnp.concatenate([top, rolled[SUBLANES:]], axis=0)
    halo = pltpu.roll(next8, SUBLANES + k, 0)
    bot = jnp.where(sub >= SUBLANES + k, halo, rolled[n - SUBLANES:])
    return jnp.concatenate([rolled[:n - SUBLANES], bot], axis=0)


def _halo_tiles(prev_ref, next_ref, per):
    j = pl.program_id(0) % per
    prev8 = jnp.where(j == 0, 0.0, prev_ref[...])
    next8 = jnp.where(j == per - 1, 0.0, next_ref[...])
    return prev8, next8


def _conv_kernel(u_ref, prev_ref, next_ref, w_ref, b_ref, o_ref, *, per):
    u = u_ref[...]
    prev8, next8 = _halo_tiles(prev_ref, next_ref, per)
    acc = (w_ref[0:1, :] * _row_shift(u, prev8, next8, 2) + w_ref[1:2, :] * _row_shift(u, prev8, next8, 1)
           + w_ref[2:3, :] * u + w_ref[3:4, :] * _row_shift(u, prev8, next8, -1)) + b_ref[...]
    o_ref[...] = acc * jax.nn.sigmoid(acc)


def _row_neighbour_call(kernel_fn, u, seg, extra, name):
    t, n = u.shape
    tr = _pick(seg, (256, 128, 64, 32, 16, 8))
    tc = _pick(n, (2048, 1024, 768, 512, 384, 256, 128))
    per = seg // tr
    r8 = tr // SUBLANES
    last8 = t // SUBLANES - 1
    in_specs = [pl.BlockSpec((tr, tc), lambda i, j: (i, j)),
                pl.BlockSpec((SUBLANES, tc), lambda i, j: (jnp.maximum(i * r8 - 1, 0), j)),
                pl.BlockSpec((SUBLANES, tc), lambda i, j: (jnp.minimum((i + 1) * r8, last8), j))]
    for e in extra:
        in_specs.append(pl.BlockSpec((e.shape[0], tc), lambda i, j: (0, j)))
    return pl.pallas_call(
        functools.partial(kernel_fn, per=per),
        grid=(t // tr, n // tc),
        in_specs=in_specs,
        out_specs=pl.BlockSpec((tr, tc), lambda i, j: (i, j)),
        out_shape=jax.ShapeDtypeStruct((t, n), F32),
        compiler_params=_params(("parallel", "parallel")),
        name=name,
    )(u, u, u, *extra)


def _head_sum_matrix(n):
    idx = np.arange(n) // HEAD_DIM
    return jnp.asarray(idx[:, None] == idx[None, :], BF16)


def _centred_shift(u_ref, prev_ref, next_ref, mu_ref, per):
    u = u_ref[...]
    prev8, next8 = _halo_tiles(prev_ref, next_ref, per)
    nb = 0.5 * (_row_shift(u, prev8, next8, 1) + _row_shift(u, prev8, next8, -1))
    return u + mu_ref[...] * (nb - u)


def _rwkv_prep_kernel(*refs, wd_w, ad_w, per):
    r, k, v, lo = [_centred_shift(*refs[4 * n:4 * n + 4], per) for n in range(4)]
    (w0_ref, wup_ref, a0_ref, aup_ref, gup_ref, kk_ref_, ka_ref, rk_ref, hs_ref,
     r_o, v_o, kk_o, lw_o, kd_o, b_o, g_o, bv_o) = refs[16:]
    wd =jnp.tanh(lo[:, 0:wd_w]).astype(BF16)
    ad = lo[:, wd_w:wd_w + ad_w].astype(BF16)
    gs = jax.nn.sigmoid(lo[:, wd_w + ad_w:]).astype(BF16)
    hs = hs_ref[...]
    kkr = k * kk_ref_[...]
    ss = _dot_exact_rhs(kkr * kkr, hs)
    kk = kkr * lax.rsqrt(jnp.maximum(ss, 1e-24))
    ksum = jnp.zeros_like(k)
    for d in range(2):
        z = -(w0_ref[d:d + 1, :] + _dot(wd, wup_ref[d]))
        w_log = -(jnp.maximum(z, 0.0) + jnp.log(1.0 + jnp.exp(-jnp.abs(z)))) - 0.5
        lw_o[d] = -jnp.exp(w_log)
        asig = jax.nn.sigmoid(a0_ref[d:d + 1, :] + _dot(ad, aup_ref[d]))
        kd = k * (1.0 + (asig - 1.0) * ka_ref[...])
        kd_o[d] = kd
        b_o[d] = kk * asig
        ksum = ksum + kd
    bonus = _dot_exact_rhs(r * ksum * rk_ref[...], hs)
    r_o[...] = r
    v_o[...] = v
    kk_o[...] = kk
    g_o[...] = _dot(gs, gup_ref[...])
    bv_o[...] = bonus * v


def _rwkv_prep(p_rkv, p_lora, mu_rkv, mu_lora, p, seg):
    t = p_rkv.shape[0]
    a = p_rkv.shape[1] // 3
    nl = p_lora.shape[1]
    tr = _pick(seg, (256, 128, 64, 32, 16, 8))
    tc = _pick(a, (512, 256, 128))
    nj = a // tc
    per = seg // tr
    r8 = tr // SUBLANES
    last8 = t // SUBLANES - 1

    def shifted_specs(width, colmap):
        return [pl.BlockSpec((tr, width), lambda i, j: (i, colmap(j))),
                pl.BlockSpec((SUBLANES, width), lambda i, j: (jnp.maximum(i * r8 - 1, 0), colmap(j))),
                pl.BlockSpec((SUBLANES, width), lambda i, j: (jnp.minimum((i + 1) * r8, last8), colmap(j))),
                pl.BlockSpec((1, width), lambda i, j: (0, colmap(j)))]

    shifted_in = []
    shifted_args = []
    for off in range(3):
        shifted_in += shifted_specs(tc, functools.partial(lambda o, j: o * nj + j, off))
        shifted_args += [p_rkv, p_rkv, p_rkv, mu_rkv]
    shifted_in += shifted_specs(nl, lambda j: 0)
    shifted_args += [p_lora, p_lora, p_lora, mu_lora]
    par = lambda rows: pl.BlockSpec((rows, tc), lambda i, j: (0, j))
    par3 = lambda rows: pl.BlockSpec((2, rows, tc), lambda i, j: (0, 0, j))
    one = jax.ShapeDtypeStruct((t, a), F32)
    two = jax.ShapeDtypeStruct((2, t, a), F32)
    o1 = pl.BlockSpec((tr, tc), lambda i, j: (i, j))
    o2 = pl.BlockSpec((2, tr, tc), lambda i, j: (0, i, j))
    return pl.pallas_call(
        functools.partial(_rwkv_prep_kernel, wd_w=p["wd_w"], ad_w=p["ad_w"], per=per),
        grid=(t // tr, nj),
        in_specs=shifted_in + [par(2), par3(p["wd_w"]), par(2), par3(p["ad_w"]), par(p["g_up"].shape[0]),
                               par(1), par(1), par(1),
                               pl.BlockSpec((tc, tc), lambda i, j: (0, 0))],
        out_specs=[o1, o1, o1, o2, o2, o2, o1, o1],
        out_shape=[one, one, one, two, two, two, one, one],
        compiler_params=_params(("parallel", "parallel")),
        name="rwkv_prep",
    )(*shifted_args, p["w0"], p["w_up"], p["a0"], p["a_up"], p["g_up"], p["k_k"], p["k_a"],
      p["r_k"], _head_sum_matrix(tc))


def _rwkv_chunk_pairs(ins, states, d, masks):
    incl, strict, tri, eye, first, first2, blockdiag = masks
    cl = RWKV_CHUNK
    pairs = range(len(ins))
    heads = [(p, hh) for p in pairs for hh in range(2)]
    splits = [_split3(ins[p][3]) for p in pairs]
    cum = [_dot(tri, h1) + _dot(tri, h2) + _dot(tri, h3) for h1, h2, h3 in splits]
    cum_last, ar, ar_b, bk, bk_end, sb, vb = [], [], [], [], [], [], []
    for p in pairs:
        r, v, kk, lw, kd, b = ins[p]
        cl_p = jnp.where(d == 0, cum[p][cl - 1:cl, :], cum[p][0:1, :])
        ginv = jnp.exp(-cum[p])
        to_end = jnp.exp(cl_p - cum[p])
        at = -kk * jnp.exp(cum[p] - lw)
        rt = r * jnp.exp(cum[p])
        cum_last.append(cl_p)
        bk.append(jnp.concatenate([b * ginv, kd * ginv], axis=0).astype(BF16))
        bk_end.append(jnp.concatenate([b * to_end, kd * to_end], axis=0).astype(BF16))
        ar.append(jnp.concatenate([at, rt], axis=0))
        ar_b.append(ar[p].astype(BF16))
        sb.append(states[p].astype(BF16))
        vb.append(v.astype(BF16))
    ar_h = {(p, hh): jnp.where(first2 if hh == 0 else jnp.logical_not(first2), ar[p], 0.0).astype(BF16)
            for p, hh in heads}
    m = {k: _dot_nt(ar_h[k], bk[k[0]]) for k in heads}
    xs = [_dot_nt(ar_b[p], sb[p]) for p in pairs]
    a_ab = {k: jnp.where(strict, m[k][:cl, :cl], 0.0) for k in heads}
    a_ak = {k: jnp.where(strict, m[k][:cl, cl:], 0.0).astype(BF16) for k in heads}
    rbk = {k: jnp.concatenate([jnp.where(incl, m[k][cl:, :cl], 0.0), jnp.where(incl, m[k][cl:, cl:], 0.0)],
                              axis=1).astype(BF16) for k in heads}
    rhs = {k: (xs[k[0]][:cl] + _dot(a_ak[k], vb[k[0]])).astype(BF16) for k in heads}
    tinv = {k: eye + a_ab[k] for k in heads}
    pw = {k: a_ab[k].astype(BF16) for k in heads}
    for _ in range(int(math.log2(cl)) - 1):
        pw = {k: _dot(pw[k], pw[k]).astype(BF16) for k in heads}
        tinv = {k: tinv[k] + _dot(tinv[k].astype(BF16), pw[k]) for k in heads}
    u_h = {k: _dot(tinv[k].astype(BF16), rhs[k]) for k in heads}
    uv = [jnp.concatenate([jnp.where(first, u_h[(p, 0)], u_h[(p, 1)]), ins[p][1]], axis=0).astype(BF16)
          for p in pairs]
    y_h = {k: _dot(rbk[k], uv[k[0]]) for k in heads}
    upd = [_dot_tn(uv[p], bk_end[p]) for p in pairs]
    out = []
    for p in pairs:
        y = xs[p][cl:] + jnp.where(first, y_h[(p, 0)], y_h[(p, 1)])
        s_new = states[p] * jnp.exp(cum_last[p]) + jnp.where(blockdiag, upd[p], 0.0)
        out.append((y, s_new))
    return out


def _rwkv_scan_kernel(r_ref, v_ref, kk_ref, lw_ref, kd_ref, b_ref, s0_ref, y_ref, sf_ref, s_scr, *, n_chunks,
                      n_pairs):
    d = pl.program_id(1)
    c = pl.program_id(3)
    cl = RWKV_CHUNK

    @pl.when(c == 0)
    def _():
        s_scr[...] = s0_ref[...]

    sgn = 1 - 2 * d
    row = lax.broadcasted_iota(jnp.int32, (cl, cl), 0)
    col = lax.broadcasted_iota(jnp.int32, (cl, cl), 1)
    order = (row - col) * sgn
    incl = order >= 0
    strict = order > 0
    tri = jnp.where(incl, 1.0, 0.0).astype(BF16)
    eye = jnp.where(row == col, 1.0, 0.0)
    first = lax.broadcasted_iota(jnp.int32, (cl, LANES), 1) < HEAD_DIM
    first2 = lax.broadcasted_iota(jnp.int32, (2 * cl, LANES), 1) < HEAD_DIM
    rr = lax.broadcasted_iota(jnp.int32, (LANES, LANES), 0) // HEAD_DIM
    cc = lax.broadcasted_iota(jnp.int32, (LANES, LANES), 1) // HEAD_DIM
    masks = (incl, strict, tri, eye, first, first2, rr == cc)
    ins = []
    for p in range(n_pairs):
        sl = slice(p * LANES, (p + 1) * LANES)
        ins.append((r_ref[:, sl], v_ref[:, sl], kk_ref[:, sl], lw_ref[:, sl], kd_ref[:, sl], b_ref[:, sl]))
    results = _rwkv_chunk_pairs(ins, [s_scr[p] for p in range(n_pairs)], d, masks)
    for p, (y, s_new) in enumerate(results):
        y_ref[:, p * LANES:(p + 1) * LANES] = y
        s_scr[p] = s_new

    @pl.when(c == n_chunks - 1)
    def _():
        sf_ref[...] = s_scr[...]


def _rwkv_scan(r, v, kk, lw, kd, b, s0, n_seg):
    t, a = r.shape
    seg = t // n_seg
    cl = RWKV_CHUNK
    nc = seg // cl
    npair = a // LANES
    pb = _pick(npair, (RWKV_PAIRS_PER_STEP, 4, 2, 1))
    w = pb * LANES

    def rows(bi, d, p, c):
        return bi * nc + jnp.where(d == 0, c, nc - 1 - c)

    shared = pl.BlockSpec((cl, w), lambda bi, d, p, c: (rows(bi, d, p, c), p))
    per_dir = pl.BlockSpec((None, cl, w), lambda bi, d, p, c: (d, rows(bi, d, p, c), p))
    state = pl.BlockSpec((None, None, pb, LANES, LANES), lambda bi, d, p, c: (bi, d, p, 0, 0))
    return pl.pallas_call(
        functools.partial(_rwkv_scan_kernel, n_chunks=nc, n_pairs=pb),
        grid=(n_seg, 2, npair // pb, nc),
        in_specs=[shared, shared, shared, per_dir, per_dir, per_dir, state],
        out_specs=[per_dir, state],
        out_shape=[jax.ShapeDtypeStruct((2, t, a), F32), jax.ShapeDtypeStruct(s0.shape, F32)],
        scratch_shapes=[pltpu.VMEM((pb, LANES, LANES), F32)],
        compiler_params=_params(("parallel", "parallel", "parallel", "arbitrary")),
        name="rwkv_scan",
    )(r, v, kk, lw, kd, b, s0)


def _rwkv_out_kernel(y_ref, g_ref, bv_ref, lnw_ref, lnb_ref, hs_ref, o_ref):
    y = y_ref[0] + y_ref[1]
    hs = hs_ref[...]
    inv = 1.0 / HEAD_DIM
    mu = _dot_exact_rhs(y, hs) * inv
    yc = y - mu
    var = _dot_exact_rhs(yc * yc, hs) * inv
    yn = yc * lax.rsqrt(var + GN_EPS) * lnw_ref[...] + lnb_ref[...]
    o_ref[...] = ((yn + bv_ref[...]) * g_ref[...]).astype(o_ref.dtype)


def _rwkv_out(y, g, bv, ln_w, ln_b):
    _, t, a = y.shape
    tr = _pick(t, (256, 128, 64, 32, 16, 8))
    tc = _pick(a, (512, 256, 128))
    o1 = pl.BlockSpec((tr, tc), lambda i, j: (i, j))
    par = pl.BlockSpec((1, tc), lambda i, j: (0, j))
    return pl.pallas_call(
        _rwkv_out_kernel,
        grid=(t // tr, a // tc),
        in_specs=[pl.BlockSpec((2, tr, tc), lambda i, j: (0, i, j)), o1, o1, par, par,
                  pl.BlockSpec((tc, tc), lambda i, j: (0, 0))],
        out_specs=o1,
        out_shape=jax.ShapeDtypeStruct((t, a), BF16),
        compiler_params=_params(("parallel", "parallel")),
        name="rwkv_out",
    )(y, g, bv, ln_w.reshape(1, a), ln_b.reshape(1, a), _head_sum_matrix(tc))


def _rwkv_params(shift_mu, w0, w_up, a0, a_up, g_up, k_k, k_a, r_k, a_width):
    dr, ar_, gr = w_up.shape[1], a_up.shape[1], g_up.shape[0]
    pad = lambda n: -(-n // LANES) * LANES
    wd_w, ad_w, gd_w = pad(2 * dr), pad(2 * ar_), pad(gr)

    def up(wu, rank, width):
        out = jnp.zeros((2, width, a_width), F32)
        for d in range(2):
            out = out.at[d, d * rank:(d + 1) * rank].set(wu[d])
        return out.astype(BF16)

    return dict(wd_w=wd_w, ad_w=ad_w, gd_w=gd_w, dr=dr, ar=ar_, gr=gr,
                w0=w0, a0=a0, w_up=up(w_up, dr, wd_w), a_up=up(a_up, ar_, ad_w),
                g_up=jnp.pad(g_up, ((0, gd_w - gr), (0, 0))).astype(BF16),
                k_k=k_k.reshape(1, -1), k_a=k_a.reshape(1, -1), r_k=r_k.reshape(1, -1))


def _pad_lora_cols(w, dr2, ar2, gr, p):
    parts = [(w[..., :dr2], p["wd_w"]), (w[..., dr2:dr2 + ar2], p["ad_w"]), (w[..., dr2 + ar2:], p["gd_w"])]
    return jnp.concatenate([jnp.pad(x, [(0, 0)] * (x.ndim - 1) + [(0, wd - x.shape[-1])]) for x, wd in parts],
                           axis=-1)


def _na_bias(rpb, rows):
    kr = min(WIN_R, rows)
    n_heads, n_ro, n_off = rpb.shape
    assert LANES % GRID_W == 0 and (kr * GRID_W) % LANES == 0
    return pl.pallas_call(
        functools.partial(_na_bias_kernel, kr=kr, n_ro=n_ro, n_off=n_off),
        grid=(n_heads, kr),
        in_specs=[pl.BlockSpec(memory_space=pltpu.SMEM)],
        out_specs=pl.BlockSpec((None, None, GRID_W, kr * GRID_W), lambda h, p: (h, p, 0, 0)),
        out_shape=jax.ShapeDtypeStruct((n_heads, kr, GRID_W, kr * GRID_W), F32),
        compiler_params=_params(("parallel", "parallel")),
        name="na_bias_table",
    )(rpb.reshape(-1).astype(F32))


def _na_bias_kernel(rpb_ref, o_ref, *, kr, n_ro, n_off):
    h = pl.program_id(0)
    pat = pl.program_id(1)
    w = GRID_W
    per_tile = LANES // w
    q = lax.broadcasted_iota(jnp.int32, (w, LANES), 0)
    lane = lax.broadcasted_iota(jnp.int32, (w, LANES), 1)
    i_local = lane // w
    c = lane - i_local * w
    off = c - q + (WIN_C - 1)
    start = jnp.clip(q - WIN_C // 2, 0, w - WIN_C)
    inside = jnp.logical_and(c >= start, c < start + WIN_C)
    for t in range(kr // per_tile):
        base = [(h * n_ro + (t * per_tile + k - pat + WIN_R - 1)) * n_off for k in range(per_tile)]
        val = jnp.zeros((w, LANES), F32)
        for u in range(n_off):
            s = rpb_ref[base[per_tile - 1] + u]
            for k in range(per_tile - 2, -1, -1):
                s = jnp.where(i_local == k, rpb_ref[base[k] + u], s)
            val = jnp.where(off == u, s, val)
        o_ref[:, t * LANES:(t + 1) * LANES] = jnp.where(inside, val, NEG_BIG)


def _softmax_pv(s_list, v_list):
    m = s_list[0].max(axis=-1, keepdims=True)
    for s in s_list[1:]:
        m = jnp.maximum(m, s.max(axis=-1, keepdims=True))
    den = 0.0
    acc = 0.0
    for s, vv in zip(s_list, v_list):
        p = jnp.exp(s - m)
        den = den + p.sum(axis=-1, keepdims=True)
        acc = acc + _dot(p.astype(BF16), vv)
    return acc / den


def _na_kernel(q_ref, k_ref, v_ref, kc_ref, vc_ref, bias_ref, o_ref, *, rows, kr):
    w = GRID_W
    lane = lax.broadcasted_iota(jnp.int32, (w, LANES), 1)
    first = lane < HEAD_DIM
    kc = kc_ref[...]
    vc = vc_ref[...]
    scale = HEAD_DIM ** -0.5

    rb = NA_ROWS_PER_STEP if rows % NA_ROWS_PER_STEP == 0 else 1

    def body(it, carry):
        rr = [it * rb + k for k in range(rb)]
        r0 = [jnp.clip(r - kr // 2, 0, rows - kr) for r in rr]
        q = [(q_ref[pl.ds(pl.multiple_of(r * w, w), w), :].astype(F32) * scale).astype(BF16) for r in rr]
        kw = [k_ref[pl.ds(pl.multiple_of(r * w, w), kr * w), :] for r in r0]
        vw = [v_ref[pl.ds(pl.multiple_of(r * w, w), kr * w), :] for r in r0]
        chains = [(k, hh) for k in range(rb) for hh in range(2)]
        qm = {(k, hh): jnp.where(first if hh == 0 else jnp.logical_not(first), q[k], jnp.zeros_like(q[k]))
              for k, hh in chains}
        s_loc = {c: _dot_nt(qm[c], kw[c[0]]) + bias_ref[c[1], pl.ds(rr[c[0]] - r0[c[0]], 1)][0] for c in chains}
        s_ctx = {c: _dot_nt(qm[c], kc) for c in chains}
        mx = {c: jnp.maximum(s_loc[c].max(axis=-1, keepdims=True), s_ctx[c].max(axis=-1, keepdims=True))
              for c in chains}
        p_loc = {c: jnp.exp(s_loc[c] - mx[c]) for c in chains}
        p_ctx = {c: jnp.exp(s_ctx[c] - mx[c]) for c in chains}
        den = {c: p_loc[c].sum(axis=-1, keepdims=True) + p_ctx[c].sum(axis=-1, keepdims=True) for c in chains}
        acc = {c: _dot(p_loc[c].astype(BF16), vw[c[0]]) + _dot(p_ctx[c].astype(BF16), vc) for c in chains}
        for k in range(rb):
            out = jnp.where(first, acc[(k, 0)] / den[(k, 0)], acc[(k, 1)] / den[(k, 1)])
            o_ref[pl.ds(pl.multiple_of(rr[k] * w, w), w), :] = out.astype(o_ref.dtype)
        return carry

    lax.fori_loop(0, rows // rb, body, 0)


def _ctx_attn_kernel(q_ref, k_ref, v_ref, o_ref):
    n = q_ref.shape[0]
    lane = lax.broadcasted_iota(jnp.int32, (n, LANES), 1)
    first = lane < HEAD_DIM
    q = (q_ref[...].astype(F32) * HEAD_DIM ** -0.5).astype(BF16)
    k = k_ref[...]
    v = v_ref[...]
    outs = []
    for hh in range(2):
        mh = first if hh == 0 else jnp.logical_not(first)
        qm = jnp.where(mh, q, jnp.zeros_like(q))
        outs.append(_softmax_pv([_dot_nt(qm, k)], [v]))
    o_ref[...] = jnp.where(first, outs[0], outs[1]).astype(o_ref.dtype)


def _attention(qkv_x, qkv_c, rpb, n_seg, need_ctx):
    tx, w3 = qkv_x.shape
    bw = w3 // 3
    npair = bw // LANES
    lx = tx // n_seg
    lc = qkv_c.shape[0] // n_seg
    rows = lx // GRID_W
    kr = min(WIN_R, rows)
    bias = _na_bias(rpb, rows)
    blk = lambda length, off: pl.BlockSpec((length, LANES), lambda bi, p: (bi, off * npair + p))
    nx = pl.pallas_call(
        functools.partial(_na_kernel, rows=rows, kr=kr),
        grid=(n_seg, npair),
        in_specs=[blk(lx, 0), blk(lx, 1), blk(lx, 2), blk(lc, 1), blk(lc, 2),
                  pl.BlockSpec((2, kr, GRID_W, kr * GRID_W), lambda bi, p: (p, 0, 0, 0))],
        out_specs=pl.BlockSpec((lx, LANES), lambda bi, p: (bi, p)),
        out_shape=jax.ShapeDtypeStruct((tx, bw), BF16),
        compiler_params=_params(("parallel", "parallel")),
        name="neighbourhood_attention",
    )(qkv_x, qkv_x, qkv_x, qkv_c, qkv_c, bias)
    ncx = None
    if need_ctx:
        ncx = pl.pallas_call(
            _ctx_attn_kernel,
            grid=(n_seg, npair),
            in_specs=[blk(lc, 0), blk(lc, 1), blk(lc, 2)],
            out_specs=pl.BlockSpec((lc, LANES), lambda bi, p: (bi, p)),
            out_shape=jax.ShapeDtypeStruct((qkv_c.shape[0], bw), BF16),
            compiler_params=_params(("parallel", "parallel")),
            name="context_attention",
        )(qkv_c, qkv_c, qkv_c)
    return nx, ncx


def _even_mixer(hx, hc, n_seg, w_in, shift_mu, w0, w_up, a0, a_up, g_up, k_k, k_a, r_k, ln_w, ln_b, rpb, w_out,
                need_ctx):
    d = hx.shape[1]
    a_width = k_k.shape[0]
    p = _rwkv_params(shift_mu, w0, w_up, a0, a_up, g_up, k_k, k_a, r_k, a_width)
    dr2, ar2, gr = 2 * p["dr"], 2 * p["ar"], p["gr"]
    a_cols = 3 * a_width + dr2 + ar2 + gr
    w_rkv = w_in[:, :3 * a_width].astype(BF16)
    w_lora = _pad_lora_cols(w_in[:, 3 * a_width:a_cols], dr2, ar2, gr, p).astype(BF16)
    w_qkv = w_in[:, a_cols:].astype(BF16)
    mu_rkv = shift_mu[:3 * a_width].reshape(1, -1)
    mu_lora = _pad_lora_cols(shift_mu[3 * a_width:], dr2, ar2, gr, p).reshape(1, -1)
    w_out_b = w_out.astype(BF16)

    def rwkv_side(h, s0):
        seg = h.shape[0] // n_seg
        r, v, kk, lw, kd, b, g, bv = _rwkv_prep(_matmul(h, w_rkv), _matmul(h, w_lora), mu_rkv, mu_lora, p, seg)
        y, s_fin = _rwkv_scan(r, v, kk, lw, kd, b, s0, n_seg)
        return (y, g, bv), s_fin

    s_zero = jnp.zeros((n_seg, 2, a_width // LANES, LANES, LANES), F32)
    terms_c, s_ctx = rwkv_side(hc, s_zero)
    terms_x, _ = rwkv_side(hx, s_ctx)
    qkv_x = _matmul(hx, w_qkv, out_dtype=BF16)
    qkv_c = _matmul(hc, w_qkv, out_dtype=BF16)
    nx, ncx = _attention(qkv_x, qkv_c, rpb, n_seg, need_ctx)
    rx = _rwkv_out(*terms_x, ln_w, ln_b)
    ox = _matmul(jnp.concatenate([rx, nx], axis=1), w_out_b)
    oc = None
    if need_ctx:
        rc = _rwkv_out(*terms_c, ln_w, ln_b)
        oc = _matmul(jnp.concatenate([rc, ncx], axis=1), w_out_b)
    return ox, oc


def _softplus(x):
    return jnp.maximum(x, 0.0) + jnp.log(1.0 + jnp.exp(-jnp.abs(x)))


def _ssd_scan_kernel(x_ref, b_ref, c_ref, dt_ref, bias_ref, alog_ref, s0_ref, y_ref, sf_ref, s_scr, *,
                     n_chunks, n_e, n_grp):
    d = pl.program_id(1)
    gb = pl.program_id(2)
    c = pl.program_id(3)
    cl = x_ref.shape[0]
    hp = dt_ref.shape[1]
    ep = n_e * SSM_HEAD_DIM
    grp = range(n_grp)

    @pl.when(c == 0)
    def _():
        s_scr[...] = s0_ref[...]

    sgn = 1 - 2 * d
    row = lax.broadcasted_iota(jnp.int32, (cl, cl), 0)
    col = lax.broadcasted_iota(jnp.int32, (cl, cl), 1)
    incl = (row - col) * sgn >= 0
    tri = jnp.where(incl, 1.0, 0.0).astype(BF16)
    dt_all = _softplus(dt_ref[...] + bias_ref[...])
    dta_all = dt_all * (-jnp.exp(alog_ref[...]))
    dt_parts = _split3(dt_all)
    dta_parts = _split3(dta_all)
    hr = lax.broadcasted_iota(jnp.int32, (hp, LANES), 0)
    hc = lax.broadcasted_iota(jnp.int32, (hp, LANES), 1)
    sel = [jnp.where(jnp.logical_and(hr == (gb * n_grp + gi) * n_e + hc, hc < n_e), 1.0, 0.0).astype(BF16)
           for gi in grp]
    dt_g = [sum(_dot(p, sel[gi]) for p in dt_parts) for gi in grp]
    dta_g = [sum(_dot(p, sel[gi]) for p in dta_parts) for gi in grp]
    dta_split = [_split3(dta_g[gi]) for gi in grp]
    cum = [sum(_dot(tri, p) for p in dta_split[gi]) for gi in grp]
    bm = [b_ref[:, gi * SSM_STATE:(gi + 1) * SSM_STATE].astype(BF16) for gi in grp]
    cm = [c_ref[:, gi * SSM_STATE:(gi + 1) * SSM_STATE].astype(BF16) for gi in grp]
    cb = [_dot_nt(cm[gi], bm[gi]) for gi in grp]
    state = [s_scr[gi] for gi in grp]
    y_state = [_dot(cm[gi], state[gi].astype(BF16)) for gi in grp]
    cum_t = [cum[gi].T for gi in grp]
    dt_t = [dt_g[gi].T for gi in grp]
    cum_last = [jnp.where(d == 0, cum[gi][cl - 1:cl, :], cum[gi][0:1, :]) for gi in grp]
    e_cum = [jnp.exp(cum[gi]) for gi in grp]
    dt_end = [dt_g[gi] * jnp.exp(cum_last[gi] - cum[gi]) for gi in grp]
    e_last = [jnp.exp(cum_last[gi]) for gi in grp]
    first = lax.broadcasted_iota(jnp.int32, (cl, LANES), 1) < SSM_HEAD_DIM
    first_row = first[0:1, :]
    x_end = [[] for _ in grp]
    decays = [[] for _ in grp]
    for q in range(n_e // 2):
        j0, j1 = 2 * q, 2 * q + 1
        pick = lambda t: jnp.where(first, t[:, j0:j0 + 1], t[:, j1:j1 + 1])
        xq = [x_ref[:, gi * ep + q * LANES:gi * ep + (q + 1) * LANES] for gi in grp]
        x_b = [xq[gi].astype(BF16) for gi in grp]
        m = {(gi, j): (cb[gi] * jnp.exp(jnp.where(incl, cum[gi][:, j:j + 1] - cum_t[gi][j:j + 1, :], NEG_BIG))
                       * dt_t[gi][j:j + 1, :]).astype(BF16) for gi in grp for j in (j0, j1)}
        ys = {k: _dot(m[k], x_b[k[0]]) for k in m}
        for gi in grp:
            lo = gi * ep + q * LANES
            y_ref[:, lo:lo + LANES] = (jnp.where(first, ys[(gi, j0)], ys[(gi, j1)])
                                       + y_state[gi][:, q * LANES:(q + 1) * LANES] * pick(e_cum[gi]))
            x_end[gi].append((xq[gi] * pick(dt_end[gi])).astype(BF16))
            decays[gi].append(jnp.where(first_row, e_last[gi][:, j0:j0 + 1], e_last[gi][:, j1:j1 + 1]))
    upd = [_dot_tn(bm[gi], jnp.concatenate(x_end[gi], axis=1)) for gi in grp]
    for gi in grp:
        s_scr[gi] = state[gi] * jnp.concatenate(decays[gi], axis=1) + upd[gi]

    @pl.when(c == n_chunks - 1)
    def _():
        sf_ref[...] = s_scr[...]


def _ssd_scan(xbc, dt_raw, dt_bias, a_log, s0, n_seg, inner):
    t = xbc.shape[0]
    seg = t // n_seg
    cl = min(SSM_CHUNK, seg)
    nc = seg // cl
    ep = inner // SSM_GROUPS
    n_e = ep // SSM_HEAD_DIM
    hp = dt_raw.shape[1] // 2
    ng = SSD_GROUPS_PER_STEP
    assert SSM_GROUPS % ng == 0 and (inner // SSM_STATE) % ng == 0
    bc_w = ng * SSM_STATE
    nb = inner // bc_w

    def rows(bi, d, g, c):
        return bi * nc + jnp.where(d == 0, c, nc - 1 - c)

    state = pl.BlockSpec((None, None, ng, SSM_STATE, ep), lambda bi, d, g, c: (bi, d, g, 0, 0))
    par = pl.BlockSpec((None, 1, hp), lambda bi, d, g, c: (d, 0, 0))
    return pl.pallas_call(
        functools.partial(_ssd_scan_kernel, n_chunks=nc, n_e=n_e, n_grp=ng),
        grid=(n_seg, 2, SSM_GROUPS // ng, nc),
        in_specs=[pl.BlockSpec((cl, ng * ep), lambda bi, d, g, c: (rows(bi, d, g, c), g)),
                  pl.BlockSpec((cl, bc_w), lambda bi, d, g, c: (rows(bi, d, g, c), nb + g)),
                  pl.BlockSpec((cl, bc_w), lambda bi, d, g, c: (rows(bi, d, g, c), nb + SSM_GROUPS // ng + g)),
                  pl.BlockSpec((cl, hp), lambda bi, d, g, c: (rows(bi, d, g, c), d)),
                  par, par, state],
        out_specs=[pl.BlockSpec((None, cl, ng * ep), lambda bi, d, g, c: (d, rows(bi, d, g, c), g)), state],
        out_shape=[jax.ShapeDtypeStruct((2, t, inner), F32), jax.ShapeDtypeStruct(s0.shape, F32)],
        scratch_shapes=[pltpu.VMEM((ng, SSM_STATE, ep), F32)],
        compiler_params=_params(("parallel", "parallel", "parallel", "arbitrary")),
        name="ssd_scan",
    )(xbc, xbc, xbc, dt_raw, dt_bias, a_log, s0)


def _ssm_out_kernel(y_ref, xs_ref, z_ref, dsk_ref, nw_ref, o_ref):
    z = z_ref[...]
    y = (y_ref[0] + y_ref[1] + dsk_ref[...] * xs_ref[...]) * (z * jax.nn.sigmoid(z))
    o_ref[...] = _rms(y, nw_ref[...]).astype(o_ref.dtype)


def _ssm_out(y, xbc, z, d_skip_cols, norm_w):
    _, t, inner = y.shape
    gw = inner // SSM_GROUPS
    tr = _pick(t, (256, 128, 64, 32, 16, 8))
    blk = pl.BlockSpec((tr, gw), lambda i, g: (i, g))
    par = pl.BlockSpec((1, gw), lambda i, g: (0, g))
    return pl.pallas_call(
        _ssm_out_kernel,
        grid=(t // tr, SSM_GROUPS),
        in_specs=[pl.BlockSpec((2, tr, gw), lambda i, g: (0, i, g)), blk, blk, par, par],
        out_specs=blk,
        out_shape=jax.ShapeDtypeStruct((t, inner), BF16),
        compiler_params=_params(("parallel", "parallel")),
        name="ssm_out",
    )(y, xbc, z, d_skip_cols, norm_w.reshape(1, inner))


def _odd_mixer(hx, hc, n_seg, w_in, conv_w, conv_b, dt_bias, a_log, d_skip, norm_w, w_out, need_ctx):
    inner = norm_w.shape[0]
    n_heads = d_skip.shape[0]
    conv_dim = conv_w.shape[1]
    hp = -(-n_heads // LANES) * LANES
    w_z = w_in[:, :inner].astype(BF16)
    w_xbc = w_in[:, inner:inner + conv_dim].astype(BF16)
    w_dt = jnp.pad(w_in[:, inner + conv_dim:].reshape(-1, 2, n_heads),
                   ((0, 0), (0, 0), (0, hp - n_heads))).reshape(-1, 2 * hp).astype(BF16)
    pad_h = lambda p: jnp.pad(p, ((0, 0), (0, hp - n_heads))).reshape(2, 1, hp)
    bias_p, alog_p = pad_h(dt_bias), pad_h(a_log)
    d_cols = jnp.repeat(d_skip, SSM_HEAD_DIM).reshape(1, inner)
    w_out_b = w_out.astype(BF16)
    conv_b2 = conv_b.reshape(1, conv_dim)

    def side(h, s0, need_out):
        seg = h.shape[0] // n_seg
        z = _matmul(h, w_z)
        xbc = _row_neighbour_call(_conv_kernel, _matmul(h, w_xbc), seg, [conv_w, conv_b2], "conv_silu")
        dt_raw = _matmul(h, w_dt)
        y, s_fin = _ssd_scan(xbc, dt_raw, bias_p, alog_p, s0, n_seg, inner)
        out = _matmul(_ssm_out(y, xbc, z, d_cols, norm_w), w_out_b) if need_out else None
        return out, s_fin

    s_zero = jnp.zeros((n_seg, 2, SSM_GROUPS, SSM_STATE, inner // SSM_GROUPS), F32)
    oc, s_ctx = side(hc, s_zero, need_ctx)
    ox, _ = side(hx, s_ctx, True)
    return ox, oc


def kernel(x, c, ctx, c_ctx, ada_w, ada_b, norm_g, ev_w_in, ev_shift_mu, rk_w0, rk_w_up, rk_a0, rk_a_up, rk_g_up,
           rk_k_k, rk_k_a, rk_r_k, rk_ln_w, rk_ln_b, na_rpb, ev_w_out, od_w_in, od_conv_w, od_conv_b, od_dt_bias,
           od_a_log, od_d, od_norm_w, od_w_out, router_w, router_b, moe_w1, moe_b1, moe_w2, moe_b2):
    n_b, n_seq, d = x.shape
    n_ctx = ctx.shape[1]
    depth = ada_w.shape[0]
    xt = x.reshape(n_b * n_seq, d)
    ct = ctx.reshape(n_b * n_ctx, d)
    n_cond = -(-(n_b + 1) // SUBLANES) * SUBLANES
    cond = jnp.zeros((n_cond, d), F32).at[:n_b].set(c).at[n_b].set(c_ctx)
    mods = _adaln(cond, ada_w, ada_b)
    hx = hc = None
    for layer in range(depth):
        need_ctx = layer < depth - 1
        i = layer // 2
        mod_x = mods[layer, :n_b].reshape(n_b, 6, d)
        mod_c = jnp.broadcast_to(mods[layer, n_b].reshape(1, 6, d), (n_b, 6, d))
        g = norm_g[layer]
        if layer == 0:
            hx = _norm_mod(xt, g[0], mod_x, 0, 1)
            hc = _norm_mod(ct, g[0], mod_c, 0, 1)
        if layer % 2 == 0:
            ox, oc = _even_mixer(hx, hc, n_b, ev_w_in[i], ev_shift_mu[i], rk_w0[i], rk_w_up[i], rk_a0[i],
                                 rk_a_up[i], rk_g_up[i], rk_k_k[i], rk_k_a[i], rk_r_k[i], rk_ln_w[i], rk_ln_b[i],
                                 na_rpb[i], ev_w_out[i], need_ctx)
        else:
            ox, oc = _odd_mixer(hx, hc, n_b, od_w_in[i], od_conv_w[i], od_conv_b[i], od_dt_bias[i], od_a_log[i],
                                od_d[i], od_norm_w[i], od_w_out[i], need_ctx)
        router = (router_w[layer], router_b[layer])
        experts = _moe_weights(moe_w1, layer, moe_b1[layer], moe_w2[layer], moe_b2[layer])
        streams = [(xt, ox, mod_x)] + ([(ct, oc, mod_c)] if need_ctx else [])
        new = []
        for tok, o, mod in streams:
            tok, h2, top_idx, gate = _residual(tok, o, g[1], mod, 2, g_b=g[2], i_shift=3, i_scale=4, router=router)
            slots = _moe(h2, top_idx, *experts)
            if layer + 1 < depth:
                g_next = norm_g[layer + 1]
                mod_next = (mods[layer + 1, :n_b].reshape(n_b, 6, d) if mod is mod_x else
                            jnp.broadcast_to(mods[layer + 1, n_b].reshape(1, 6, d), (n_b, 6, d)))
                tok, h_next = _residual(tok, slots, g[3], mod, 5, n_sum=TOP_K, g_b=g_next[0], i_shift=0, i_scale=1,
                                        h_dtype=BF16, mod_h=mod_next, slot_gate=gate)
            else:
                (tok,) = _residual(tok, slots, g[3], mod, 5, n_sum=TOP_K, slot_gate=gate)
                h_next = None
            new.append((tok, h_next))
        xt, hx = new[0]
        if need_ctx:
            ct, hc = new[1]
    return xt.reshape(n_b, n_seq, d)
```

```python
import functools
import math

import numpy as np
import jax
import jax.numpy as jnp
from jax import lax
from jax.experimental import pallas as pl
from jax.experimental.pallas import tpu as pltpu

F32 = jnp.float32
BF16 = jnp.bfloat16

RMS_EPS = 1e-6
GN_EPS = 64e-5
HEAD_DIM = 64
GRID_W = 64
WIN_R = 8
WIN_C = 16
SSM_HEAD_DIM = 64
SSM_GROUPS = 8
SSM_STATE = 128
SSM_CHUNK = 128
TOP_K = 4
SWIGLU_ALPHA = 1.702
SWIGLU_LIMIT = 7.0
MOE_ROWS = 256
RWKV_CHUNK = 64
MM_TM = (1024, 512, 256, 128, 64, 32, 16, 8)
MM_TN = (1024, 768, 640, 512, 384, 256, 128)
MM_TK = (4096, 2048, 1024, 512, 256, 128)
SSD_GROUPS_PER_STEP = 4
NA_ROWS_PER_STEP = 16
RWKV_PAIRS_PER_STEP = 16
LANES = 128
SUBLANES = 8
VMEM_LIMIT = 56 * 1024 * 1024
NEG_BIG = -1e30


def _pick(n, cands):
    for c in cands:
        if n % c == 0:
            return c
    return n


def _params(sem, vmem=VMEM_LIMIT):
    return pltpu.CompilerParams(dimension_semantics=sem, vmem_limit_bytes=vmem)


def _split3(x):
    hi = x.astype(BF16)
    r1 = x - hi.astype(F32)
    mid = r1.astype(BF16)
    lo = (r1 - mid.astype(F32)).astype(BF16)
    return hi, mid, lo


def _dot(a, b):
    return jnp.dot(a, b, preferred_element_type=F32)


def _dot_nt(a, b):
    return lax.dot_general(a, b, (((1,), (1,)), ((), ())), preferred_element_type=F32)


def _dot_tn(a, b):
    return lax.dot_general(a, b, (((0,), (0,)), ((), ())), preferred_element_type=F32)


def _dot_exact_rhs(x, m_bf16):
    hi, mid, lo = _split3(x)
    return _dot(hi, m_bf16) + _dot(mid, m_bf16) + _dot(lo, m_bf16)


def _mm_kernel(a_ref, w_ref, b_ref, o_ref, acc_ref, *, nk, pre):
    if nk == 1:
        a = a_ref[...]
        if pre == "silu":
            a = a * jax.nn.sigmoid(a)
        o_ref[...] = (_dot(a.astype(BF16), w_ref[...].astype(BF16)) + b_ref[...]).astype(o_ref.dtype)
        return
    k = pl.program_id(2)

    @pl.when(k == 0)
    def _():
        acc_ref[...] = jnp.zeros_like(acc_ref)

    a = a_ref[...]
    if pre == "silu":
        a = a * jax.nn.sigmoid(a)
    acc_ref[...] += _dot(a.astype(BF16), w_ref[...].astype(BF16))

    @pl.when(k == nk - 1)
    def _():
        o_ref[...] = (acc_ref[...] + b_ref[...]).astype(o_ref.dtype)


def _matmul(a, w, bias=None, out_dtype=F32, pre=None, rows=None):
    start, m = (0, a.shape[0]) if rows is None else rows
    kdim = a.shape[1]
    n = w.shape[1]
    tm = _pick(math.gcd(m, start) if start else m, MM_TM)
    tn = _pick(n, MM_TN)
    tk = _pick(kdim, MM_TK)
    nk = kdim // tk
    first = start // tm
    if bias is None:
        bias = jnp.zeros((n,), F32)
    return pl.pallas_call(
        functools.partial(_mm_kernel, nk=nk, pre=pre),
        grid=(m // tm, n // tn, nk),
        in_specs=[pl.BlockSpec((tm, tk), lambda i, j, k: (first + i, k)),
                  pl.BlockSpec((tk, tn), lambda i, j, k: (k, j)),
                  pl.BlockSpec((1, tn), lambda i, j, k: (0, j))],
        out_specs=pl.BlockSpec((tm, tn), lambda i, j, k: (i, j)),
        out_shape=jax.ShapeDtypeStruct((m, n), out_dtype),
        scratch_shapes=[pltpu.VMEM((tm, tn), F32)],
        compiler_params=_params(("parallel", "parallel", "arbitrary")),
        name="matmul",
    )(a, w, bias.reshape(1, n).astype(F32))


def _adaln_kernel(c_ref, w_ref, b_ref, o_ref, acc_ref, *, nk):
    k = pl.program_id(2)

    @pl.when(k == 0)
    def _():
        acc_ref[...] = jnp.zeros_like(acc_ref)

    c = c_ref[...]
    c = c * jax.nn.sigmoid(c)
    acc_ref[...] += _dot(c.astype(BF16), w_ref[...].astype(BF16))

    @pl.when(k == nk - 1)
    def _():
        o_ref[...] = acc_ref[...] + b_ref[...]


def _adaln(cond, ada_w, ada_b):
    depth, d, n = ada_w.shape
    r = cond.shape[0]
    tn = _pick(n, (2048, 1024, 512, 256, 128))
    tk = _pick(d, (1024, 512, 256, 128))
    nk = d // tk
    return pl.pallas_call(
        functools.partial(_adaln_kernel, nk=nk),
        grid=(depth, n // tn, nk),
        in_specs=[pl.BlockSpec((r, tk), lambda l, j, k: (0, k)),
                  pl.BlockSpec((None, tk, tn), lambda l, j, k: (l, k, j)),
                  pl.BlockSpec((None, 1, tn), lambda l, j, k: (l, 0, j))],
        out_specs=pl.BlockSpec((None, r, tn), lambda l, j, k: (l, 0, j)),
        out_shape=jax.ShapeDtypeStruct((depth, r, n), F32),
        scratch_shapes=[pltpu.VMEM((r, tn), F32)],
        compiler_params=_params(("parallel", "parallel", "arbitrary")),
        name="adaln",
    )(cond, ada_w, ada_b.reshape(depth, 1, n))


def _rms(x, g):
    return x * lax.rsqrt(jnp.mean(x * x, axis=-1, keepdims=True) + RMS_EPS) * g


def _norm_mod_kernel(x_ref, g_ref, mod_ref, o_ref, *, i_shift, i_scale):
    y = _rms(x_ref[...], g_ref[...])
    o = y * (1.0 + mod_ref[i_scale:i_scale + 1, :]) + mod_ref[i_shift:i_shift + 1, :]
    o_ref[...] = o.astype(o_ref.dtype)


def _norm_mod(x, g, mod, i_shift, i_scale, out_dtype=BF16):
    t, d = x.shape
    seg = t // mod.shape[0]
    tr = _pick(seg, (256, 128, 64, 32, 16, 8))
    per = seg // tr
    return pl.pallas_call(
        functools.partial(_norm_mod_kernel, i_shift=i_shift, i_scale=i_scale),
        grid=(t // tr,),
        in_specs=[pl.BlockSpec((tr, d), lambda i: (i, 0)),
                  pl.BlockSpec((1, d), lambda i: (0, 0)),
                  pl.BlockSpec((None, 6, d), lambda i: (i // per, 0, 0))],
        out_specs=pl.BlockSpec((tr, d), lambda i: (i, 0)),
        out_shape=jax.ShapeDtypeStruct((t, d), out_dtype),
        compiler_params=_params(("parallel",)),
        name="norm_mod",
    )(x, g.reshape(1, d), mod)


def _residual_kernel(*refs, n_sum, i_gate, i_shift, i_scale, with_h, with_router, n_exp):
    x_ref = refs[0]
    o_refs = refs[1:1 + n_sum]
    ga_ref, mod_ref = refs[1 + n_sum:3 + n_sum]
    pos = 3 + n_sum
    if n_sum > 1:
        sg_ref = refs[pos]
        pos += 1
    if with_h:
        gb_ref, modh_ref = refs[pos:pos + 2]
        pos += 2
    if with_router:
        rw_ref, rb_ref = refs[pos:pos + 2]
        pos += 2
    xo_ref = refs[pos]
    pos += 1
    if n_sum == 1:
        o = o_refs[0][...]
    else:
        sg = sg_ref[...]
        o = o_refs[0][...] * sg[:, 0:1]
        for j in range(1, n_sum):
            o = o + o_refs[j][...] * sg[:, j:j + 1]
    xn =x_ref[...] + mod_ref[i_gate:i_gate + 1, :] * _rms(o, ga_ref[...])
    xo_ref[...] = xn
    if not with_h:
        return
    h_ref = refs[pos]
    pos += 1
    h = _rms(xn, gb_ref[...]) * (1.0 + modh_ref[i_scale:i_scale + 1, :]) + modh_ref[i_shift:i_shift + 1, :]
    h_ref[...] = h.astype(h_ref.dtype)
    if not with_router:
        return
    idx_ref, gate_ref = refs[pos:pos + 2]
    h1, h2, h3 = _split3(h)
    w1 = rw_ref[0]
    w2 = rw_ref[1]
    logits = (_dot(h1, w1) + _dot(h2, w1) + _dot(h1, w2) + _dot(h3, w1) + _dot(h2, w2)) + rb_ref[...]
    lane = lax.broadcasted_iota(jnp.int32, logits.shape, 1)
    logits = jnp.where(lane < n_exp, logits, NEG_BIG)
    vals, idxs = [], []
    for _ in range(TOP_K):
        m = jnp.max(logits, axis=-1, keepdims=True)
        ix = jnp.min(jnp.where(logits == m, lane, 1 << 30), axis=-1, keepdims=True)
        vals.append(m)
        idxs.append(ix)
        logits = jnp.where(lane == ix, NEG_BIG * 2, logits)
    es = [jnp.exp(v - vals[0]) for v in vals]
    den = es[0]
    for e in es[1:]:
        den = den + e
    idx_out = jnp.zeros(lane.shape, jnp.int32)
    gate_out = jnp.zeros(lane.shape, F32)
    for j in range(TOP_K):
        idx_out = jnp.where(lane == j, idxs[j], idx_out)
        gate_out = jnp.where(lane == j, es[j] / den, gate_out)
    idx_ref[...] = idx_out
    gate_ref[...] = gate_out


def _residual(x, o, g_a, mod, i_gate, n_sum=1, g_b=None, i_shift=0, i_scale=0, h_dtype=F32, router=None,
              mod_h=None, slot_gate=None):
    t, d = x.shape
    seg = t // mod.shape[0]
    tr = _pick(seg, (128, 64, 32, 16, 8))
    per = seg // tr
    nt = t // tr
    with_h = g_b is not None
    with_router = router is not None
    row = lambda i: (i, 0)
    fixed = lambda i: (0, 0)
    ins = [x] + [o] * n_sum + [g_a.reshape(1, d), mod]
    in_specs = ([pl.BlockSpec((tr, d), row)]
                + [pl.BlockSpec((tr, d), functools.partial(lambda j, i: (j * nt + i, 0), j)) for j in range(n_sum)]
                + [pl.BlockSpec((1, d), fixed), pl.BlockSpec((None, 6, d), lambda i: (i // per, 0, 0))])
    outs = [jax.ShapeDtypeStruct((t, d), F32)]
    out_specs = [pl.BlockSpec((tr, d), row)]
    n_exp = 0
    if n_sum > 1:
        ins.append(slot_gate)
        in_specs.append(pl.BlockSpec((tr, slot_gate.shape[1]), row))
    if with_h:
        ins += [g_b.reshape(1, d), mod if mod_h is None else mod_h]
        in_specs += [pl.BlockSpec((1, d), fixed), pl.BlockSpec((None, 6, d), lambda i: (i // per, 0, 0))]
        outs.append(jax.ShapeDtypeStruct((t, d), h_dtype))
        out_specs.append(pl.BlockSpec((tr, d), row))
    if with_router:
        rw, rb = router
        n_exp = rw.shape[1]
        rw = jnp.pad(rw, ((0, 0), (0, LANES - n_exp)))
        rw_hi = rw.astype(BF16)
        rw_lo = (rw - rw_hi.astype(F32)).astype(BF16)
        ins += [jnp.stack([rw_hi, rw_lo]), jnp.pad(rb, (0, LANES - n_exp)).reshape(1, LANES)]
        in_specs += [pl.BlockSpec((2, d, LANES), lambda i: (0, 0, 0)), pl.BlockSpec((1, LANES), fixed)]
        outs += [jax.ShapeDtypeStruct((t, LANES), jnp.int32), jax.ShapeDtypeStruct((t, LANES), F32)]
        out_specs += [pl.BlockSpec((tr, LANES), row), pl.BlockSpec((tr, LANES), row)]
    res = pl.pallas_call(
        functools.partial(_residual_kernel, n_sum=n_sum, i_gate=i_gate, i_shift=i_shift, i_scale=i_scale,
                          with_h=with_h, with_router=with_router, n_exp=n_exp),
        grid=(t // tr,),
        in_specs=in_specs,
        out_specs=out_specs,
        out_shape=outs,
        compiler_params=_params(("parallel",)),
        name="residual",
    )(*ins)
    return res


def _moe_kernel(nb_ref, be_ref, qs_ref, cnt_ref, order_ref, h_hbm, w1_ref, b1_ref, w2_ref, b2_ref, out_hbm,
                xbuf_a, xbuf_b, obuf_a, obuf_b, sem_in, sem_out, *, bm, n_tok, f):
    del be_ref
    i = pl.program_id(0)
    nb = nb_ref[0]
    xbuf = (xbuf_a, xbuf_b)
    obuf = (obuf_a, obuf_b)
    shift = TOP_K.bit_length() - 1
    unroll = 8

    def gather_row(q0, buf, r):
        tok = lax.shift_right_logical(order_ref[q0 + r], shift)
        return pltpu.make_async_copy(h_hbm.at[pl.ds(tok, 1)], xbuf[buf].at[pl.ds(r, 1)], sem_in.at[buf])

    def scatter_row(q0, buf, r):
        a = order_ref[q0 + r]
        slot = (a & (TOP_K - 1)) * n_tok + lax.shift_right_logical(a, shift)
        return pltpu.make_async_copy(obuf[buf].at[pl.ds(r, 1)], out_hbm.at[pl.ds(slot, 1)], sem_out.at[buf])

    def for_rows(fn, n_rows):
        n_full = lax.shift_right_logical(n_rows, unroll.bit_length() - 1)

        def group(gi, c):
            for k in range(unroll):
                fn(gi * unroll + k)
            return c

        def single(r, c):
            fn(r)
            return c

        lax.fori_loop(0, n_full, group, 0)
        lax.fori_loop(n_full * unroll, n_rows, single, 0)

    def wait_gathered(buf):
        pltpu.make_async_copy(h_hbm.at[pl.ds(0, bm)], xbuf[buf], sem_in.at[buf]).wait()

    def wait_scattered(buf, n_rows):
        row = pltpu.make_async_copy(obuf[buf].at[pl.ds(0, 1)], out_hbm.at[pl.ds(0, 1)], sem_out.at[buf])
        for_rows(lambda r: row.wait(), n_rows)

    def step(cur):
        if cur == 0:
            @pl.when(i == 0)
            def _():
                for_rows(lambda r: gather_row(qs_ref[0], 0, r).start(), bm)

        wait_gathered(cur)
        q_next = qs_ref[jnp.minimum(i + 1, nb - 1)]
        for r in range(bm):
            gather_row(q_next, 1 - cur, r).start()
        x = xbuf[cur][...].astype(BF16)
        u = _dot(x, w1_ref[...]) + b1_ref[...]
        glu = jnp.minimum(u[:, :f], SWIGLU_LIMIT)
        lin = jnp.clip(u[:, f:], -SWIGLU_LIMIT, SWIGLU_LIMIT)
        act = glu * jax.nn.sigmoid(SWIGLU_ALPHA * glu) * (lin + 1.0)
        out = _dot(act.astype(BF16), w2_ref[...]) + b2_ref[...]

        @pl.when(i >= 1)
        def _():
            wait_scattered(1 - cur, cnt_ref[jnp.maximum(i - 1, 0)])

        obuf[cur][...] = out
        q_cur = qs_ref[i]
        for_rows(lambda r: scatter_row(q_cur, cur, r).start(), cnt_ref[i])

        @pl.when(i == nb - 1)
        def _():
            wait_scattered(cur, cnt_ref[i])
            wait_gathered(1 - cur)

    for parity in range(2):
        pl.when(jnp.logical_and(i < nb, i % 2 == parity))(functools.partial(step, parity))


def _moe(h, top_idx, w1, b1, w2, b2):
    assert TOP_K & (TOP_K - 1) == 0
    t, d = h.shape
    n_exp, f = w2.shape[0], w2.shape[1]
    bm = MOE_ROWS
    n_assign = t * TOP_K
    flat_e = top_idx[:, :TOP_K].reshape(-1)
    order = jnp.argsort(flat_e, stable=True).astype(jnp.int32)
    counts = jnp.sum(flat_e[:, None] == jnp.arange(n_exp, dtype=flat_e.dtype)[None, :], axis=0, dtype=jnp.int32)
    padded = (counts + bm - 1) // bm * bm
    pad_end = jnp.cumsum(padded)
    pad_start = pad_end - padded
    sort_start = jnp.cumsum(counts) - counts
    n_blocks = -(-(n_assign + n_exp * (bm - 1)) // bm)
    blk_row = jnp.arange(n_blocks, dtype=jnp.int32) * bm
    block_exp = jnp.minimum(jnp.searchsorted(pad_end, blk_row, side="right"), n_exp - 1).astype(jnp.int32)
    off = blk_row - pad_start[block_exp]
    q_start = (sort_start[block_exp] + off).astype(jnp.int32)
    cnt = jnp.clip(counts[block_exp] - off, 0, bm).astype(jnp.int32)
    n_used = (pad_end[-1:] // bm).astype(jnp.int32)
    pre = lambda i, nb, be, qs, ct, od: (be[i], 0, 0)
    grid_spec = pltpu.PrefetchScalarGridSpec(
        num_scalar_prefetch=5,
        grid=(n_blocks,),
        in_specs=[pl.BlockSpec(memory_space=pl.ANY),
                  pl.BlockSpec((None, d, 2 * f), pre),
                  pl.BlockSpec((None, 1, 2 * f), pre),
                  pl.BlockSpec((None, f, d), pre),
                  pl.BlockSpec((None, 1, d), pre)],
        out_specs=pl.BlockSpec(memory_space=pl.ANY),
        scratch_shapes=[pltpu.VMEM((bm, d), F32), pltpu.VMEM((bm, d), F32),
                        pltpu.VMEM((bm, d), F32), pltpu.VMEM((bm, d), F32),
                        pltpu.SemaphoreType.DMA((2,)), pltpu.SemaphoreType.DMA((2,))],
    )
    return pl.pallas_call(
        functools.partial(_moe_kernel, bm=bm, n_tok=t, f=f),
        grid_spec=grid_spec,
        out_shape=jax.ShapeDtypeStruct((n_assign, d), F32),
        compiler_params=_params(("arbitrary",)),
        name="moe_experts",
    )(n_used, block_exp, q_start, cnt, jnp.pad(order, (0, bm)), h, w1, b1, w2, b2)


def _moe_weights(w1_all, layer, b1, w2, b2):
    _, n_exp, d, f2 = w1_all.shape
    perm = np.concatenate([np.arange(0, f2, 2), np.arange(1, f2, 2)])
    pmat = jnp.asarray(np.arange(f2)[:, None] == perm[None, :], BF16)
    w1p = _matmul(w1_all.reshape(-1, f2), pmat, out_dtype=BF16,
                  rows=(layer * n_exp * d, n_exp * d)).reshape(n_exp, d, f2)
    return w1p, b1[:, perm].reshape(n_exp, 1, f2), w2.astype(BF16), b2.reshape(n_exp, 1, -1)


def _row_shift(u, prev8, next8, k):
    n = u.shape[0]
    rolled = pltpu.roll(u, k % n, 0)
    sub = lax.broadcasted_iota(jnp.int32, (SUBLANES, u.shape[1]), 0)
    if k > 0:
        halo = pltpu.roll(prev8, k, 0)
        top = jnp.where(sub < k, halo, rolled[0:SUBLANES])
        return jnp.concatenate([top, rolled[SUBLANES:]], axis=0)
    halo = pltpu.roll(next8, SUBLANES + k, 0)
    bot = jnp.where(sub >= SUBLANES + k, halo, rolled[n - SUBLANES:])
    return jnp.concatenate([rolled[:n - SUBLANES], bot], axis=0)


def _halo_tiles(prev_ref, next_ref, per):
    j = pl.program_id(0) % per
    prev8 = jnp.where(j == 0, 0.0, prev_ref[...])
    next8 = jnp.where(j == per - 1, 0.0, next_ref[...])
    return prev8, next8


def _conv_kernel(u_ref, prev_ref, next_ref, w_ref, b_ref, o_ref, *, per):
    u = u_ref[...]
    prev8, next8 = _halo_tiles(prev_ref, next_ref, per)
    acc = (w_ref[0:1, :] * _row_shift(u, prev8, next8, 2) + w_ref[1:2, :] * _row_shift(u, prev8, next8, 1)
           + w_ref[2:3, :] * u + w_ref[3:4, :] * _row_shift(u, prev8, next8, -1)) + b_ref[...]
    o_ref[...] = acc * jax.nn.sigmoid(acc)


def _row_neighbour_call(kernel_fn, u, seg, extra, name):
    t, n = u.shape
    tr = _pick(seg, (256, 128, 64, 32, 16, 8))
    tc = _pick(n, (2048, 1024, 768, 512, 384, 256, 128))
    per = seg // tr
    r8 = tr // SUBLANES
    last8 = t // SUBLANES - 1
    in_specs = [pl.BlockSpec((tr, tc), lambda i, j: (i, j)),
                pl.BlockSpec((SUBLANES, tc), lambda i, j: (jnp.maximum(i * r8 - 1, 0), j)),
                pl.BlockSpec((SUBLANES, tc), lambda i, j: (jnp.minimum((i + 1) * r8, last8), j))]
    for e in extra:
        in_specs.append(pl.BlockSpec((e.shape[0], tc), lambda i, j: (0, j)))
    return pl.pallas_call(
        functools.partial(kernel_fn, per=per),
        grid=(t // tr, n // tc),
        in_specs=in_specs,
        out_specs=pl.BlockSpec((tr, tc), lambda i, j: (i, j)),
        out_shape=jax.ShapeDtypeStruct((t, n), F32),
        compiler_params=_params(("parallel", "parallel")),
        name=name,
    )(u, u, u, *extra)


def _head_sum_matrix(n):
    idx = np.arange(n) // HEAD_DIM
    return jnp.asarray(idx[:, None] == idx[None, :], BF16)


def _centred_shift(u_ref, prev_ref, next_ref, mu_ref, per):
    u = u_ref[...]
    prev8, next8 = _halo_tiles(prev_ref, next_ref, per)
    nb = 0.5 * (_row_shift(u, prev8, next8, 1) + _row_shift(u, prev8, next8, -1))
    return u + mu_ref[...] * (nb - u)


def _rwkv_prep_kernel(*refs, wd_w, ad_w, per):
    r, k, v, lo = [_centred_shift(*refs[4 * n:4 * n + 4], per) for n in range(4)]
    (w0_ref, wup_ref, a0_ref, aup_ref, gup_ref, kk_ref_, ka_ref, rk_ref, hs_ref,
     r_o, v_o, kk_o, lw_o, kd_o, b_o, g_o, bv_o) = refs[16:]
    wd =jnp.tanh(lo[:, 0:wd_w]).astype(BF16)
    ad = lo[:, wd_w:wd_w + ad_w].astype(BF16)
    gs = jax.nn.sigmoid(lo[:, wd_w + ad_w:]).astype(BF16)
    hs = hs_ref[...]
    kkr = k * kk_ref_[...]
    ss = _dot_exact_rhs(kkr * kkr, hs)
    kk = kkr * lax.rsqrt(jnp.maximum(ss, 1e-24))
    ksum = jnp.zeros_like(k)
    for d in range(2):
        z = -(w0_ref[d:d + 1, :] + _dot(wd, wup_ref[d]))
        w_log = -(jnp.maximum(z, 0.0) + jnp.log(1.0 + jnp.exp(-jnp.abs(z)))) - 0.5
        lw_o[d] = -jnp.exp(w_log)
        asig = jax.nn.sigmoid(a0_ref[d:d + 1, :] + _dot(ad, aup_ref[d]))
        kd = k * (1.0 + (asig - 1.0) * ka_ref[...])
        kd_o[d] = kd
        b_o[d] = kk * asig
        ksum = ksum + kd
    bonus = _dot_exact_rhs(r * ksum * rk_ref[...], hs)
    r_o[...] = r
    v_o[...] = v
    kk_o[...] = kk
    g_o[...] = _dot(gs, gup_ref[...])
    bv_o[...] = bonus * v


def _rwkv_prep(p_rkv, p_lora, mu_rkv, mu_lora, p, seg):
    t = p_rkv.shape[0]
    a = p_rkv.shape[1] // 3
    nl = p_lora.shape[1]
    tr = _pick(seg, (256, 128, 64, 32, 16, 8))
    tc = _pick(a, (512, 256, 128))
    nj = a // tc
    per = seg // tr
    r8 = tr // SUBLANES
    last8 = t // SUBLANES - 1

    def shifted_specs(width, colmap):
        return [pl.BlockSpec((tr, width), lambda i, j: (i, colmap(j))),
                pl.BlockSpec((SUBLANES, width), lambda i, j: (jnp.maximum(i * r8 - 1, 0), colmap(j))),
                pl.BlockSpec((SUBLANES, width), lambda i, j: (jnp.minimum((i + 1) * r8, last8), colmap(j))),
                pl.BlockSpec((1, width), lambda i, j: (0, colmap(j)))]

    shifted_in = []
    shifted_args = []
    for off in range(3):
        shifted_in += shifted_specs(tc, functools.partial(lambda o, j: o * nj + j, off))
        shifted_args += [p_rkv, p_rkv, p_rkv, mu_rkv]
    shifted_in += shifted_specs(nl, lambda j: 0)
    shifted_args += [p_lora, p_lora, p_lora, mu_lora]
    par = lambda rows: pl.BlockSpec((rows, tc), lambda i, j: (0, j))
    par3 = lambda rows: pl.BlockSpec((2, rows, tc), lambda i, j: (0, 0, j))
    one = jax.ShapeDtypeStruct((t, a), F32)
    two = jax.ShapeDtypeStruct((2, t, a), F32)
    o1 = pl.BlockSpec((tr, tc), lambda i, j: (i, j))
    o2 = pl.BlockSpec((2, tr, tc), lambda i, j: (0, i, j))
    return pl.pallas_call(
        functools.partial(_rwkv_prep_kernel, wd_w=p["wd_w"], ad_w=p["ad_w"], per=per),
        grid=(t // tr, nj),
        in_specs=shifted_in + [par(2), par3(p["wd_w"]), par(2), par3(p["ad_w"]), par(p["g_up"].shape[0]),
                               par(1), par(1), par(1),
                               pl.BlockSpec((tc, tc), lambda i, j: (0, 0))],
        out_specs=[o1, o1, o1, o2, o2, o2, o1, o1],
        out_shape=[one, one, one, two, two, two, one, one],
        compiler_params=_params(("parallel", "parallel")),
        name="rwkv_prep",
    )(*shifted_args, p["w0"], p["w_up"], p["a0"], p["a_up"], p["g_up"], p["k_k"], p["k_a"],
      p["r_k"], _head_sum_matrix(tc))


def _rwkv_chunk_pairs(ins, states, d, masks):
    incl, strict, tri, eye, first, first2, blockdiag = masks
    cl = RWKV_CHUNK
    pairs = range(len(ins))
    heads = [(p, hh) for p in pairs for hh in range(2)]
    splits = [_split3(ins[p][3]) for p in pairs]
    cum = [_dot(tri, h1) + _dot(tri, h2) + _dot(tri, h3) for h1, h2, h3 in splits]
    cum_last, ar, ar_b, bk, bk_end, sb, vb = [], [], [], [], [], [], []
    for p in pairs:
        r, v, kk, lw, kd, b = ins[p]
        cl_p = jnp.where(d == 0, cum[p][cl - 1:cl, :], cum[p][0:1, :])
        ginv = jnp.exp(-cum[p])
        to_end = jnp.exp(cl_p - cum[p])
        at = -kk * jnp.exp(cum[p] - lw)
        rt = r * jnp.exp(cum[p])
        cum_last.append(cl_p)
        bk.append(jnp.concatenate([b * ginv, kd * ginv], axis=0).astype(BF16))
        bk_end.append(jnp.concatenate([b * to_end, kd * to_end], axis=0).astype(BF16))
        ar.append(jnp.concatenate([at, rt], axis=0))
        ar_b.append(ar[p].astype(BF16))
        sb.append(states[p].astype(BF16))
        vb.append(v.astype(BF16))
    ar_h = {(p, hh): jnp.where(first2 if hh == 0 else jnp.logical_not(first2), ar[p], 0.0).astype(BF16)
            for p, hh in heads}
    m = {k: _dot_nt(ar_h[k], bk[k[0]]) for k in heads}
    xs = [_dot_nt(ar_b[p], sb[p]) for p in pairs]
    a_ab = {k: jnp.where(strict, m[k][:cl, :cl], 0.0) for k in heads}
    a_ak = {k: jnp.where(strict, m[k][:cl, cl:], 0.0).astype(BF16) for k in heads}
    rbk = {k: jnp.concatenate([jnp.where(incl, m[k][cl:, :cl], 0.0), jnp.where(incl, m[k][cl:, cl:], 0.0)],
                              axis=1).astype(BF16) for k in heads}
    rhs = {k: (xs[k[0]][:cl] + _dot(a_ak[k], vb[k[0]])).astype(BF16) for k in heads}
    tinv = {k: eye + a_ab[k] for k in heads}
    pw = {k: a_ab[k].astype(BF16) for k in heads}
    for _ in range(int(math.log2(cl)) - 1):
        pw = {k: _dot(pw[k], pw[k]).astype(BF16) for k in heads}
        tinv = {k: tinv[k] + _dot(tinv[k].astype(BF16), pw[k]) for k in heads}
    u_h = {k: _dot(tinv[k].astype(BF16), rhs[k]) for k in heads}
    uv = [jnp.concatenate([jnp.where(first, u_h[(p, 0)], u_h[(p, 1)]), ins[p][1]], axis=0).astype(BF16)
          for p in pairs]
    y_h = {k: _dot(rbk[k], uv[k[0]]) for k in heads}
    upd = [_dot_tn(uv[p], bk_end[p]) for p in pairs]
    out = []
    for p in pairs:
        y = xs[p][cl:] + jnp.where(first, y_h[(p, 0)], y_h[(p, 1)])
        s_new = states[p] * jnp.exp(cum_last[p]) + jnp.where(blockdiag, upd[p], 0.0)
        out.append((y, s_new))
    return out


def _rwkv_scan_kernel(r_ref, v_ref, kk_ref, lw_ref, kd_ref, b_ref, s0_ref, y_ref, sf_ref, s_scr, *, n_chunks,
                      n_pairs):
    d = pl.program_id(1)
    c = pl.program_id(3)
    cl = RWKV_CHUNK

    @pl.when(c == 0)
    def _():
        s_scr[...] = s0_ref[...]

    sgn = 1 - 2 * d
    row = lax.broadcasted_iota(jnp.int32, (cl, cl), 0)
    col = lax.broadcasted_iota(jnp.int32, (cl, cl), 1)
    order = (row - col) * sgn
    incl = order >= 0
    strict = order > 0
    tri = jnp.where(incl, 1.0, 0.0).astype(BF16)
    eye = jnp.where(row == col, 1.0, 0.0)
    first = lax.broadcasted_iota(jnp.int32, (cl, LANES), 1) < HEAD_DIM
    first2 = lax.broadcasted_iota(jnp.int32, (2 * cl, LANES), 1) < HEAD_DIM
    rr = lax.broadcasted_iota(jnp.int32, (LANES, LANES), 0) // HEAD_DIM
    cc = lax.broadcasted_iota(jnp.int32, (LANES, LANES), 1) // HEAD_DIM
    masks = (incl, strict, tri, eye, first, first2, rr == cc)
    ins = []
    for p in range(n_pairs):
        sl = slice(p * LANES, (p + 1) * LANES)
        ins.append((r_ref[:, sl], v_ref[:, sl], kk_ref[:, sl], lw_ref[:, sl], kd_ref[:, sl], b_ref[:, sl]))
    results = _rwkv_chunk_pairs(ins, [s_scr[p] for p in range(n_pairs)], d, masks)
    for p, (y, s_new) in enumerate(results):
        y_ref[:, p * LANES:(p + 1) * LANES] = y
        s_scr[p] = s_new

    @pl.when(c == n_chunks - 1)
    def _():
        sf_ref[...] = s_scr[...]


def _rwkv_scan(r, v, kk, lw, kd, b, s0, n_seg):
    t, a = r.shape
    seg = t // n_seg
    cl = RWKV_CHUNK
    nc = seg // cl
    npair = a // LANES
    pb = _pick(npair, (RWKV_PAIRS_PER_STEP, 4, 2, 1))
    w = pb * LANES

    def rows(bi, d, p, c):
        return bi * nc + jnp.where(d == 0, c, nc - 1 - c)

    shared = pl.BlockSpec((cl, w), lambda bi, d, p, c: (rows(bi, d, p, c), p))
    per_dir = pl.BlockSpec((None, cl, w), lambda bi, d, p, c: (d, rows(bi, d, p, c), p))
    state = pl.BlockSpec((None, None, pb, LANES, LANES), lambda bi, d, p, c: (bi, d, p, 0, 0))
    return pl.pallas_call(
        functools.partial(_rwkv_scan_kernel, n_chunks=nc, n_pairs=pb),
        grid=(n_seg, 2, npair // pb, nc),
        in_specs=[shared, shared, shared, per_dir, per_dir, per_dir, state],
        out_specs=[per_dir, state],
        out_shape=[jax.ShapeDtypeStruct((2, t, a), F32), jax.ShapeDtypeStruct(s0.shape, F32)],
        scratch_shapes=[pltpu.VMEM((pb, LANES, LANES), F32)],
        compiler_params=_params(("parallel", "parallel", "parallel", "arbitrary")),
        name="rwkv_scan",
    )(r, v, kk, lw, kd, b, s0)


def _rwkv_out_kernel(y_ref, g_ref, bv_ref, lnw_ref, lnb_ref, hs_ref, o_ref):
    y = y_ref[0] + y_ref[1]
    hs = hs_ref[...]
    inv = 1.0 / HEAD_DIM
    mu = _dot_exact_rhs(y, hs) * inv
    yc = y - mu
    var = _dot_exact_rhs(yc * yc, hs) * inv
    yn = yc * lax.rsqrt(var + GN_EPS) * lnw_ref[...] + lnb_ref[...]
    o_ref[...] = ((yn + bv_ref[...]) * g_ref[...]).astype(o_ref.dtype)


def _rwkv_out(y, g, bv, ln_w, ln_b):
    _, t, a = y.shape
    tr = _pick(t, (256, 128, 64, 32, 16, 8))
    tc = _pick(a, (512, 256, 128))
    o1 = pl.BlockSpec((tr, tc), lambda i, j: (i, j))
    par = pl.BlockSpec((1, tc), lambda i, j: (0, j))
    return pl.pallas_call(
        _rwkv_out_kernel,
        grid=(t // tr, a // tc),
        in_specs=[pl.BlockSpec((2, tr, tc), lambda i, j: (0, i, j)), o1, o1, par, par,
                  pl.BlockSpec((tc, tc), lambda i, j: (0, 0))],
        out_specs=o1,
        out_shape=jax.ShapeDtypeStruct((t, a), BF16),
        compiler_params=_params(("parallel", "parallel")),
        name="rwkv_out",
    )(y, g, bv, ln_w.reshape(1, a), ln_b.reshape(1, a), _head_sum_matrix(tc))


def _rwkv_params(shift_mu, w0, w_up, a0, a_up, g_up, k_k, k_a, r_k, a_width):
    dr, ar_, gr = w_up.shape[1], a_up.shape[1], g_up.shape[0]
    pad = lambda n: -(-n // LANES) * LANES
    wd_w, ad_w, gd_w = pad(2 * dr), pad(2 * ar_), pad(gr)

    def up(wu, rank, width):
        out = jnp.zeros((2, width, a_width), F32)
        for d in range(2):
            out = out.at[d, d * rank:(d + 1) * rank].set(wu[d])
        return out.astype(BF16)

    return dict(wd_w=wd_w, ad_w=ad_w, gd_w=gd_w, dr=dr, ar=ar_, gr=gr,
                w0=w0, a0=a0, w_up=up(w_up, dr, wd_w), a_up=up(a_up, ar_, ad_w),
                g_up=jnp.pad(g_up, ((0, gd_w - gr), (0, 0))).astype(BF16),
                k_k=k_k.reshape(1, -1), k_a=k_a.reshape(1, -1), r_k=r_k.reshape(1, -1))


def _pad_lora_cols(w, dr2, ar2, gr, p):
    parts = [(w[..., :dr2], p["wd_w"]), (w[..., dr2:dr2 + ar2], p["ad_w"]), (w[..., dr2 + ar2:], p["gd_w"])]
    return jnp.concatenate([jnp.pad(x, [(0, 0)] * (x.ndim - 1) + [(0, wd - x.shape[-1])]) for x, wd in parts],
                           axis=-1)


def _na_bias(rpb, rows):
    kr = min(WIN_R, rows)
    n_heads, n_ro, n_off = rpb.shape
    assert LANES % GRID_W == 0 and (kr * GRID_W) % LANES == 0
    return pl.pallas_call(
        functools.partial(_na_bias_kernel, kr=kr, n_ro=n_ro, n_off=n_off),
        grid=(n_heads, kr),
        in_specs=[pl.BlockSpec(memory_space=pltpu.SMEM)],
        out_specs=pl.BlockSpec((None, None, GRID_W, kr * GRID_W), lambda h, p: (h, p, 0, 0)),
        out_shape=jax.ShapeDtypeStruct((n_heads, kr, GRID_W, kr * GRID_W), F32),
        compiler_params=_params(("parallel", "parallel")),
        name="na_bias_table",
    )(rpb.reshape(-1).astype(F32))


def _na_bias_kernel(rpb_ref, o_ref, *, kr, n_ro, n_off):
    h = pl.program_id(0)
    pat = pl.program_id(1)
    w = GRID_W
    per_tile = LANES // w
    q = lax.broadcasted_iota(jnp.int32, (w, LANES), 0)
    lane = lax.broadcasted_iota(jnp.int32, (w, LANES), 1)
    i_local = lane // w
    c = lane - i_local * w
    off = c - q + (WIN_C - 1)
    start = jnp.clip(q - WIN_C // 2, 0, w - WIN_C)
    inside = jnp.logical_and(c >= start, c < start + WIN_C)
    for t in range(kr // per_tile):
        base = [(h * n_ro + (t * per_tile + k - pat + WIN_R - 1)) * n_off for k in range(per_tile)]
        val = jnp.zeros((w, LANES), F32)
        for u in range(n_off):
            s = rpb_ref[base[per_tile - 1] + u]
            for k in range(per_tile - 2, -1, -1):
                s = jnp.where(i_local == k, rpb_ref[base[k] + u], s)
            val = jnp.where(off == u, s, val)
        o_ref[:, t * LANES:(t + 1) * LANES] = jnp.where(inside, val, NEG_BIG)


def _softmax_pv(s_list, v_list):
    m = s_list[0].max(axis=-1, keepdims=True)
    for s in s_list[1:]:
        m = jnp.maximum(m, s.max(axis=-1, keepdims=True))
    den = 0.0
    acc = 0.0
    for s, vv in zip(s_list, v_list):
        p = jnp.exp(s - m)
        den = den + p.sum(axis=-1, keepdims=True)
        acc = acc + _dot(p.astype(BF16), vv)
    return acc / den


def _na_kernel(q_ref, k_ref, v_ref, kc_ref, vc_ref, bias_ref, o_ref, *, rows, kr):
    w = GRID_W
    lane = lax.broadcasted_iota(jnp.int32, (w, LANES), 1)
    first = lane < HEAD_DIM
    kc = kc_ref[...]
    vc = vc_ref[...]
    scale = HEAD_DIM ** -0.5

    rb = NA_ROWS_PER_STEP if rows % NA_ROWS_PER_STEP == 0 else 1

    def body(it, carry):
        rr = [it * rb + k for k in range(rb)]
        r0 = [jnp.clip(r - kr // 2, 0, rows - kr) for r in rr]
        q = [(q_ref[pl.ds(pl.multiple_of(r * w, w), w), :].astype(F32) * scale).astype(BF16) for r in rr]
        kw = [k_ref[pl.ds(pl.multiple_of(r * w, w), kr * w), :] for r in r0]
        vw = [v_ref[pl.ds(pl.multiple_of(r * w, w), kr * w), :] for r in r0]
        chains = [(k, hh) for k in range(rb) for hh in range(2)]
        qm = {(k, hh): jnp.where(first if hh == 0 else jnp.logical_not(first), q[k], jnp.zeros_like(q[k]))
              for k, hh in chains}
        s_loc = {c: _dot_nt(qm[c], kw[c[0]]) + bias_ref[c[1], pl.ds(rr[c[0]] - r0[c[0]], 1)][0] for c in chains}
        s_ctx = {c: _dot_nt(qm[c], kc) for c in chains}
        mx = {c: jnp.maximum(s_loc[c].max(axis=-1, keepdims=True), s_ctx[c].max(axis=-1, keepdims=True))
              for c in chains}
        p_loc = {c: jnp.exp(s_loc[c] - mx[c]) for c in chains}
        p_ctx = {c: jnp.exp(s_ctx[c] - mx[c]) for c in chains}
        den = {c: p_loc[c].sum(axis=-1, keepdims=True) + p_ctx[c].sum(axis=-1, keepdims=True) for c in chains}
        acc = {c: _dot(p_loc[c].astype(BF16), vw[c[0]]) + _dot(p_ctx[c].astype(BF16), vc) for c in chains}
        for k in range(rb):
            out = jnp.where(first, acc[(k, 0)] / den[(k, 0)], acc[(k, 1)] / den[(k, 1)])
            o_ref[pl.ds(pl.multiple_of(rr[k] * w, w), w), :] = out.astype(o_ref.dtype)
        return carry

    lax.fori_loop(0, rows // rb, body, 0)


def _ctx_attn_kernel(q_ref, k_ref, v_ref, o_ref):
    n = q_ref.shape[0]
    lane = lax.broadcasted_iota(jnp.int32, (n, LANES), 1)
    first = lane < HEAD_DIM
    q = (q_ref[...].astype(F32) * HEAD_DIM ** -0.5).astype(BF16)
    k = k_ref[...]
    v = v_ref[...]
    outs = []
    for hh in range(2):
        mh = first if hh == 0 else jnp.logical_not(first)
        qm = jnp.where(mh, q, jnp.zeros_like(q))
        outs.append(_softmax_pv([_dot_nt(qm, k)], [v]))
    o_ref[...] = jnp.where(first, outs[0], outs[1]).astype(o_ref.dtype)


def _attention(qkv_x, qkv_c, rpb, n_seg, need_ctx):
    tx, w3 = qkv_x.shape
    bw = w3 // 3
    npair = bw // LANES
    lx = tx // n_seg
    lc = qkv_c.shape[0] // n_seg
    rows = lx // GRID_W
    kr = min(WIN_R, rows)
    bias = _na_bias(rpb, rows)
    blk = lambda length, off: pl.BlockSpec((length, LANES), lambda bi, p: (bi, off * npair + p))
    nx = pl.pallas_call(
        functools.partial(_na_kernel, rows=rows, kr=kr),
        grid=(n_seg, npair),
        in_specs=[blk(lx, 0), blk(lx, 1), blk(lx, 2), blk(lc, 1), blk(lc, 2),
                  pl.BlockSpec((2, kr, GRID_W, kr * GRID_W), lambda bi, p: (p, 0, 0, 0))],
        out_specs=pl.BlockSpec((lx, LANES), lambda bi, p: (bi, p)),
        out_shape=jax.ShapeDtypeStruct((tx, bw), BF16),
        compiler_params=_params(("parallel", "parallel")),
        name="neighbourhood_attention",
    )(qkv_x, qkv_x, qkv_x, qkv_c, qkv_c, bias)
    ncx = None
    if need_ctx:
        ncx = pl.pallas_call(
            _ctx_attn_kernel,
            grid=(n_seg, npair),
            in_specs=[blk(lc, 0), blk(lc, 1), blk(lc, 2)],
            out_specs=pl.BlockSpec((lc, LANES), lambda bi, p: (bi, p)),
            out_shape=jax.ShapeDtypeStruct((qkv_c.shape[0], bw), BF16),
            compiler_params=_params(("parallel", "parallel")),
            name="context_attention",
        )(qkv_c, qkv_c, qkv_c)
    return nx, ncx


def _even_mixer(hx, hc, n_seg, w_in, shift_mu, w0, w_up, a0, a_up, g_up, k_k, k_a, r_k, ln_w, ln_b, rpb, w_out,
                need_ctx):
    d = hx.shape[1]
    a_width = k_k.shape[0]
    p = _rwkv_params(shift_mu, w0, w_up, a0, a_up, g_up, k_k, k_a, r_k, a_width)
    dr2, ar2, gr = 2 * p["dr"], 2 * p["ar"], p["gr"]
    a_cols = 3 * a_width + dr2 + ar2 + gr
    w_rkv = w_in[:, :3 * a_width].astype(BF16)
    w_lora = _pad_lora_cols(w_in[:, 3 * a_width:a_cols], dr2, ar2, gr, p).astype(BF16)
    w_qkv = w_in[:, a_cols:].astype(BF16)
    mu_rkv = shift_mu[:3 * a_width].reshape(1, -1)
    mu_lora = _pad_lora_cols(shift_mu[3 * a_width:], dr2, ar2, gr, p).reshape(1, -1)
    w_out_b = w_out.astype(BF16)

    def rwkv_side(h, s0):
        seg = h.shape[0] // n_seg
        r, v, kk, lw, kd, b, g, bv = _rwkv_prep(_matmul(h, w_rkv), _matmul(h, w_lora), mu_rkv, mu_lora, p, seg)
        y, s_fin = _rwkv_scan(r, v, kk, lw, kd, b, s0, n_seg)
        return (y, g, bv), s_fin

    s_zero = jnp.zeros((n_seg, 2, a_width // LANES, LANES, LANES), F32)
    terms_c, s_ctx = rwkv_side(hc, s_zero)
    terms_x, _ = rwkv_side(hx, s_ctx)
    qkv_x = _matmul(hx, w_qkv, out_dtype=BF16)
    qkv_c = _matmul(hc, w_qkv, out_dtype=BF16)
    nx, ncx = _attention(qkv_x, qkv_c, rpb, n_seg, need_ctx)
    rx = _rwkv_out(*terms_x, ln_w, ln_b)
    ox = _matmul(jnp.concatenate([rx, nx], axis=1), w_out_b)
    oc = None
    if need_ctx:
        rc = _rwkv_out(*terms_c, ln_w, ln_b)
        oc = _matmul(jnp.concatenate([rc, ncx], axis=1), w_out_b)
    return ox, oc


def _softplus(x):
    return jnp.maximum(x, 0.0) + jnp.log(1.0 + jnp.exp(-jnp.abs(x)))


def _ssd_scan_kernel(x_ref, b_ref, c_ref, dt_ref, bias_ref, alog_ref, s0_ref, y_ref, sf_ref, s_scr, *,
                     n_chunks, n_e, n_grp):
    d = pl.program_id(1)
    gb = pl.program_id(2)
    c = pl.program_id(3)
    cl = x_ref.shape[0]
    hp = dt_ref.shape[1]
    ep = n_e * SSM_HEAD_DIM
    grp = range(n_grp)

    @pl.when(c == 0)
    def _():
        s_scr[...] = s0_ref[...]

    sgn = 1 - 2 * d
    row = lax.broadcasted_iota(jnp.int32, (cl, cl), 0)
    col = lax.broadcasted_iota(jnp.int32, (cl, cl), 1)
    incl = (row - col) * sgn >= 0
    tri = jnp.where(incl, 1.0, 0.0).astype(BF16)
    dt_all = _softplus(dt_ref[...] + bias_ref[...])
    dta_all = dt_all * (-jnp.exp(alog_ref[...]))
    dt_parts = _split3(dt_all)
    dta_parts = _split3(dta_all)
    hr = lax.broadcasted_iota(jnp.int32, (hp, LANES), 0)
    hc = lax.broadcasted_iota(jnp.int32, (hp, LANES), 1)
    sel = [jnp.where(jnp.logical_and(hr == (gb * n_grp + gi) * n_e + hc, hc < n_e), 1.0, 0.0).astype(BF16)
           for gi in grp]
    dt_g = [sum(_dot(p, sel[gi]) for p in dt_parts) for gi in grp]
    dta_g = [sum(_dot(p, sel[gi]) for p in dta_parts) for gi in grp]
    dta_split = [_split3(dta_g[gi]) for gi in grp]
    cum = [sum(_dot(tri, p) for p in dta_split[gi]) for gi in grp]
    bm = [b_ref[:, gi * SSM_STATE:(gi + 1) * SSM_STATE].astype(BF16) for gi in grp]
    cm = [c_ref[:, gi * SSM_STATE:(gi + 1) * SSM_STATE].astype(BF16) for gi in grp]
    cb = [_dot_nt(cm[gi], bm[gi]) for gi in grp]
    state = [s_scr[gi] for gi in grp]
    y_state = [_dot(cm[gi], state[gi].astype(BF16)) for gi in grp]
    cum_t = [cum[gi].T for gi in grp]
    dt_t = [dt_g[gi].T for gi in grp]
    cum_last = [jnp.where(d == 0, cum[gi][cl - 1:cl, :], cum[gi][0:1, :]) for gi in grp]
    e_cum = [jnp.exp(cum[gi]) for gi in grp]
    dt_end = [dt_g[gi] * jnp.exp(cum_last[gi] - cum[gi]) for gi in grp]
    e_last = [jnp.exp(cum_last[gi]) for gi in grp]
    first = lax.broadcasted_iota(jnp.int32, (cl, LANES), 1) < SSM_HEAD_DIM
    first_row = first[0:1, :]
    er = lax.broadcasted_iota(jnp.int32, (LANES, ep), 0)
    ec = lax.broadcasted_iota(jnp.int32, (LANES, ep), 1)
    spread = jnp.where(ec // SSM_HEAD_DIM == er, 1.0, 0.0).astype(BF16)

    def spread_heads(t):
        hi = t.astype(BF16)
        lo = (t - hi.astype(F32)).astype(BF16)
        return _dot(hi, spread) + _dot(lo, spread)

    e_cum_x = [spread_heads(e_cum[gi]) for gi in grp]
    dt_end_x = [spread_heads(dt_end[gi]) for gi in grp]
    x_end = [[] for _ in grp]
    decays = [[] for _ in grp]
    for q in range(n_e // 2):
        j0, j1 = 2 * q, 2 * q + 1
        xq = [x_ref[:, gi * ep + q * LANES:gi * ep + (q + 1) * LANES] for gi in grp]
        x_b = [xq[gi].astype(BF16) for gi in grp]
        m = {(gi, j): (cb[gi] * jnp.exp(jnp.where(incl, cum[gi][:, j:j + 1] - cum_t[gi][j:j + 1, :], NEG_BIG))
                       * dt_t[gi][j:j + 1, :]).astype(BF16) for gi in grp for j in (j0, j1)}
        ys = {k: _dot(m[k], x_b[k[0]]) for k in m}
        for gi in grp:
            lo = gi * ep + q * LANES
            qs = slice(q * LANES, (q + 1) * LANES)
            y_ref[:, lo:lo + LANES] = (jnp.where(first, ys[(gi, j0)], ys[(gi, j1)])
                                       + y_state[gi][:, qs] * e_cum_x[gi][:, qs])
            x_end[gi].append((xq[gi] * dt_end_x[gi][:, qs]).astype(BF16))
            decays[gi].append(jnp.where(first_row, e_last[gi][:, j0:j0 + 1], e_last[gi][:, j1:j1 + 1]))
    upd = [_dot_tn(bm[gi], jnp.concatenate(x_end[gi], axis=1)) for gi in grp]
    for gi in grp:
        s_scr[gi] = state[gi] * jnp.concatenate(decays[gi], axis=1) + upd[gi]

    @pl.when(c == n_chunks - 1)
    def _():
        sf_ref[...] = s_scr[...]


def _ssd_scan(xbc, dt_raw, dt_bias, a_log, s0, n_seg, inner):
    t = xbc.shape[0]
    seg = t // n_seg
    cl = min(SSM_CHUNK, seg)
    nc = seg // cl
    ep = inner // SSM_GROUPS
    n_e = ep // SSM_HEAD_DIM
    hp = dt_raw.shape[1] // 2
    ng = SSD_GROUPS_PER_STEP
    assert SSM_GROUPS % ng == 0 and (inner // SSM_STATE) % ng == 0
    bc_w = ng * SSM_STATE
    nb = inner // bc_w

    def rows(bi, d, g, c):
        return bi * nc + jnp.where(d == 0, c, nc - 1 - c)

    state = pl.BlockSpec((None, None, ng, SSM_STATE, ep), lambda bi, d, g, c: (bi, d, g, 0, 0))
    par = pl.BlockSpec((None, 1, hp), lambda bi, d, g, c: (d, 0, 0))
    return pl.pallas_call(
        functools.partial(_ssd_scan_kernel, n_chunks=nc, n_e=n_e, n_grp=ng),
        grid=(n_seg, 2, SSM_GROUPS // ng, nc),
        in_specs=[pl.BlockSpec((cl, ng * ep), lambda bi, d, g, c: (rows(bi, d, g, c), g)),
                  pl.BlockSpec((cl, bc_w), lambda bi, d, g, c: (rows(bi, d, g, c), nb + g)),
                  pl.BlockSpec((cl, bc_w), lambda bi, d, g, c: (rows(bi, d, g, c), nb + SSM_GROUPS // ng + g)),
                  pl.BlockSpec((cl, hp), lambda bi, d, g, c: (rows(bi, d, g, c), d)),
                  par, par, state],
        out_specs=[pl.BlockSpec((None, cl, ng * ep), lambda bi, d, g, c: (d, rows(bi, d, g, c), g)), state],
        out_shape=[jax.ShapeDtypeStruct((2, t, inner), F32), jax.ShapeDtypeStruct(s0.shape, F32)],
        scratch_shapes=[pltpu.VMEM((ng, SSM_STATE, ep), F32)],
        compiler_params=_params(("parallel", "parallel", "parallel", "arbitrary")),
        name="ssd_scan",
    )(xbc, xbc, xbc, dt_raw, dt_bias, a_log, s0)


def _ssm_out_kernel(y_ref, xs_ref, z_ref, dsk_ref, nw_ref, o_ref):
    z = z_ref[...]
    y = (y_ref[0] + y_ref[1] + dsk_ref[...] * xs_ref[...]) * (z * jax.nn.sigmoid(z))
    o_ref[...] = _rms(y, nw_ref[...]).astype(o_ref.dtype)


def _ssm_out(y, xbc, z, d_skip_cols, norm_w):
    _, t, inner = y.shape
    gw = inner // SSM_GROUPS
    tr = _pick(t, (256, 128, 64, 32, 16, 8))
    blk = pl.BlockSpec((tr, gw), lambda i, g: (i, g))
    par = pl.BlockSpec((1, gw), lambda i, g: (0, g))
    return pl.pallas_call(
        _ssm_out_kernel,
        grid=(t // tr, SSM_GROUPS),
        in_specs=[pl.BlockSpec((2, tr, gw), lambda i, g: (0, i, g)), blk, blk, par, par],
        out_specs=blk,
        out_shape=jax.ShapeDtypeStruct((t, inner), BF16),
        compiler_params=_params(("parallel", "parallel")),
        name="ssm_out",
    )(y, xbc, z, d_skip_cols, norm_w.reshape(1, inner))


def _odd_mixer(hx, hc, n_seg, w_in, conv_w, conv_b, dt_bias, a_log, d_skip, norm_w, w_out, need_ctx):
    inner = norm_w.shape[0]
    n_heads = d_skip.shape[0]
    conv_dim = conv_w.shape[1]
    hp = -(-n_heads // LANES) * LANES
    w_z = w_in[:, :inner].astype(BF16)
    w_xbc = w_in[:, inner:inner + conv_dim].astype(BF16)
    w_dt = jnp.pad(w_in[:, inner + conv_dim:].reshape(-1, 2, n_heads),
                   ((0, 0), (0, 0), (0, hp - n_heads))).reshape(-1, 2 * hp).astype(BF16)
    pad_h = lambda p: jnp.pad(p, ((0, 0), (0, hp - n_heads))).reshape(2, 1, hp)
    bias_p, alog_p = pad_h(dt_bias), pad_h(a_log)
    d_cols = jnp.repeat(d_skip, SSM_HEAD_DIM).reshape(1, inner)
    w_out_b = w_out.astype(BF16)
    conv_b2 = conv_b.reshape(1, conv_dim)

    def side(h, s0, need_out):
        seg = h.shape[0] // n_seg
        z = _matmul(h, w_z)
        xbc = _row_neighbour_call(_conv_kernel, _matmul(h, w_xbc), seg, [conv_w, conv_b2], "conv_silu")
        dt_raw = _matmul(h, w_dt)
        y, s_fin = _ssd_scan(xbc, dt_raw, bias_p, alog_p, s0, n_seg, inner)
        out = _matmul(_ssm_out(y, xbc, z, d_cols, norm_w), w_out_b) if need_out else None
        return out, s_fin

    s_zero = jnp.zeros((n_seg, 2, SSM_GROUPS, SSM_STATE, inner // SSM_GROUPS), F32)
    oc, s_ctx = side(hc, s_zero, need_ctx)
    ox, _ = side(hx, s_ctx, True)
    return ox, oc


def kernel(x, c, ctx, c_ctx, ada_w, ada_b, norm_g, ev_w_in, ev_shift_mu, rk_w0, rk_w_up, rk_a0, rk_a_up, rk_g_up,
           rk_k_k, rk_k_a, rk_r_k, rk_ln_w, rk_ln_b, na_rpb, ev_w_out, od_w_in, od_conv_w, od_conv_b, od_dt_bias,
           od_a_log, od_d, od_norm_w, od_w_out, router_w, router_b, moe_w1, moe_b1, moe_w2, moe_b2):
    n_b, n_seq, d = x.shape
    n_ctx = ctx.shape[1]
    depth = ada_w.shape[0]
    xt = x.reshape(n_b * n_seq, d)
    ct = ctx.reshape(n_b * n_ctx, d)
    n_cond = -(-(n_b + 1) // SUBLANES) * SUBLANES
    cond = jnp.zeros((n_cond, d), F32).at[:n_b].set(c).at[n_b].set(c_ctx)
    mods = _adaln(cond, ada_w, ada_b)
    hx = hc = None
    for layer in range(depth):
        need_ctx = layer < depth - 1
        i = layer // 2
        mod_x = mods[layer, :n_b].reshape(n_b, 6, d)
        mod_c = jnp.broadcast_to(mods[layer, n_b].reshape(1, 6, d), (n_b, 6, d))
        g = norm_g[layer]
        if layer == 0:
            hx = _norm_mod(xt, g[0], mod_x, 0, 1)
            hc = _norm_mod(ct, g[0], mod_c, 0, 1)
        if layer % 2 == 0:
            ox, oc = _even_mixer(hx, hc, n_b, ev_w_in[i], ev_shift_mu[i], rk_w0[i], rk_w_up[i], rk_a0[i],
                                 rk_a_up[i], rk_g_up[i], rk_k_k[i], rk_k_a[i], rk_r_k[i], rk_ln_w[i], rk_ln_b[i],
                                 na_rpb[i], ev_w_out[i], need_ctx)
        else:
            ox, oc = _odd_mixer(hx, hc, n_b, od_w_in[i], od_conv_w[i], od_conv_b[i], od_dt_bias[i], od_a_log[i],
                                od_d[i], od_norm_w[i], od_w_out[i], need_ctx)
        router = (router_w[layer], router_b[layer])
        experts = _moe_weights(moe_w1, layer, moe_b1[layer], moe_w2[layer], moe_b2[layer])
        streams = [(xt, ox, mod_x)] + ([(ct, oc, mod_c)] if need_ctx else [])
        new = []
        for tok, o, mod in streams:
            tok, h2, top_idx, gate = _residual(tok, o, g[1], mod, 2, g_b=g[2], i_shift=3, i_scale=4, router=router)
            slots = _moe(h2, top_idx, *experts)
            if layer + 1 < depth:
                g_next = norm_g[layer + 1]
                mod_next = (mods[layer + 1, :n_b].reshape(n_b, 6, d) if mod is mod_x else
                            jnp.broadcast_to(mods[layer + 1, n_b].reshape(1, 6, d), (n_b, 6, d)))
                tok, h_next = _residual(tok, slots, g[3], mod, 5, n_sum=TOP_K, g_b=g_next[0], i_shift=0, i_scale=1,
                                        h_dtype=BF16, mod_h=mod_next, slot_gate=gate)
            else:
                (tok,) = _residual(tok, slots, g[3], mod, 5, n_sum=TOP_K, slot_gate=gate)
                h_next = None
            new.append((tok, h_next))
        xt, hx = new[0]
        if need_ctx:
            ct, hc = new[1]
    return xt.reshape(n_b, n_seq, d)
```

```python
import functools
import math

import numpy as np
import jax
import jax.numpy as jnp
from jax import lax
from jax.experimental import pallas as pl
from jax.experimental.pallas import tpu as pltpu

F32 = jnp.float32
BF16 = jnp.bfloat16

RMS_EPS = 1e-6
GN_EPS = 64e-5
HEAD_DIM = 64
GRID_W = 64
WIN_R = 8
WIN_C = 16
SSM_HEAD_DIM = 64
SSM_GROUPS = 8
SSM_STATE = 128
SSM_CHUNK = 128
TOP_K = 4
SWIGLU_ALPHA = 1.702
SWIGLU_LIMIT = 7.0
MOE_ROWS = 256
RWKV_CHUNK = 64
MM_TM = (1024, 512, 256, 128, 64, 32, 16, 8)
MM_TN = (1024, 768, 640, 512, 384, 256, 128)
MM_TK = (4096, 2048, 1024, 512, 256, 128)
SSD_GROUPS_PER_STEP = 4
NA_ROWS_PER_STEP = 16
RWKV_PAIRS_PER_STEP = 16
LANES = 128
SUBLANES = 8
VMEM_LIMIT = 56 * 1024 * 1024
NEG_BIG = -1e30


def _pick(n, cands):
    for c in cands:
        if n % c == 0:
            return c
    return n


def _params(sem, vmem=VMEM_LIMIT):
    return pltpu.CompilerParams(dimension_semantics=sem, vmem_limit_bytes=vmem)


def _split3(x):
    hi = x.astype(BF16)
    r1 = x - hi.astype(F32)
    mid = r1.astype(BF16)
    lo = (r1 - mid.astype(F32)).astype(BF16)
    return hi, mid, lo


def _dot(a, b):
    return jnp.dot(a, b, preferred_element_type=F32)


def _dot_nt(a, b):
    return lax.dot_general(a, b, (((1,), (1,)), ((), ())), preferred_element_type=F32)


def _dot_tn(a, b):
    return lax.dot_general(a, b, (((0,), (0,)), ((), ())), preferred_element_type=F32)


def _dot_exact_rhs(x, m_bf16):
    hi, mid, lo = _split3(x)
    return _dot(hi, m_bf16) + _dot(mid, m_bf16) + _dot(lo, m_bf16)


def _mm_kernel(a_ref, w_ref, b_ref, o_ref, acc_ref, *, nk, pre):
    if nk == 1:
        a = a_ref[...]
        if pre == "silu":
            a = a * jax.nn.sigmoid(a)
        o_ref[...] = (_dot(a.astype(BF16), w_ref[...].astype(BF16)) + b_ref[...]).astype(o_ref.dtype)
        return
    k = pl.program_id(2)

    @pl.when(k == 0)
    def _():
        acc_ref[...] = jnp.zeros_like(acc_ref)

    a = a_ref[...]
    if pre == "silu":
        a = a * jax.nn.sigmoid(a)
    acc_ref[...] += _dot(a.astype(BF16), w_ref[...].astype(BF16))

    @pl.when(k == nk - 1)
    def _():
        o_ref[...] = (acc_ref[...] + b_ref[...]).astype(o_ref.dtype)


def _matmul(a, w, bias=None, out_dtype=F32, pre=None, rows=None):
    start, m = (0, a.shape[0]) if rows is None else rows
    kdim = a.shape[1]
    n = w.shape[1]
    tm = _pick(math.gcd(m, start) if start else m, MM_TM)
    tn = _pick(n, MM_TN)
    tk = _pick(kdim, MM_TK)
    nk = kdim // tk
    first = start // tm
    if bias is None:
        bias = jnp.zeros((n,), F32)
    return pl.pallas_call(
        functools.partial(_mm_kernel, nk=nk, pre=pre),
        grid=(m // tm, n // tn, nk),
        in_specs=[pl.BlockSpec((tm, tk), lambda i, j, k: (first + i, k)),
                  pl.BlockSpec((tk, tn), lambda i, j, k: (k, j)),
                  pl.BlockSpec((1, tn), lambda i, j, k: (0, j))],
        out_specs=pl.BlockSpec((tm, tn), lambda i, j, k: (i, j)),
        out_shape=jax.ShapeDtypeStruct((m, n), out_dtype),
        scratch_shapes=[pltpu.VMEM((tm, tn), F32)],
        compiler_params=_params(("parallel", "parallel", "arbitrary")),
        name="matmul",
    )(a, w, bias.reshape(1, n).astype(F32))


def _adaln_kernel(c_ref, w_ref, b_ref, o_ref, acc_ref, *, nk):
    k = pl.program_id(2)

    @pl.when(k == 0)
    def _():
        acc_ref[...] = jnp.zeros_like(acc_ref)

    c = c_ref[...]
    c = c * jax.nn.sigmoid(c)
    acc_ref[...] += _dot(c.astype(BF16), w_ref[...].astype(BF16))

    @pl.when(k == nk - 1)
    def _():
        o_ref[...] = acc_ref[...] + b_ref[...]


def _adaln(cond, ada_w, ada_b):
    depth, d, n = ada_w.shape
    r = cond.shape[0]
    tn = _pick(n, (2048, 1024, 512, 256, 128))
    tk = _pick(d, (1024, 512, 256, 128))
    nk = d // tk
    return pl.pallas_call(
        functools.partial(_adaln_kernel, nk=nk),
        grid=(depth, n // tn, nk),
        in_specs=[pl.BlockSpec((r, tk), lambda l, j, k: (0, k)),
                  pl.BlockSpec((None, tk, tn), lambda l, j, k: (l, k, j)),
                  pl.BlockSpec((None, 1, tn), lambda l, j, k: (l, 0, j))],
        out_specs=pl.BlockSpec((None, r, tn), lambda l, j, k: (l, 0, j)),
        out_shape=jax.ShapeDtypeStruct((depth, r, n), F32),
        scratch_shapes=[pltpu.VMEM((r, tn), F32)],
        compiler_params=_params(("parallel", "parallel", "arbitrary")),
        name="adaln",
    )(cond, ada_w, ada_b.reshape(depth, 1, n))


def _rms(x, g):
    return x * lax.rsqrt(jnp.mean(x * x, axis=-1, keepdims=True) + RMS_EPS) * g


def _norm_mod_kernel(x_ref, g_ref, mod_ref, o_ref, *, i_shift, i_scale):
    y = _rms(x_ref[...], g_ref[...])
    o = y * (1.0 + mod_ref[i_scale:i_scale + 1, :]) + mod_ref[i_shift:i_shift + 1, :]
    o_ref[...] = o.astype(o_ref.dtype)


def _norm_mod(x, g, mod, i_shift, i_scale, out_dtype=BF16):
    t, d = x.shape
    seg = t // mod.shape[0]
    tr = _pick(seg, (256, 128, 64, 32, 16, 8))
    per = seg // tr
    return pl.pallas_call(
        functools.partial(_norm_mod_kernel, i_shift=i_shift, i_scale=i_scale),
        grid=(t // tr,),
        in_specs=[pl.BlockSpec((tr, d), lambda i: (i, 0)),
                  pl.BlockSpec((1, d), lambda i: (0, 0)),
                  pl.BlockSpec((None, 6, d), lambda i: (i // per, 0, 0))],
        out_specs=pl.BlockSpec((tr, d), lambda i: (i, 0)),
        out_shape=jax.ShapeDtypeStruct((t, d), out_dtype),
        compiler_params=_params(("parallel",)),
        name="norm_mod",
    )(x, g.reshape(1, d), mod)


def _residual_kernel(*refs, n_sum, i_gate, i_shift, i_scale, with_h, with_router, n_exp):
    x_ref = refs[0]
    o_refs = refs[1:1 + n_sum]
    ga_ref, mod_ref = refs[1 + n_sum:3 + n_sum]
    pos = 3 + n_sum
    if n_sum > 1:
        sg_ref = refs[pos]
        pos += 1
    if with_h:
        gb_ref, modh_ref = refs[pos:pos + 2]
        pos += 2
    if with_router:
        rw_ref, rb_ref = refs[pos:pos + 2]
        pos += 2
    xo_ref = refs[pos]
    pos += 1
    if n_sum == 1:
        o = o_refs[0][...]
    else:
        sg = sg_ref[...]
        o = o_refs[0][...] * sg[:, 0:1]
        for j in range(1, n_sum):
            o = o + o_refs[j][...] * sg[:, j:j + 1]
    xn =x_ref[...] + mod_ref[i_gate:i_gate + 1, :] * _rms(o, ga_ref[...])
    xo_ref[...] = xn
    if not with_h:
        return
    h_ref = refs[pos]
    pos += 1
    h = _rms(xn, gb_ref[...]) * (1.0 + modh_ref[i_scale:i_scale + 1, :]) + modh_ref[i_shift:i_shift + 1, :]
    h_ref[...] = h.astype(h_ref.dtype)
    if not with_router:
        return
    idx_ref, gate_ref = refs[pos:pos + 2]
    h1, h2, h3 = _split3(h)
    w1 = rw_ref[0]
    w2 = rw_ref[1]
    logits = (_dot(h1, w1) + _dot(h2, w1) + _dot(h1, w2) + _dot(h3, w1) + _dot(h2, w2)) + rb_ref[...]
    lane = lax.broadcasted_iota(jnp.int32, logits.shape, 1)
    logits = jnp.where(lane < n_exp, logits, NEG_BIG)
    vals, idxs = [], []
    for _ in range(TOP_K):
        m = jnp.max(logits, axis=-1, keepdims=True)
        ix = jnp.min(jnp.where(logits == m, lane, 1 << 30), axis=-1, keepdims=True)
        vals.append(m)
        idxs.append(ix)
        logits = jnp.where(lane == ix, NEG_BIG * 2, logits)
    es = [jnp.exp(v - vals[0]) for v in vals]
    den = es[0]
    for e in es[1:]:
        den = den + e
    idx_out = jnp.zeros(lane.shape, jnp.int32)
    gate_out = jnp.zeros(lane.shape, F32)
    for j in range(TOP_K):
        idx_out = jnp.where(lane == j, idxs[j], idx_out)
        gate_out = jnp.where(lane == j, es[j] / den, gate_out)
    idx_ref[...] = idx_out
    gate_ref[...] = gate_out


def _residual(x, o, g_a, mod, i_gate, n_sum=1, g_b=None, i_shift=0, i_scale=0, h_dtype=F32, router=None,
              mod_h=None, slot_gate=None):
    t, d = x.shape
    seg = t // mod.shape[0]
    tr = _pick(seg, (128, 64, 32, 16, 8))
    per = seg // tr
    nt = t // tr
    with_h = g_b is not None
    with_router = router is not None
    row = lambda i: (i, 0)
    fixed = lambda i: (0, 0)
    ins = [x] + [o] * n_sum + [g_a.reshape(1, d), mod]
    in_specs = ([pl.BlockSpec((tr, d), row)]
                + [pl.BlockSpec((tr, d), functools.partial(lambda j, i: (j * nt + i, 0), j)) for j in range(n_sum)]
                + [pl.BlockSpec((1, d), fixed), pl.BlockSpec((None, 6, d), lambda i: (i // per, 0, 0))])
    outs = [jax.ShapeDtypeStruct((t, d), F32)]
    out_specs = [pl.BlockSpec((tr, d), row)]
    n_exp = 0
    if n_sum > 1:
        ins.append(slot_gate)
        in_specs.append(pl.BlockSpec((tr, slot_gate.shape[1]), row))
    if with_h:
        ins += [g_b.reshape(1, d), mod if mod_h is None else mod_h]
        in_specs += [pl.BlockSpec((1, d), fixed), pl.BlockSpec((None, 6, d), lambda i: (i // per, 0, 0))]
        outs.append(jax.ShapeDtypeStruct((t, d), h_dtype))
        out_specs.append(pl.BlockSpec((tr, d), row))
    if with_router:
        rw, rb = router
        n_exp = rw.shape[1]
        rw = jnp.pad(rw, ((0, 0), (0, LANES - n_exp)))
        rw_hi = rw.astype(BF16)
        rw_lo = (rw - rw_hi.astype(F32)).astype(BF16)
        ins += [jnp.stack([rw_hi, rw_lo]), jnp.pad(rb, (0, LANES - n_exp)).reshape(1, LANES)]
        in_specs += [pl.BlockSpec((2, d, LANES), lambda i: (0, 0, 0)), pl.BlockSpec((1, LANES), fixed)]
        outs += [jax.ShapeDtypeStruct((t, LANES), jnp.int32), jax.ShapeDtypeStruct((t, LANES), F32)]
        out_specs += [pl.BlockSpec((tr, LANES), row), pl.BlockSpec((tr, LANES), row)]
    res = pl.pallas_call(
        functools.partial(_residual_kernel, n_sum=n_sum, i_gate=i_gate, i_shift=i_shift, i_scale=i_scale,
                          with_h=with_h, with_router=with_router, n_exp=n_exp),
        grid=(t // tr,),
        in_specs=in_specs,
        out_specs=out_specs,
        out_shape=outs,
        compiler_params=_params(("parallel",)),
        name="residual",
    )(*ins)
    return res


def _moe_kernel(nb_ref, be_ref, qs_ref, cnt_ref, order_ref, h_hbm, w1_ref, b1_ref, w2_ref, b2_ref, out_hbm,
                xbuf_a, xbuf_b, obuf_a, obuf_b, sem_in, sem_out, *, bm, n_tok, f):
    del be_ref
    i = pl.program_id(0)
    nb = nb_ref[0]
    xbuf = (xbuf_a, xbuf_b)
    obuf = (obuf_a, obuf_b)
    shift = TOP_K.bit_length() - 1
    unroll = 8

    def gather_row(q0, buf, r):
        tok = lax.shift_right_logical(order_ref[q0 + r], shift)
        return pltpu.make_async_copy(h_hbm.at[pl.ds(tok, 1)], xbuf[buf].at[pl.ds(r, 1)], sem_in.at[buf])

    def scatter_row(q0, buf, r):
        a = order_ref[q0 + r]
        slot = (a & (TOP_K - 1)) * n_tok + lax.shift_right_logical(a, shift)
        return pltpu.make_async_copy(obuf[buf].at[pl.ds(r, 1)], out_hbm.at[pl.ds(slot, 1)], sem_out.at[buf])

    def for_rows(fn, n_rows):
        n_full = lax.shift_right_logical(n_rows, unroll.bit_length() - 1)

        def group(gi, c):
            for k in range(unroll):
                fn(gi * unroll + k)
            return c

        def single(r, c):
            fn(r)
            return c

        lax.fori_loop(0, n_full, group, 0)
        lax.fori_loop(n_full * unroll, n_rows, single, 0)

    def wait_gathered(buf):
        pltpu.make_async_copy(h_hbm.at[pl.ds(0, bm)], xbuf[buf], sem_in.at[buf]).wait()

    def wait_scattered(buf, n_rows):
        row = pltpu.make_async_copy(obuf[buf].at[pl.ds(0, 1)], out_hbm.at[pl.ds(0, 1)], sem_out.at[buf])
        for_rows(lambda r: row.wait(), n_rows)

    def step(cur):
        if cur == 0:
            @pl.when(i == 0)
            def _():
                for_rows(lambda r: gather_row(qs_ref[0], 0, r).start(), bm)

        wait_gathered(cur)
        q_next = qs_ref[jnp.minimum(i + 1, nb - 1)]
        for r in range(bm):
            gather_row(q_next, 1 - cur, r).start()
        x = xbuf[cur][...].astype(BF16)
        u = _dot(x, w1_ref[...]) + b1_ref[...]
        glu = jnp.minimum(u[:, :f], SWIGLU_LIMIT)
        lin = jnp.clip(u[:, f:], -SWIGLU_LIMIT, SWIGLU_LIMIT)
        act = glu * jax.nn.sigmoid(SWIGLU_ALPHA * glu) * (lin + 1.0)
        out = _dot(act.astype(BF16), w2_ref[...]) + b2_ref[...]

        @pl.when(i >= 1)
        def _():
            wait_scattered(1 - cur, cnt_ref[jnp.maximum(i - 1, 0)])

        obuf[cur][...] = out
        q_cur = qs_ref[i]
        for_rows(lambda r: scatter_row(q_cur, cur, r).start(), cnt_ref[i])

        @pl.when(i == nb - 1)
        def _():
            wait_scattered(cur, cnt_ref[i])
            wait_gathered(1 - cur)

    for parity in range(2):
        pl.when(jnp.logical_and(i < nb, i % 2 == parity))(functools.partial(step, parity))


def _moe(h, top_idx, w1, b1, w2, b2):
    assert TOP_K & (TOP_K - 1) == 0
    t, d = h.shape
    n_exp, f = w2.shape[0], w2.shape[1]
    bm = MOE_ROWS
    n_assign = t * TOP_K
    flat_e = top_idx[:, :TOP_K].reshape(-1)
    order = jnp.argsort(flat_e, stable=True).astype(jnp.int32)
    counts = jnp.sum(flat_e[:, None] == jnp.arange(n_exp, dtype=flat_e.dtype)[None, :], axis=0, dtype=jnp.int32)
    padded = (counts + bm - 1) // bm * bm
    pad_end = jnp.cumsum(padded)
    pad_start = pad_end - padded
    sort_start = jnp.cumsum(counts) - counts
    n_blocks = -(-(n_assign + n_exp * (bm - 1)) // bm)
    blk_row = jnp.arange(n_blocks, dtype=jnp.int32) * bm
    block_exp = jnp.minimum(jnp.sum(pad_end[None, :] <= blk_row[:, None], axis=1), n_exp - 1).astype(jnp.int32)
    off = blk_row - pad_start[block_exp]
    q_start = (sort_start[block_exp] + off).astype(jnp.int32)
    cnt = jnp.clip(counts[block_exp] - off, 0, bm).astype(jnp.int32)
    n_used = (pad_end[-1:] // bm).astype(jnp.int32)
    pre = lambda i, nb, be, qs, ct, od: (be[i], 0, 0)
    grid_spec = pltpu.PrefetchScalarGridSpec(
        num_scalar_prefetch=5,
        grid=(n_blocks,),
        in_specs=[pl.BlockSpec(memory_space=pl.ANY),
                  pl.BlockSpec((None, d, 2 * f), pre),
                  pl.BlockSpec((None, 1, 2 * f), pre),
                  pl.BlockSpec((None, f, d), pre),
                  pl.BlockSpec((None, 1, d), pre)],
        out_specs=pl.BlockSpec(memory_space=pl.ANY),
        scratch_shapes=[pltpu.VMEM((bm, d), F32), pltpu.VMEM((bm, d), F32),
                        pltpu.VMEM((bm, d), F32), pltpu.VMEM((bm, d), F32),
                        pltpu.SemaphoreType.DMA((2,)), pltpu.SemaphoreType.DMA((2,))],
    )
    return pl.pallas_call(
        functools.partial(_moe_kernel, bm=bm, n_tok=t, f=f),
        grid_spec=grid_spec,
        out_shape=jax.ShapeDtypeStruct((n_assign, d), F32),
        compiler_params=_params(("arbitrary",)),
        name="moe_experts",
    )(n_used, block_exp, q_start, cnt, jnp.pad(order, (0, bm)), h, w1, b1, w2, b2)


def _moe_weights(w1_all, layer, b1, w2, b2):
    _, n_exp, d, f2 = w1_all.shape
    perm = np.concatenate([np.arange(0, f2, 2), np.arange(1, f2, 2)])
    pmat = jnp.asarray(np.arange(f2)[:, None] == perm[None, :], BF16)
    w1p = _matmul(w1_all.reshape(-1, f2), pmat, out_dtype=BF16,
                  rows=(layer * n_exp * d, n_exp * d)).reshape(n_exp, d, f2)
    return w1p, b1[:, perm].reshape(n_exp, 1, f2), w2.astype(BF16), b2.reshape(n_exp, 1, -1)


def _row_shift(u, prev8, next8, k):
    n = u.shape[0]
    rolled = pltpu.roll(u, k % n, 0)
    sub = lax.broadcasted_iota(jnp.int32, (SUBLANES, u.shape[1]), 0)
    if k > 0:
        halo = pltpu.roll(prev8, k, 0)
        top = jnp.where(sub < k, halo, rolled[0:SUBLANES])
        return jnp.concatenate([top, rolled[SUBLANES:]], axis=0)
    halo = pltpu.roll(next8, SUBLANES + k, 0)
    bot = jnp.where(sub >= SUBLANES + k, halo, rolled[n - SUBLANES:])
    return jnp.concatenate([rolled[:n - SUBLANES], bot], axis=0)


def _halo_tiles(prev_ref, next_ref, per):
    j = pl.program_id(0) % per
    prev8 = jnp.where(j == 0, 0.0, prev_ref[...])
    next8 = jnp.where(j == per - 1, 0.0, next_ref[...])
    return prev8, next8


def _conv_kernel(u_ref, prev_ref, next_ref, w_ref, b_ref, o_ref, *, per):
    u = u_ref[...]
    prev8, next8 = _halo_tiles(prev_ref, next_ref, per)
    acc = (w_ref[0:1, :] * _row_shift(u, prev8, next8, 2) + w_ref[1:2, :] * _row_shift(u, prev8, next8, 1)
           + w_ref[2:3, :] * u + w_ref[3:4, :] * _row_shift(u, prev8, next8, -1)) + b_ref[...]
    o_ref[...] = acc * jax.nn.sigmoid(acc)


def _row_neighbour_call(kernel_fn, u, seg, extra, name):
    t, n = u.shape
    tr = _pick(seg, (256, 128, 64, 32, 16, 8))
    tc = _pick(n, (2048, 1024, 768, 512, 384, 256, 128))
    per = seg // tr
    r8 = tr // SUBLANES
    last8 = t // SUBLANES - 1
    in_specs = [pl.BlockSpec((tr, tc), lambda i, j: (i, j)),
                pl.BlockSpec((SUBLANES, tc), lambda i, j: (jnp.maximum(i * r8 - 1, 0), j)),
                pl.BlockSpec((SUBLANES, tc), lambda i, j: (jnp.minimum((i + 1) * r8, last8), j))]
    for e in extra:
        in_specs.append(pl.BlockSpec((e.shape[0], tc), lambda i, j: (0, j)))
    return pl.pallas_call(
        functools.partial(kernel_fn, per=per),
        grid=(t // tr, n // tc),
        in_specs=in_specs,
        out_specs=pl.BlockSpec((tr, tc), lambda i, j: (i, j)),
        out_shape=jax.ShapeDtypeStruct((t, n), F32),
        compiler_params=_params(("parallel", "parallel")),
        name=name,
    )(u, u, u, *extra)


def _head_sum_matrix(n):
    idx = np.arange(n) // HEAD_DIM
    return jnp.asarray(idx[:, None] == idx[None, :], BF16)


def _centred_shift(u_ref, prev_ref, next_ref, mu_ref, per):
    u = u_ref[...]
    prev8, next8 = _halo_tiles(prev_ref, next_ref, per)
    nb = 0.5 * (_row_shift(u, prev8, next8, 1) + _row_shift(u, prev8, next8, -1))
    return u + mu_ref[...] * (nb - u)


def _rwkv_prep_kernel(*refs, wd_w, ad_w, per):
    r, k, v, lo = [_centred_shift(*refs[4 * n:4 * n + 4], per) for n in range(4)]
    (w0_ref, wup_ref, a0_ref, aup_ref, gup_ref, kk_ref_, ka_ref, rk_ref, hs_ref,
     r_o, v_o, kk_o, lw_o, kd_o, b_o, g_o, bv_o) = refs[16:]
    wd =jnp.tanh(lo[:, 0:wd_w]).astype(BF16)
    ad = lo[:, wd_w:wd_w + ad_w].astype(BF16)
    gs = jax.nn.sigmoid(lo[:, wd_w + ad_w:]).astype(BF16)
    hs = hs_ref[...]
    kkr = k * kk_ref_[...]
    ss = _dot_exact_rhs(kkr * kkr, hs)
    kk = kkr * lax.rsqrt(jnp.maximum(ss, 1e-24))
    ksum = jnp.zeros_like(k)
    for d in range(2):
        z = -(w0_ref[d:d + 1, :] + _dot(wd, wup_ref[d]))
        w_log = -(jnp.maximum(z, 0.0) + jnp.log(1.0 + jnp.exp(-jnp.abs(z)))) - 0.5
        lw_o[d] = -jnp.exp(w_log)
        asig = jax.nn.sigmoid(a0_ref[d:d + 1, :] + _dot(ad, aup_ref[d]))
        kd = k * (1.0 + (asig - 1.0) * ka_ref[...])
        kd_o[d] = kd
        b_o[d] = kk * asig
        ksum = ksum + kd
    bonus = _dot_exact_rhs(r * ksum * rk_ref[...], hs)
    r_o[...] = r
    v_o[...] = v
    kk_o[...] = kk
    g_o[...] = _dot(gs, gup_ref[...])
    bv_o[...] = bonus * v


def _rwkv_prep(p_rkv, p_lora, mu_rkv, mu_lora, p, seg):
    t = p_rkv.shape[0]
    a = p_rkv.shape[1] // 3
    nl = p_lora.shape[1]
    tr = _pick(seg, (256, 128, 64, 32, 16, 8))
    tc = _pick(a, (512, 256, 128))
    nj = a // tc
    per = seg // tr
    r8 = tr // SUBLANES
    last8 = t // SUBLANES - 1

    def shifted_specs(width, colmap):
        return [pl.BlockSpec((tr, width), lambda i, j: (i, colmap(j))),
                pl.BlockSpec((SUBLANES, width), lambda i, j: (jnp.maximum(i * r8 - 1, 0), colmap(j))),
                pl.BlockSpec((SUBLANES, width), lambda i, j: (jnp.minimum((i + 1) * r8, last8), colmap(j))),
                pl.BlockSpec((1, width), lambda i, j: (0, colmap(j)))]

    shifted_in = []
    shifted_args = []
    for off in range(3):
        shifted_in += shifted_specs(tc, functools.partial(lambda o, j: o * nj + j, off))
        shifted_args += [p_rkv, p_rkv, p_rkv, mu_rkv]
    shifted_in += shifted_specs(nl, lambda j: 0)
    shifted_args += [p_lora, p_lora, p_lora, mu_lora]
    par = lambda rows: pl.BlockSpec((rows, tc), lambda i, j: (0, j))
    par3 = lambda rows: pl.BlockSpec((2, rows, tc), lambda i, j: (0, 0, j))
    one = jax.ShapeDtypeStruct((t, a), F32)
    two = jax.ShapeDtypeStruct((2, t, a), F32)
    o1 = pl.BlockSpec((tr, tc), lambda i, j: (i, j))
    o2 = pl.BlockSpec((2, tr, tc), lambda i, j: (0, i, j))
    return pl.pallas_call(
        functools.partial(_rwkv_prep_kernel, wd_w=p["wd_w"], ad_w=p["ad_w"], per=per),
        grid=(t // tr, nj),
        in_specs=shifted_in + [par(2), par3(p["wd_w"]), par(2), par3(p["ad_w"]), par(p["g_up"].shape[0]),
                               par(1), par(1), par(1),
                               pl.BlockSpec((tc, tc), lambda i, j: (0, 0))],
        out_specs=[o1, o1, o1, o2, o2, o2, o1, o1],
        out_shape=[one, one, one, two, two, two, one, one],
        compiler_params=_params(("parallel", "parallel")),
        name="rwkv_prep",
    )(*shifted_args, p["w0"], p["w_up"], p["a0"], p["a_up"], p["g_up"], p["k_k"], p["k_a"],
      p["r_k"], _head_sum_matrix(tc))


def _rwkv_chunk_pairs(ins, states, d, masks):
    incl, strict, tri, eye, first, first2, blockdiag = masks
    cl = RWKV_CHUNK
    pairs = range(len(ins))
    heads = [(p, hh) for p in pairs for hh in range(2)]
    splits = [_split3(ins[p][3]) for p in pairs]
    cum = [_dot(tri, h1) + _dot(tri, h2) + _dot(tri, h3) for h1, h2, h3 in splits]
    cum_last, ar, ar_b, bk, bk_end, sb, vb = [], [], [], [], [], [], []
    for p in pairs:
        r, v, kk, lw, kd, b = ins[p]
        cl_p = jnp.where(d == 0, cum[p][cl - 1:cl, :], cum[p][0:1, :])
        ginv = jnp.exp(-cum[p])
        to_end = jnp.exp(cl_p - cum[p])
        at = -kk * jnp.exp(cum[p] - lw)
        rt = r * jnp.exp(cum[p])
        cum_last.append(cl_p)
        bk.append(jnp.concatenate([b * ginv, kd * ginv], axis=0).astype(BF16))
        bk_end.append(jnp.concatenate([b * to_end, kd * to_end], axis=0).astype(BF16))
        ar.append(jnp.concatenate([at, rt], axis=0))
        ar_b.append(ar[p].astype(BF16))
        sb.append(states[p].astype(BF16))
        vb.append(v.astype(BF16))
    ar_h = {(p, hh): jnp.where(first2 if hh == 0 else jnp.logical_not(first2), ar[p], 0.0).astype(BF16)
            for p, hh in heads}
    m = {k: _dot_nt(ar_h[k], bk[k[0]]) for k in heads}
    xs = [_dot_nt(ar_b[p], sb[p]) for p in pairs]
    a_ab = {k: jnp.where(strict, m[k][:cl, :cl], 0.0) for k in heads}
    a_ak = {k: jnp.where(strict, m[k][:cl, cl:], 0.0).astype(BF16) for k in heads}
    rbk = {k: jnp.concatenate([jnp.where(incl, m[k][cl:, :cl], 0.0), jnp.where(incl, m[k][cl:, cl:], 0.0)],
                              axis=1).astype(BF16) for k in heads}
    rhs = {k: (xs[k[0]][:cl] + _dot(a_ak[k], vb[k[0]])).astype(BF16) for k in heads}
    tinv = {k: eye + a_ab[k] for k in heads}
    pw = {k: a_ab[k].astype(BF16) for k in heads}
    for _ in range(int(math.log2(cl)) - 1):
        pw = {k: _dot(pw[k], pw[k]).astype(BF16) for k in heads}
        tinv = {k: tinv[k] + _dot(tinv[k].astype(BF16), pw[k]) for k in heads}
    u_h = {k: _dot(tinv[k].astype(BF16), rhs[k]) for k in heads}
    uv = [jnp.concatenate([jnp.where(first, u_h[(p, 0)], u_h[(p, 1)]), ins[p][1]], axis=0).astype(BF16)
          for p in pairs]
    y_h = {k: _dot(rbk[k], uv[k[0]]) for k in heads}
    upd = [_dot_tn(uv[p], bk_end[p]) for p in pairs]
    out = []
    for p in pairs:
        y = xs[p][cl:] + jnp.where(first, y_h[(p, 0)], y_h[(p, 1)])
        s_new = states[p] * jnp.exp(cum_last[p]) + jnp.where(blockdiag, upd[p], 0.0)
        out.append((y, s_new))
    return out


def _rwkv_scan_kernel(r_ref, v_ref, kk_ref, lw_ref, kd_ref, b_ref, s0_ref, y_ref, sf_ref, s_scr, *, n_chunks,
                      n_pairs):
    d = pl.program_id(1)
    c = pl.program_id(3)
    cl = RWKV_CHUNK

    @pl.when(c == 0)
    def _():
        s_scr[...] = s0_ref[...]

    sgn = 1 - 2 * d
    row = lax.broadcasted_iota(jnp.int32, (cl, cl), 0)
    col = lax.broadcasted_iota(jnp.int32, (cl, cl), 1)
    order = (row - col) * sgn
    incl = order >= 0
    strict = order > 0
    tri = jnp.where(incl, 1.0, 0.0).astype(BF16)
    eye = jnp.where(row == col, 1.0, 0.0)
    first = lax.broadcasted_iota(jnp.int32, (cl, LANES), 1) < HEAD_DIM
    first2 = lax.broadcasted_iota(jnp.int32, (2 * cl, LANES), 1) < HEAD_DIM
    rr = lax.broadcasted_iota(jnp.int32, (LANES, LANES), 0) // HEAD_DIM
    cc = lax.broadcasted_iota(jnp.int32, (LANES, LANES), 1) // HEAD_DIM
    masks = (incl, strict, tri, eye, first, first2, rr == cc)
    ins = []
    for p in range(n_pairs):
        sl = slice(p * LANES, (p + 1) * LANES)
        ins.append((r_ref[:, sl], v_ref[:, sl], kk_ref[:, sl], lw_ref[:, sl], kd_ref[:, sl], b_ref[:, sl]))
    results = _rwkv_chunk_pairs(ins, [s_scr[p] for p in range(n_pairs)], d, masks)
    for p, (y, s_new) in enumerate(results):
        y_ref[:, p * LANES:(p + 1) * LANES] = y
        s_scr[p] = s_new

    @pl.when(c == n_chunks - 1)
    def _():
        sf_ref[...] = s_scr[...]


def _rwkv_scan(r, v, kk, lw, kd, b, s0, n_seg):
    t, a = r.shape
    seg = t // n_seg
    cl = RWKV_CHUNK
    nc = seg // cl
    npair = a // LANES
    pb = _pick(npair, (RWKV_PAIRS_PER_STEP, 4, 2, 1))
    w = pb * LANES

    def rows(bi, d, p, c):
        return bi * nc + jnp.where(d == 0, c, nc - 1 - c)

    shared = pl.BlockSpec((cl, w), lambda bi, d, p, c: (rows(bi, d, p, c), p))
    per_dir = pl.BlockSpec((None, cl, w), lambda bi, d, p, c: (d, rows(bi, d, p, c), p))
    state = pl.BlockSpec((None, None, pb, LANES, LANES), lambda bi, d, p, c: (bi, d, p, 0, 0))
    return pl.pallas_call(
        functools.partial(_rwkv_scan_kernel, n_chunks=nc, n_pairs=pb),
        grid=(n_seg, 2, npair // pb, nc),
        in_specs=[shared, shared, shared, per_dir, per_dir, per_dir, state],
        out_specs=[per_dir, state],
        out_shape=[jax.ShapeDtypeStruct((2, t, a), F32), jax.ShapeDtypeStruct(s0.shape, F32)],
        scratch_shapes=[pltpu.VMEM((pb, LANES, LANES), F32)],
        compiler_params=_params(("parallel", "parallel", "parallel", "arbitrary")),
        name="rwkv_scan",
    )(r, v, kk, lw, kd, b, s0)


def _rwkv_out_kernel(y_ref, g_ref, bv_ref, lnw_ref, lnb_ref, hs_ref, o_ref):
    y = y_ref[0] + y_ref[1]
    hs = hs_ref[...]
    inv = 1.0 / HEAD_DIM
    mu = _dot_exact_rhs(y, hs) * inv
    yc = y - mu
    var = _dot_exact_rhs(yc * yc, hs) * inv
    yn = yc * lax.rsqrt(var + GN_EPS) * lnw_ref[...] + lnb_ref[...]
    o_ref[...] = ((yn + bv_ref[...]) * g_ref[...]).astype(o_ref.dtype)


def _rwkv_out(y, g, bv, ln_w, ln_b):
    _, t, a = y.shape
    tr = _pick(t, (256, 128, 64, 32, 16, 8))
    tc = _pick(a, (512, 256, 128))
    o1 = pl.BlockSpec((tr, tc), lambda i, j: (i, j))
    par = pl.BlockSpec((1, tc), lambda i, j: (0, j))
    return pl.pallas_call(
        _rwkv_out_kernel,
        grid=(t // tr, a // tc),
        in_specs=[pl.BlockSpec((2, tr, tc), lambda i, j: (0, i, j)), o1, o1, par, par,
                  pl.BlockSpec((tc, tc), lambda i, j: (0, 0))],
        out_specs=o1,
        out_shape=jax.ShapeDtypeStruct((t, a), BF16),
        compiler_params=_params(("parallel", "parallel")),
        name="rwkv_out",
    )(y, g, bv, ln_w.reshape(1, a), ln_b.reshape(1, a), _head_sum_matrix(tc))


def _rwkv_params(shift_mu, w0, w_up, a0, a_up, g_up, k_k, k_a, r_k, a_width):
    dr, ar_, gr = w_up.shape[1], a_up.shape[1], g_up.shape[0]
    pad = lambda n: -(-n // LANES) * LANES
    wd_w, ad_w, gd_w = pad(2 * dr), pad(2 * ar_), pad(gr)

    def up(wu, rank, width):
        out = jnp.zeros((2, width, a_width), F32)
        for d in range(2):
            out = out.at[d, d * rank:(d + 1) * rank].set(wu[d])
        return out.astype(BF16)

    return dict(wd_w=wd_w, ad_w=ad_w, gd_w=gd_w, dr=dr, ar=ar_, gr=gr,
                w0=w0, a0=a0, w_up=up(w_up, dr, wd_w), a_up=up(a_up, ar_, ad_w),
                g_up=jnp.pad(g_up, ((0, gd_w - gr), (0, 0))).astype(BF16),
                k_k=k_k.reshape(1, -1), k_a=k_a.reshape(1, -1), r_k=r_k.reshape(1, -1))


def _pad_lora_cols(w, dr2, ar2, gr, p):
    parts = [(w[..., :dr2], p["wd_w"]), (w[..., dr2:dr2 + ar2], p["ad_w"]), (w[..., dr2 + ar2:], p["gd_w"])]
    return jnp.concatenate([jnp.pad(x, [(0, 0)] * (x.ndim - 1) + [(0, wd - x.shape[-1])]) for x, wd in parts],
                           axis=-1)


def _na_bias(rpb, rows):
    kr = min(WIN_R, rows)
    n_heads, n_ro, n_off = rpb.shape
    assert LANES % GRID_W == 0 and (kr * GRID_W) % LANES == 0
    return pl.pallas_call(
        functools.partial(_na_bias_kernel, kr=kr, n_ro=n_ro, n_off=n_off),
        grid=(n_heads, kr),
        in_specs=[pl.BlockSpec(memory_space=pltpu.SMEM)],
        out_specs=pl.BlockSpec((None, None, GRID_W, kr * GRID_W), lambda h, p: (h, p, 0, 0)),
        out_shape=jax.ShapeDtypeStruct((n_heads, kr, GRID_W, kr * GRID_W), F32),
        compiler_params=_params(("parallel", "parallel")),
        name="na_bias_table",
    )(rpb.reshape(-1).astype(F32))


def _na_bias_kernel(rpb_ref, o_ref, *, kr, n_ro, n_off):
    h = pl.program_id(0)
    pat = pl.program_id(1)
    w = GRID_W
    per_tile = LANES // w
    q = lax.broadcasted_iota(jnp.int32, (w, LANES), 0)
    lane = lax.broadcasted_iota(jnp.int32, (w, LANES), 1)
    i_local = lane // w
    c = lane - i_local * w
    off = c - q + (WIN_C - 1)
    start = jnp.clip(q - WIN_C // 2, 0, w - WIN_C)
    inside = jnp.logical_and(c >= start, c < start + WIN_C)
    for t in range(kr // per_tile):
        base = [(h * n_ro + (t * per_tile + k - pat + WIN_R - 1)) * n_off for k in range(per_tile)]
        val = jnp.zeros((w, LANES), F32)
        for u in range(n_off):
            s = rpb_ref[base[per_tile - 1] + u]
            for k in range(per_tile - 2, -1, -1):
                s = jnp.where(i_local == k, rpb_ref[base[k] + u], s)
            val = jnp.where(off == u, s, val)
        o_ref[:, t * LANES:(t + 1) * LANES] = jnp.where(inside, val, NEG_BIG)


def _softmax_pv(s_list, v_list):
    m = s_list[0].max(axis=-1, keepdims=True)
    for s in s_list[1:]:
        m = jnp.maximum(m, s.max(axis=-1, keepdims=True))
    den = 0.0
    acc = 0.0
    for s, vv in zip(s_list, v_list):
        p = jnp.exp(s - m)
        den = den + p.sum(axis=-1, keepdims=True)
        acc = acc + _dot(p.astype(BF16), vv)
    return acc / den


def _na_kernel(q_ref, k_ref, v_ref, kc_ref, vc_ref, bias_ref, o_ref, *, rows, kr):
    w = GRID_W
    lane = lax.broadcasted_iota(jnp.int32, (w, LANES), 1)
    first = lane < HEAD_DIM
    kc = kc_ref[...]
    vc = vc_ref[...]
    scale = HEAD_DIM ** -0.5

    rb = NA_ROWS_PER_STEP if rows % NA_ROWS_PER_STEP == 0 else 1

    def body(it, carry):
        rr = [it * rb + k for k in range(rb)]
        r0 = [jnp.clip(r - kr // 2, 0, rows - kr) for r in rr]
        q = [(q_ref[pl.ds(pl.multiple_of(r * w, w), w), :].astype(F32) * scale).astype(BF16) for r in rr]
        kw = [k_ref[pl.ds(pl.multiple_of(r * w, w), kr * w), :] for r in r0]
        vw = [v_ref[pl.ds(pl.multiple_of(r * w, w), kr * w), :] for r in r0]
        chains = [(k, hh) for k in range(rb) for hh in range(2)]
        qm = {(k, hh): jnp.where(first if hh == 0 else jnp.logical_not(first), q[k], jnp.zeros_like(q[k]))
              for k, hh in chains}
        s_loc = {c: _dot_nt(qm[c], kw[c[0]]) + bias_ref[c[1], pl.ds(rr[c[0]] - r0[c[0]], 1)][0] for c in chains}
        s_ctx = {c: _dot_nt(qm[c], kc) for c in chains}
        mx = {c: jnp.maximum(s_loc[c].max(axis=-1, keepdims=True), s_ctx[c].max(axis=-1, keepdims=True))
              for c in chains}
        p_loc = {c: jnp.exp(s_loc[c] - mx[c]) for c in chains}
        p_ctx = {c: jnp.exp(s_ctx[c] - mx[c]) for c in chains}
        den = {c: p_loc[c].sum(axis=-1, keepdims=True) + p_ctx[c].sum(axis=-1, keepdims=True) for c in chains}
        acc = {c: _dot(p_loc[c].astype(BF16), vw[c[0]]) + _dot(p_ctx[c].astype(BF16), vc) for c in chains}
        for k in range(rb):
            out = jnp.where(first, acc[(k, 0)] / den[(k, 0)], acc[(k, 1)] / den[(k, 1)])
            o_ref[pl.ds(pl.multiple_of(rr[k] * w, w), w), :] = out.astype(o_ref.dtype)
        return carry

    lax.fori_loop(0, rows // rb, body, 0)


def _ctx_attn_kernel(q_ref, k_ref, v_ref, o_ref):
    n = q_ref.shape[0]
    lane = lax.broadcasted_iota(jnp.int32, (n, LANES), 1)
    first = lane < HEAD_DIM
    q = (q_ref[...].astype(F32) * HEAD_DIM ** -0.5).astype(BF16)
    k = k_ref[...]
    v = v_ref[...]
    outs = []
    for hh in range(2):
        mh = first if hh == 0 else jnp.logical_not(first)
        qm = jnp.where(mh, q, jnp.zeros_like(q))
        outs.append(_softmax_pv([_dot_nt(qm, k)], [v]))
    o_ref[...] = jnp.where(first, outs[0], outs[1]).astype(o_ref.dtype)


def _attention(qkv_x, qkv_c, rpb, n_seg, need_ctx):
    tx, w3 = qkv_x.shape
    bw = w3 // 3
    npair = bw // LANES
    lx = tx // n_seg
    lc = qkv_c.shape[0] // n_seg
    rows = lx // GRID_W
    kr = min(WIN_R, rows)
    bias = _na_bias(rpb, rows)
    blk = lambda length, off: pl.BlockSpec((length, LANES), lambda bi, p: (bi, off * npair + p))
    nx = pl.pallas_call(
        functools.partial(_na_kernel, rows=rows, kr=kr),
        grid=(n_seg, npair),
        in_specs=[blk(lx, 0), blk(lx, 1), blk(lx, 2), blk(lc, 1), blk(lc, 2),
                  pl.BlockSpec((2, kr, GRID_W, kr * GRID_W), lambda bi, p: (p, 0, 0, 0))],
        out_specs=pl.BlockSpec((lx, LANES), lambda bi, p: (bi, p)),
        out_shape=jax.ShapeDtypeStruct((tx, bw), BF16),
        compiler_params=_params(("parallel", "parallel")),
        name="neighbourhood_attention",
    )(qkv_x, qkv_x, qkv_x, qkv_c, qkv_c, bias)
    ncx = None
    if need_ctx:
        ncx = pl.pallas_call(
            _ctx_attn_kernel,
            grid=(n_seg, npair),
            in_specs=[blk(lc, 0), blk(lc, 1), blk(lc, 2)],
            out_specs=pl.BlockSpec((lc, LANES), lambda bi, p: (bi, p)),
            out_shape=jax.ShapeDtypeStruct((qkv_c.shape[0], bw), BF16),
            compiler_params=_params(("parallel", "parallel")),
            name="context_attention",
        )(qkv_c, qkv_c, qkv_c)
    return nx, ncx


def _even_mixer(hx, hc, n_seg, w_in, shift_mu, w0, w_up, a0, a_up, g_up, k_k, k_a, r_k, ln_w, ln_b, rpb, w_out,
                need_ctx):
    d = hx.shape[1]
    a_width = k_k.shape[0]
    p = _rwkv_params(shift_mu, w0, w_up, a0, a_up, g_up, k_k, k_a, r_k, a_width)
    dr2, ar2, gr = 2 * p["dr"], 2 * p["ar"], p["gr"]
    a_cols = 3 * a_width + dr2 + ar2 + gr
    w_rkv = w_in[:, :3 * a_width].astype(BF16)
    w_lora = _pad_lora_cols(w_in[:, 3 * a_width:a_cols], dr2, ar2, gr, p).astype(BF16)
    w_qkv = w_in[:, a_cols:].astype(BF16)
    mu_rkv = shift_mu[:3 * a_width].reshape(1, -1)
    mu_lora = _pad_lora_cols(shift_mu[3 * a_width:], dr2, ar2, gr, p).reshape(1, -1)
    w_out_b = w_out.astype(BF16)

    def rwkv_side(h, s0):
        seg = h.shape[0] // n_seg
        r, v, kk, lw, kd, b, g, bv = _rwkv_prep(_matmul(h, w_rkv), _matmul(h, w_lora), mu_rkv, mu_lora, p, seg)
        y, s_fin = _rwkv_scan(r, v, kk, lw, kd, b, s0, n_seg)
        return (y, g, bv), s_fin

    s_zero = jnp.zeros((n_seg, 2, a_width // LANES, LANES, LANES), F32)
    terms_c, s_ctx = rwkv_side(hc, s_zero)
    terms_x, _ = rwkv_side(hx, s_ctx)
    qkv_x = _matmul(hx, w_qkv, out_dtype=BF16)
    qkv_c = _matmul(hc, w_qkv, out_dtype=BF16)
    nx, ncx = _attention(qkv_x, qkv_c, rpb, n_seg, need_ctx)
    rx = _rwkv_out(*terms_x, ln_w, ln_b)
    ox = _matmul(jnp.concatenate([rx, nx], axis=1), w_out_b)
    oc = None
    if need_ctx:
        rc = _rwkv_out(*terms_c, ln_w, ln_b)
        oc = _matmul(jnp.concatenate([rc, ncx], axis=1), w_out_b)
    return ox, oc


def _softplus(x):
    return jnp.maximum(x, 0.0) + jnp.log(1.0 + jnp.exp(-jnp.abs(x)))


def _ssd_scan_kernel(x_ref, b_ref, c_ref, dt_ref, bias_ref, alog_ref, s0_ref, y_ref, sf_ref, s_scr, *,
                     n_chunks, n_e, n_grp):
    d = pl.program_id(1)
    gb = pl.program_id(2)
    c = pl.program_id(3)
    cl = x_ref.shape[0]
    hp = dt_ref.shape[1]
    ep = n_e * SSM_HEAD_DIM
    grp = range(n_grp)

    @pl.when(c == 0)
    def _():
        s_scr[...] = s0_ref[...]

    sgn = 1 - 2 * d
    row = lax.broadcasted_iota(jnp.int32, (cl, cl), 0)
    col = lax.broadcasted_iota(jnp.int32, (cl, cl), 1)
    incl = (row - col) * sgn >= 0
    tri = jnp.where(incl, 1.0, 0.0).astype(BF16)
    dt_all = _softplus(dt_ref[...] + bias_ref[...])
    dta_all = dt_all * (-jnp.exp(alog_ref[...]))
    dt_parts = _split3(dt_all)
    dta_parts = _split3(dta_all)
    hr = lax.broadcasted_iota(jnp.int32, (hp, LANES), 0)
    hc = lax.broadcasted_iota(jnp.int32, (hp, LANES), 1)
    sel = [jnp.where(jnp.logical_and(hr == (gb * n_grp + gi) * n_e + hc, hc < n_e), 1.0, 0.0).astype(BF16)
           for gi in grp]
    dt_g = [sum(_dot(p, sel[gi]) for p in dt_parts) for gi in grp]
    dta_g = [sum(_dot(p, sel[gi]) for p in dta_parts) for gi in grp]
    dta_split = [_split3(dta_g[gi]) for gi in grp]
    cum = [sum(_dot(tri, p) for p in dta_split[gi]) for gi in grp]
    bm = [b_ref[:, gi * SSM_STATE:(gi + 1) * SSM_STATE].astype(BF16) for gi in grp]
    cm = [c_ref[:, gi * SSM_STATE:(gi + 1) * SSM_STATE].astype(BF16) for gi in grp]
    cb = [_dot_nt(cm[gi], bm[gi]) for gi in grp]
    state = [s_scr[gi] for gi in grp]
    y_state = [_dot(cm[gi], state[gi].astype(BF16)) for gi in grp]
    cum_t = [cum[gi].T for gi in grp]
    dt_t = [dt_g[gi].T for gi in grp]
    cum_last = [jnp.where(d == 0, cum[gi][cl - 1:cl, :], cum[gi][0:1, :]) for gi in grp]
    e_cum = [jnp.exp(cum[gi]) for gi in grp]
    dt_end = [dt_g[gi] * jnp.exp(cum_last[gi] - cum[gi]) for gi in grp]
    e_last = [jnp.exp(cum_last[gi]) for gi in grp]
    first = lax.broadcasted_iota(jnp.int32, (cl, LANES), 1) < SSM_HEAD_DIM
    first_row = first[0:1, :]
    er = lax.broadcasted_iota(jnp.int32, (LANES, ep), 0)
    ec = lax.broadcasted_iota(jnp.int32, (LANES, ep), 1)
    spread = jnp.where(ec // SSM_HEAD_DIM == er, 1.0, 0.0).astype(BF16)

    def spread_heads(t):
        hi = t.astype(BF16)
        lo = (t - hi.astype(F32)).astype(BF16)
        return _dot(hi, spread) + _dot(lo, spread)

    e_cum_x = [spread_heads(e_cum[gi]) for gi in grp]
    dt_end_x = [spread_heads(dt_end[gi]) for gi in grp]
    x_end = [[] for _ in grp]
    decays = [[] for _ in grp]
    for q in range(n_e // 2):
        j0, j1 = 2 * q, 2 * q + 1
        xq = [x_ref[:, gi * ep + q * LANES:gi * ep + (q + 1) * LANES] for gi in grp]
        x_b = [xq[gi].astype(BF16) for gi in grp]
        m = {(gi, j): (cb[gi] * jnp.exp(jnp.where(incl, cum[gi][:, j:j + 1] - cum_t[gi][j:j + 1, :], NEG_BIG))
                       * dt_t[gi][j:j + 1, :]).astype(BF16) for gi in grp for j in (j0, j1)}
        ys = {k: _dot(m[k], x_b[k[0]]) for k in m}
        for gi in grp:
            lo = gi * ep + q * LANES
            qs = slice(q * LANES, (q + 1) * LANES)
            y_ref[:, lo:lo + LANES] = (jnp.where(first, ys[(gi, j0)], ys[(gi, j1)])
                                       + y_state[gi][:, qs] * e_cum_x[gi][:, qs])
            x_end[gi].append((xq[gi] * dt_end_x[gi][:, qs]).astype(BF16))
            decays[gi].append(jnp.where(first_row, e_last[gi][:, j0:j0 + 1], e_last[gi][:, j1:j1 + 1]))
    upd = [_dot_tn(bm[gi], jnp.concatenate(x_end[gi], axis=1)) for gi in grp]
    for gi in grp:
        s_scr[gi] = state[gi] * jnp.concatenate(decays[gi], axis=1) + upd[gi]

    @pl.when(c == n_chunks - 1)
    def _():
        sf_ref[...] = s_scr[...]


def _ssd_scan(xbc, dt_raw, dt_bias, a_log, s0, n_seg, inner):
    t = xbc.shape[0]
    seg = t // n_seg
    cl = min(SSM_CHUNK, seg)
    nc = seg // cl
    ep = inner // SSM_GROUPS
    n_e = ep // SSM_HEAD_DIM
    hp = dt_raw.shape[1] // 2
    ng = SSD_GROUPS_PER_STEP
    assert SSM_GROUPS % ng == 0 and (inner // SSM_STATE) % ng == 0
    bc_w = ng * SSM_STATE
    nb = inner // bc_w

    def rows(bi, d, g, c):
        return bi * nc + jnp.where(d == 0, c, nc - 1 - c)

    state = pl.BlockSpec((None, None, ng, SSM_STATE, ep), lambda bi, d, g, c: (bi, d, g, 0, 0))
    par = pl.BlockSpec((None, 1, hp), lambda bi, d, g, c: (d, 0, 0))
    return pl.pallas_call(
        functools.partial(_ssd_scan_kernel, n_chunks=nc, n_e=n_e, n_grp=ng),
        grid=(n_seg, 2, SSM_GROUPS // ng, nc),
        in_specs=[pl.BlockSpec((cl, ng * ep), lambda bi, d, g, c: (rows(bi, d, g, c), g)),
                  pl.BlockSpec((cl, bc_w), lambda bi, d, g, c: (rows(bi, d, g, c), nb + g)),
                  pl.BlockSpec((cl, bc_w), lambda bi, d, g, c: (rows(bi, d, g, c), nb + SSM_GROUPS // ng + g)),
                  pl.BlockSpec((cl, hp), lambda bi, d, g, c: (rows(bi, d, g, c), d)),
                  par, par, state],
        out_specs=[pl.BlockSpec((None, cl, ng * ep), lambda bi, d, g, c: (d, rows(bi, d, g, c), g)), state],
        out_shape=[jax.ShapeDtypeStruct((2, t, inner), F32), jax.ShapeDtypeStruct(s0.shape, F32)],
        scratch_shapes=[pltpu.VMEM((ng, SSM_STATE, ep), F32)],
        compiler_params=_params(("parallel", "parallel", "parallel", "arbitrary")),
        name="ssd_scan",
    )(xbc, xbc, xbc, dt_raw, dt_bias, a_log, s0)


def _ssm_out_kernel(y_ref, xs_ref, z_ref, dsk_ref, nw_ref, o_ref):
    z = z_ref[...]
    y = (y_ref[0] + y_ref[1] + dsk_ref[...] * xs_ref[...]) * (z * jax.nn.sigmoid(z))
    o_ref[...] = _rms(y, nw_ref[...]).astype(o_ref.dtype)


def _ssm_out(y, xbc, z, d_skip_cols, norm_w):
    _, t, inner = y.shape
    gw = inner // SSM_GROUPS
    tr = _pick(t, (256, 128, 64, 32, 16, 8))
    blk = pl.BlockSpec((tr, gw), lambda i, g: (i, g))
    par = pl.BlockSpec((1, gw), lambda i, g: (0, g))
    return pl.pallas_call(
        _ssm_out_kernel,
        grid=(t // tr, SSM_GROUPS),
        in_specs=[pl.BlockSpec((2, tr, gw), lambda i, g: (0, i, g)), blk, blk, par, par],
        out_specs=blk,
        out_shape=jax.ShapeDtypeStruct((t, inner), BF16),
        compiler_params=_params(("parallel", "parallel")),
        name="ssm_out",
    )(y, xbc, z, d_skip_cols, norm_w.reshape(1, inner))


def _odd_mixer(hx, hc, n_seg, w_in, conv_w, conv_b, dt_bias, a_log, d_skip, norm_w, w_out, need_ctx):
    inner = norm_w.shape[0]
    n_heads = d_skip.shape[0]
    conv_dim = conv_w.shape[1]
    hp = -(-n_heads // LANES) * LANES
    w_z = w_in[:, :inner].astype(BF16)
    w_xbc = w_in[:, inner:inner + conv_dim].astype(BF16)
    w_dt = jnp.pad(w_in[:, inner + conv_dim:].reshape(-1, 2, n_heads),
                   ((0, 0), (0, 0), (0, hp - n_heads))).reshape(-1, 2 * hp).astype(BF16)
    pad_h = lambda p: jnp.pad(p, ((0, 0), (0, hp - n_heads))).reshape(2, 1, hp)
    bias_p, alog_p = pad_h(dt_bias), pad_h(a_log)
    d_cols = jnp.repeat(d_skip, SSM_HEAD_DIM).reshape(1, inner)
    w_out_b = w_out.astype(BF16)
    conv_b2 = conv_b.reshape(1, conv_dim)

    def side(h, s0, need_out):
        seg = h.shape[0] // n_seg
        z = _matmul(h, w_z)
        xbc = _row_neighbour_call(_conv_kernel, _matmul(h, w_xbc), seg, [conv_w, conv_b2], "conv_silu")
        dt_raw = _matmul(h, w_dt)
        y, s_fin = _ssd_scan(xbc, dt_raw, bias_p, alog_p, s0, n_seg, inner)
        out = _matmul(_ssm_out(y, xbc, z, d_cols, norm_w), w_out_b) if need_out else None
        return out, s_fin

    s_zero = jnp.zeros((n_seg, 2, SSM_GROUPS, SSM_STATE, inner // SSM_GROUPS), F32)
    oc, s_ctx = side(hc, s_zero, need_ctx)
    ox, _ = side(hx, s_ctx, True)
    return ox, oc


def kernel(x, c, ctx, c_ctx, ada_w, ada_b, norm_g, ev_w_in, ev_shift_mu, rk_w0, rk_w_up, rk_a0, rk_a_up, rk_g_up,
           rk_k_k, rk_k_a, rk_r_k, rk_ln_w, rk_ln_b, na_rpb, ev_w_out, od_w_in, od_conv_w, od_conv_b, od_dt_bias,
           od_a_log, od_d, od_norm_w, od_w_out, router_w, router_b, moe_w1, moe_b1, moe_w2, moe_b2):
    n_b, n_seq, d = x.shape
    n_ctx = ctx.shape[1]
    depth = ada_w.shape[0]
    xt = x.reshape(n_b * n_seq, d)
    ct = ctx.reshape(n_b * n_ctx, d)
    n_cond = -(-(n_b + 1) // SUBLANES) * SUBLANES
    cond = jnp.zeros((n_cond, d), F32).at[:n_b].set(c).at[n_b].set(c_ctx)
    mods = _adaln(cond, ada_w, ada_b)
    hx = hc = None
    for layer in range(depth):
        need_ctx = layer < depth - 1
        i = layer // 2
        mod_x = mods[layer, :n_b].reshape(n_b, 6, d)
        mod_c = jnp.broadcast_to(mods[layer, n_b].reshape(1, 6, d), (n_b, 6, d))
        g = norm_g[layer]
        if layer == 0:
            hx = _norm_mod(xt, g[0], mod_x, 0, 1)
            hc = _norm_mod(ct, g[0], mod_c, 0, 1)
        if layer % 2 == 0:
            ox, oc = _even_mixer(hx, hc, n_b, ev_w_in[i], ev_shift_mu[i], rk_w0[i], rk_w_up[i], rk_a0[i],
                                 rk_a_up[i], rk_g_up[i], rk_k_k[i], rk_k_a[i], rk_r_k[i], rk_ln_w[i], rk_ln_b[i],
                                 na_rpb[i], ev_w_out[i], need_ctx)
        else:
            ox, oc = _odd_mixer(hx, hc, n_b, od_w_in[i], od_conv_w[i], od_conv_b[i], od_dt_bias[i], od_a_log[i],
                                od_d[i], od_norm_w[i], od_w_out[i], need_ctx)
        router = (router_w[layer], router_b[layer])
        experts = _moe_weights(moe_w1, layer, moe_b1[layer], moe_w2[layer], moe_b2[layer])
        streams = [(xt, ox, mod_x)] + ([(ct, oc, mod_c)] if need_ctx else [])
        new = []
        for tok, o, mod in streams:
            tok, h2, top_idx, gate = _residual(tok, o, g[1], mod, 2, g_b=g[2], i_shift=3, i_scale=4, router=router)
            slots = _moe(h2, top_idx, *experts)
            if layer + 1 < depth:
                g_next = norm_g[layer + 1]
                mod_next = (mods[layer + 1, :n_b].reshape(n_b, 6, d) if mod is mod_x else
                            jnp.broadcast_to(mods[layer + 1, n_b].reshape(1, 6, d), (n_b, 6, d)))
                tok, h_next = _residual(tok, slots, g[3], mod, 5, n_sum=TOP_K, g_b=g_next[0], i_shift=0, i_scale=1,
                                        h_dtype=BF16, mod_h=mod_next, slot_gate=gate)
            else:
                (tok,) = _residual(tok, slots, g[3], mod, 5, n_sum=TOP_K, slot_gate=gate)
                h_next = None
            new.append((tok, h_next))
        xt, hx = new[0]
        if need_ctx:
            ct, hc = new[1]
    return xt.reshape(n_b, n_seq, d)
```
